```python
import math
import jax, jax.numpy as jnp
from jax import lax
import numpy as np


D_MODEL = 1024
BATCH = 2
SEQ = 8192
DEPTH = 4

HEAD_DIM = 64
HEADS_PER_GROUP = D_MODEL // 2 // HEAD_DIM
ATTN_PATTERN = ((128, 1), (512, 4), (2048, 16))
N_ATTN_GROUPS = len(ATTN_PATTERN)
ATTN_WIDTH = HEADS_PER_GROUP * HEAD_DIM
QKV_GROUP_WIDTH = N_ATTN_GROUPS * ATTN_WIDTH
BLK = 128
SSM_WIDTH = D_MODEL // 2
SSM_GROUP = 16
SSM_GROUPS = SSM_WIDTH // SSM_GROUP
SSM_STATE = 64
DT_MIN = 1e-3
DT_MAX = 1e-1
D_FF = ((8 * D_MODEL + 3 * 256 - 1) // (3 * 256)) * 256
EPS = 1e-6
IN_COLS = 3 * QKV_GROUP_WIDTH + SSM_WIDTH + 2 * D_MODEL
SPLIT_POINTS = (QKV_GROUP_WIDTH, 2 * QKV_GROUP_WIDTH, 3 * QKV_GROUP_WIDTH,
                3 * QKV_GROUP_WIDTH + SSM_WIDTH, 3 * QKV_GROUP_WIDTH + SSM_WIDTH + D_MODEL)

kernel_name = "hybrid_gated_dilated_attn_s5_swiglu"


def rms_norm(t, gain):
    t32 = t.astype(jnp.float32)
    y = t32 * lax.rsqrt(jnp.mean(t32 * t32, axis=-1, keepdims=True) + EPS) * gain.astype(jnp.float32)
    return y.astype(t.dtype)


def head_rms_norm(t, gain):
    t32 = t.astype(jnp.float32)
    return t32 * lax.rsqrt(jnp.mean(t32 * t32, axis=-1, keepdims=True) + EPS) * gain.astype(jnp.float32)


def dilated_window_attention(q, k, v, window, dilation):
    b_, L, H, E = q.shape
    span = window // dilation
    unit = dilation * BLK
    Lp = -(-L // unit) * unit
    M = Lp // dilation
    nb = M // BLK

    def to_blocks(t):
        t = jnp.pad(t, ((0, 0), (0, Lp - L), (0, 0), (0, 0)))
        t = t.reshape(b_, M, dilation, H, E).transpose(0, 2, 3, 1, 4)
        return t.reshape(b_, dilation, H, nb, BLK, E)

    def with_prev(t):
        prev = jnp.pad(t[:, :, :, :-1], ((0, 0), (0, 0), (0, 0), (1, 0), (0, 0), (0, 0)))
        return jnp.concatenate([prev, t], axis=4)

    qb = to_blocks(q)
    kw = with_prev(to_blocks(k))
    vw = with_prev(to_blocks(v))
    s = jnp.einsum('brhnqe,brhnke->brhnqk', qb, kw) * (HEAD_DIM ** -0.5)
    qi = jnp.arange(BLK)[:, None]
    ki = jnp.arange(2 * BLK)[None, :]
    dist = BLK + qi - ki
    blk = jnp.arange(nb)[:, None, None]
    mask = (dist >= 0) & (dist <= span) & (blk * BLK + ki - BLK >= 0)
    s = jnp.where(mask, s, -jnp.inf)
    m = jnp.max(s, axis=-1, keepdims=True)
    p = jnp.exp(s - m)
    denom = jnp.sum(p, axis=-1, keepdims=True)
    o = jnp.einsum('brhnqk,brhnke->brhnqe', p, vw) / denom
    lse = (m + jnp.log(denom))[..., 0]
    o = o.reshape(b_, dilation, H, M, E).transpose(0, 3, 1, 2, 4).reshape(b_, Lp, H, E)[:, :L]
    lse = lse.reshape(b_, dilation, H, M).transpose(0, 3, 1, 2).reshape(b_, Lp, H)[:, :L]
    return o, lse


def dilated_attention(q, k, v, g_q, g_k):
    b_, L, _ = q.shape
    shape = (b_, L, N_ATTN_GROUPS, HEADS_PER_GROUP, HEAD_DIM)
    q = head_rms_norm(q.reshape(shape), g_q)
    k = head_rms_norm(k.reshape(shape), g_k)
    v = v.reshape(shape).astype(jnp.float32)
    outs, lses = [], []
    for gi, (window, dilation) in enumerate(ATTN_PATTERN):
        o, lse = dilated_window_attention(q[:, :, gi], k[:, :, gi], v[:, :, gi], window, dilation)
        outs.append(o)
        lses.append(lse)
    w = jax.nn.softmax(jnp.stack(lses, axis=0), axis=0)
    out = jnp.sum(w[..., None] * jnp.stack(outs, axis=0), axis=0)
    return out.reshape(b_, L, ATTN_WIDTH)


def _complex_linear_combine(e1, e2):
    a1r, a1i, b1r, b1i = e1
    a2r, a2i, b2r, b2i = e2
    ar = a2r * a1r - a2i * a1i
    ai = a2r * a1i + a2i * a1r
    br = a2r * b1r - a2i * b1i + b2r
    bi = a2r * b1i + a2i * b1r + b2i
    return (ar, ai, br, bi)


def s5_ssm(u, lam_re, lam_im, log_dt, b_re, b_im, c_re, c_im, d_skip):
    b_, L, _ = u.shape
    u = u.astype(jnp.float32)
    lr = lam_re.astype(jnp.float32)
    li = lam_im.astype(jnp.float32)
    dt = jnp.exp(log_dt.astype(jnp.float32))[:, None]
    mag = jnp.exp(lr * dt)
    ang = li * dt
    abar_re = mag * jnp.cos(ang)
    abar_im = mag * jnp.sin(ang)
    nr = abar_re - 1.0
    ni = abar_im
    den = lr * lr + li * li
    cr = ((nr * lr + ni * li) / den)[..., None]
    ci = ((ni * lr - nr * li) / den)[..., None]
    br = b_re.astype(jnp.float32)
    bi = b_im.astype(jnp.float32)
    bbar_re = cr * br - ci * bi
    bbar_im = cr * bi + ci * br
    ug = u.reshape(b_, L, SSM_GROUPS, SSM_GROUP)
    bu_re = jnp.einsum('blgc,gpc->lbgp', ug, bbar_re)
    bu_im = jnp.einsum('blgc,gpc->lbgp', ug, bbar_im)
    a_re = jnp.broadcast_to(abar_re[None, None], (L, 1, SSM_GROUPS, SSM_STATE))
    a_im = jnp.broadcast_to(abar_im[None, None], (L, 1, SSM_GROUPS, SSM_STATE))
    _, _, xr, xi = lax.associative_scan(_complex_linear_combine, (a_re, a_im, bu_re, bu_im), axis=0)
    y = (jnp.einsum('lbgp,gcp->blgc', xr, c_re.astype(jnp.float32))
         - jnp.einsum('lbgp,gcp->blgc', xi, c_im.astype(jnp.float32)))
    return y.reshape(b_, L, SSM_WIDTH) + d_skip.astype(jnp.float32) * u


def setup_inputs(seed: int = 0) -> dict:
    key = jax.random.key(seed)
    ks = jax.random.split(key, 24)
    f32 = jnp.float32

    def normal(k, shape, scale):
        return jax.random.normal(k, shape, f32) * scale

    n_idx = jnp.arange(SSM_STATE, dtype=f32)
    return {
        "x": normal(ks[0], (BATCH, SEQ, D_MODEL), 1.0),
        "g_mix": 1.0 + normal(ks[1], (DEPTH, D_MODEL), 0.02),
        "w_in": normal(ks[2], (DEPTH, D_MODEL, IN_COLS), D_MODEL ** -0.5),
        "g_q": 1.0 + normal(ks[3], (DEPTH, HEAD_DIM), 0.02),
        "g_k": 1.0 + normal(ks[4], (DEPTH, HEAD_DIM), 0.02),
        "w_attn_proj": normal(ks[5], (DEPTH, ATTN_WIDTH, D_MODEL), ATTN_WIDTH ** -0.5),
        "lambda_re": -0.5 + normal(ks[6], (DEPTH, SSM_GROUPS, SSM_STATE), 0.01),
        "lambda_im": math.pi * n_idx + normal(ks[7], (DEPTH, SSM_GROUPS, SSM_STATE), 0.01),
        "log_dt": jax.random.uniform(ks[8], (DEPTH, SSM_GROUPS), f32, math.log(DT_MIN), math.log(DT_MAX)),
        "b_re": normal(ks[9], (DEPTH, SSM_GROUPS, SSM_STATE, SSM_GROUP), (2 * SSM_GROUP) ** -0.5),
        "b_im": normal(ks[10], (DEPTH, SSM_GROUPS, SSM_STATE, SSM_GROUP), (2 * SSM_GROUP) ** -0.5),
        "c_re": normal(ks[11], (DEPTH, SSM_GROUPS, SSM_GROUP, SSM_STATE), (2 * SSM_STATE) ** -0.5),
        "c_im": normal(ks[12], (DEPTH, SSM_GROUPS, SSM_GROUP, SSM_STATE), (2 * SSM_STATE) ** -0.5),
        "d_skip": normal(ks[13], (DEPTH, SSM_WIDTH), 1.0),
        "w_glu_a": normal(ks[14], (DEPTH, SSM_WIDTH, D_MODEL), SSM_WIDTH ** -0.5),
        "w_glu_b": normal(ks[15], (DEPTH, SSM_WIDTH, D_MODEL), SSM_WIDTH ** -0.5),
        "w_out": normal(ks[16], (DEPTH, D_MODEL, D_MODEL), D_MODEL ** -0.5),
        "g_ffn": 1.0 + normal(ks[17], (DEPTH, D_MODEL), 0.02),
        "w_ffn_gate": normal(ks[18], (DEPTH, D_MODEL, D_FF), D_MODEL ** -0.5),
        "w_ffn_up": normal(ks[19], (DEPTH, D_MODEL, D_FF), D_MODEL ** -0.5),
        "w_ffn_down": normal(ks[20], (DEPTH, D_FF, D_MODEL), D_FF ** -0.5),
    }


def reference(x, g_mix, w_in, g_q, g_k, w_attn_proj, lambda_re, lambda_im, log_dt,
              b_re, b_im, c_re, c_im, d_skip, w_glu_a, w_glu_b, w_out,
              g_ffn, w_ffn_gate, w_ffn_up, w_ffn_down):
    for l in range(DEPTH):
        h = rms_norm(x, g_mix[l])
        z = h @ w_in[l]
        q, k, v, u, gate_a, gate_s = jnp.split(z, SPLIT_POINTS, axis=-1)
        a = dilated_attention(q, k, v, g_q[l], g_k[l]).astype(x.dtype)
        a_out = a @ w_attn_proj[l]
        y = s5_ssm(u, lambda_re[l], lambda_im[l], log_dt[l], b_re[l], b_im[l],
                   c_re[l], c_im[l], d_skip[l])
        y = jax.nn.gelu(y).astype(x.dtype)
        s_out = (y @ w_glu_a[l]) * jax.nn.sigmoid(y @ w_glu_b[l])
        mix = jax.nn.sigmoid(gate_a) * a_out + jax.nn.sigmoid(gate_s) * s_out
        x = x + mix @ w_out[l]
        h2 = rms_norm(x, g_ffn[l])
        x = x + (jax.nn.silu(h2 @ w_ffn_gate[l]) * (h2 @ w_ffn_up[l])) @ w_ffn_down[l]
    return x
```

```python
import functools

import jax
import jax.numpy as jnp
from jax import lax
from jax.experimental import pallas as pl
from jax.experimental.pallas import tpu as pltpu

F32 = jnp.float32
BF16 = jnp.bfloat16

D_MODEL = 1024
HEAD_DIM = 64
HEADS = 8
ATTN_WIDTH = HEADS * HEAD_DIM
DILATIONS = (1, 4, 16)
N_GROUPS = len(DILATIONS)
QKV_WIDTH = N_GROUPS * ATTN_WIDTH
BLK = 128
SSM_WIDTH = 512
SSM_GROUP = 16
SSM_GROUPS = 32
SSM_STATE = 64
CHUNK = 16
CHUNK_COLS = CHUNK * SSM_GROUP
D_FF = 2816
IN_COLS = 3 * QKV_WIDTH + SSM_WIDTH + 2 * D_MODEL
EPS = 1e-6

VMEM_LIMIT_BYTES = 56 * 1024 * 1024
LANES = 128
COL_TILE = 512
TOKEN_TILE = 512


def _params(*semantics):
    return pltpu.CompilerParams(dimension_semantics=semantics,
                                vmem_limit_bytes=VMEM_LIMIT_BYTES)


def _resident(shape):
    return pl.BlockSpec(shape, lambda *_: (0,) * len(shape),
                        pipeline_mode=pl.Buffered(1))


def _rms_norm(x, gain):
    ms = jnp.mean(x * x, axis=-1, keepdims=True)
    return x * lax.rsqrt(ms + EPS) * gain


def _in_proj_kernel(x_ref, g_ref, w_ref, gq_ref, gk_ref, bd_ref,
                    q_ref, k_ref, v_ref, u_ref, gate_ref, h_ref):
    h_ref[...] = _rms_norm(x_ref[...], g_ref[...]).astype(BF16)

    def proj(j):
        return jnp.dot(h_ref[...], w_ref[:, j * COL_TILE:(j + 1) * COL_TILE],
                       preferred_element_type=F32)

    def head_norm(z, gain):
        zz = (z * z).astype(BF16)
        half = COL_TILE // 2
        ss = jnp.concatenate(
            [jnp.dot(zz[:, :half], bd_ref[...], preferred_element_type=F32),
             jnp.dot(zz[:, half:], bd_ref[...], preferred_element_type=F32)], axis=1)
        return z * lax.rsqrt(ss * (1.0 / HEAD_DIM) + EPS) * gain

    for j in range(N_GROUPS):
        cols = slice(j * COL_TILE, (j + 1) * COL_TILE)
        q_ref[:, cols] = head_norm(proj(j), gq_ref[...]).astype(BF16)
        k_ref[:, cols] = head_norm(proj(N_GROUPS + j), gk_ref[...]).astype(BF16)
        v_ref[:, cols] = proj(2 * N_GROUPS + j).astype(BF16)
    u_ref[...] = proj(3 * N_GROUPS).astype(BF16)
    for j in range(2 * D_MODEL // COL_TILE):
        cols = slice(j * COL_TILE, (j + 1) * COL_TILE)
        gate_ref[:, cols] = jax.nn.sigmoid(proj(3 * N_GROUPS + 1 + j)).astype(BF16)


def _in_proj(x, g_mix, w_in, gq, gk, bd):
    n = x.shape[0]
    tm = TOKEN_TILE
    row = lambda i: (i, 0)
    return pl.pallas_call(
        _in_proj_kernel,
        grid=(n // tm,),
        in_specs=[pl.BlockSpec((tm, D_MODEL), row),
                  _resident((1, D_MODEL)),
                  _resident((D_MODEL, IN_COLS)),
                  _resident((1, COL_TILE)),
                  _resident((1, COL_TILE)),
                  _resident((COL_TILE // 2, COL_TILE // 2))],
        out_specs=[pl.BlockSpec((tm, QKV_WIDTH), row),
                   pl.BlockSpec((tm, QKV_WIDTH), row),
                   pl.BlockSpec((tm, QKV_WIDTH), row),
                   pl.BlockSpec((tm, SSM_WIDTH), row),
                   pl.BlockSpec((tm, 2 * D_MODEL), row)],
        out_shape=[jax.ShapeDtypeStruct((n, QKV_WIDTH), BF16),
                   jax.ShapeDtypeStruct((n, QKV_WIDTH), BF16),
                   jax.ShapeDtypeStruct((n, QKV_WIDTH), BF16),
                   jax.ShapeDtypeStruct((n, SSM_WIDTH), BF16),
                   jax.ShapeDtypeStruct((n, 2 * D_MODEL), BF16)],
        scratch_shapes=[pltpu.VMEM((tm, D_MODEL), BF16)],
        compiler_params=_params("arbitrary"),
        name="in_proj",
    )(x, g_mix, w_in, gq, gk, bd)


def _attn_kernel(q_ref, k_ref, v_ref, o_ref, lse_ref, kbuf, vbuf):
    n = pl.program_id(2)

    @pl.when(n == 0)
    def _():
        kbuf[0:BLK, :] = jnp.zeros((BLK, ATTN_WIDTH), BF16)
        vbuf[0:BLK, :] = jnp.zeros((BLK, ATTN_WIDTH), BF16)

    @pl.when(n > 0)
    def _():
        kbuf[0:BLK, :] = kbuf[BLK:2 * BLK, :]
        vbuf[0:BLK, :] = vbuf[BLK:2 * BLK, :]

    kbuf[BLK:2 * BLK, :] = k_ref[...]
    vbuf[BLK:2 * BLK, :] = v_ref[...]

    qi = lax.broadcasted_iota(jnp.int32, (BLK, 2 * BLK), 0)
    kj = lax.broadcasted_iota(jnp.int32, (BLK, 2 * BLK), 1)
    first_prev = jnp.where(n > 0, qi, 2 * BLK)
    neg = jnp.full((BLK, 2 * BLK), -jnp.inf, F32)
    zero = jnp.zeros((BLK, 2 * BLK), F32)
    bias = jnp.where(kj < BLK, jnp.where(kj >= first_prev, zero, neg),
                     jnp.where(kj - BLK <= qi, zero, neg))
    bias = jnp.concatenate([bias, bias], axis=0)

    lane_q = lax.broadcasted_iota(jnp.int32, (BLK, LANES), 1) < HEAD_DIM
    lane_kv = lax.broadcasted_iota(jnp.int32, (2 * BLK, LANES), 1) < HEAD_DIM

    for p in range(HEADS // 2):
        cols = slice(p * LANES, (p + 1) * LANES)
        qp = q_ref[:, cols]
        zq = jnp.zeros_like(qp)
        q2 = jnp.concatenate([jnp.where(lane_q, qp, zq), jnp.where(lane_q, zq, qp)], axis=0)
        s = lax.dot_general(q2, kbuf[:, cols], (((1,), (1,)), ((), ())),
                            preferred_element_type=F32) + bias
        m = jnp.max(s, axis=-1, keepdims=True)
        e = jnp.exp(s - m)
        den = jnp.sum(e, axis=-1, keepdims=True)
        eb = e.astype(BF16)
        vp = vbuf[:, cols]
        zv = jnp.zeros_like(vp)
        v2 = jnp.concatenate([jnp.where(lane_kv, vp, zv), jnp.where(lane_kv, zv, vp)], axis=0)
        acc = jnp.dot(jnp.concatenate([eb[:BLK], eb[BLK:]], axis=1), v2,
                      preferred_element_type=F32)
        lse = m + jnp.log(den)
        o_ref[:, cols] = acc / jnp.where(lane_q, den[:BLK], den[BLK:])
        lse_ref[:, cols] = jnp.where(lane_q, lse[:BLK], lse[BLK:])


def _attention_group(q, k, v, batch, seq, group):
    d = DILATIONS[group]
    m = seq // d
    qv, kv, vv = (t.reshape(batch, m, d * QKV_WIDTH) for t in (q, k, v))
    in_map = lambda b, r, n: (b, n, r * N_GROUPS + group)
    out_map = lambda b, r, n: (b, n, r)
    in_spec = pl.BlockSpec((None, BLK, ATTN_WIDTH), in_map)
    out_spec = pl.BlockSpec((None, BLK, ATTN_WIDTH), out_map)
    o, lse = pl.pallas_call(
        _attn_kernel,
        grid=(batch, d, m // BLK),
        in_specs=[in_spec, in_spec, in_spec],
        out_specs=[out_spec, out_spec],
        out_shape=[jax.ShapeDtypeStruct((batch, m, d * ATTN_WIDTH), F32),
                   jax.ShapeDtypeStruct((batch, m, d * ATTN_WIDTH), F32)],
        scratch_shapes=[pltpu.VMEM((2 * BLK, ATTN_WIDTH), BF16),
                        pltpu.VMEM((2 * BLK, ATTN_WIDTH), BF16)],
        compiler_params=_params("arbitrary", "arbitrary", "arbitrary"),
        name=f"attn_d{d}",
    )(qv, kv, vv)
    return (o.reshape(batch * seq, ATTN_WIDTH), lse.reshape(batch * seq, ATTN_WIDTH))


def _ssm_tables(lam_re, lam_im, log_dt, b_re, b_im, c_re, c_im):
    hi = lax.Precision.HIGHEST
    lr = lam_re.astype(F32)
    li = lam_im.astype(F32)
    dt = jnp.exp(log_dt.astype(F32))[:, None]
    mag = jnp.exp(lr * dt)
    ang = li * dt
    abar_re = mag * jnp.cos(ang)
    abar_im = mag * jnp.sin(ang)
    nr = abar_re - 1.0
    ni = abar_im
    den = lr * lr + li * li
    cr = ((nr * lr + ni * li) / den)[..., None]
    ci = ((ni * lr - nr * li) / den)[..., None]
    br = b_re.astype(F32)
    bi = b_im.astype(F32)
    bbar_re = cr * br - ci * bi
    bbar_im = cr * bi + ci * br

    pw_re = [jnp.ones_like(abar_re)]
    pw_im = [jnp.zeros_like(abar_im)]
    for _ in range(CHUNK):
        pr, pi = pw_re[-1], pw_im[-1]
        pw_re.append(pr * abar_re - pi * abar_im)
        pw_im.append(pr * abar_im + pi * abar_re)
    pw_re = jnp.stack(pw_re)
    pw_im = jnp.stack(pw_im)

    cre = c_re.astype(F32)
    cim = c_im.astype(F32)
    cp_re = cre[None] * pw_re[:, :, None, :] - cim[None] * pw_im[:, :, None, :]
    cp_im = cre[None] * pw_im[:, :, None, :] + cim[None] * pw_re[:, :, None, :]

    lag = (jnp.einsum('tgcp,gpd->tgcd', cp_re[:CHUNK], bbar_re, precision=hi)
           - jnp.einsum('tgcp,gpd->tgcd', cp_im[:CHUNK], bbar_im, precision=hi))
    s_idx = jnp.arange(CHUNK)[:, None]
    t_idx = jnp.arange(CHUNK)[None, :]
    diff = t_idx - s_idx
    toep = jnp.where((diff >= 0)[:, :, None, None, None],
                     lag[jnp.clip(diff, 0, CHUNK - 1)], 0.0)
    toep = toep.transpose(2, 0, 4, 1, 3).reshape(SSM_GROUPS, CHUNK_COLS, CHUNK_COLS)

    rev_re = pw_re[CHUNK - 1::-1][:CHUNK]
    rev_im = pw_im[CHUNK - 1::-1][:CHUNK]
    bre_t = bbar_re.transpose(0, 2, 1)
    bim_t = bbar_im.transpose(0, 2, 1)
    win_re = (rev_re.transpose(1, 0, 2)[:, :, None, :] * bre_t[:, None]
              - rev_im.transpose(1, 0, 2)[:, :, None, :] * bim_t[:, None])
    win_im = (rev_re.transpose(1, 0, 2)[:, :, None, :] * bim_t[:, None]
              + rev_im.transpose(1, 0, 2)[:, :, None, :] * bre_t[:, None])
    win_re = win_re.reshape(SSM_GROUPS, CHUNK_COLS, SSM_STATE)
    win_im = win_im.reshape(SSM_GROUPS, CHUNK_COLS, SSM_STATE)

    wout_re = cp_re[1:].transpose(1, 3, 0, 2).reshape(SSM_GROUPS, SSM_STATE, CHUNK_COLS)
    wout_im = -cp_im[1:].transpose(1, 3, 0, 2).reshape(SSM_GROUPS, SSM_STATE, CHUNK_COLS)

    odd = (jnp.arange(SSM_GROUPS) % 2 == 1)[:, None, None]
    zc = jnp.zeros_like(win_re)
    w_in = jnp.concatenate([jnp.where(odd, zc, win_re), jnp.where(odd, win_re, zc),
                            jnp.where(odd, zc, win_im), jnp.where(odd, win_im, zc)], axis=2)
    zr = jnp.zeros_like(wout_re)
    w_out = jnp.concatenate([jnp.where(odd, zr, wout_re), jnp.where(odd, wout_re, zr),
                             jnp.where(odd, zr, wout_im), jnp.where(odd, wout_im, zr)], axis=1)
    a_re = pw_re[CHUNK].reshape(SSM_GROUPS // 2, 2 * SSM_STATE)
    a_im = pw_im[CHUNK].reshape(SSM_GROUPS // 2, 2 * SSM_STATE)
    return toep.astype(BF16), w_in.astype(BF16), w_out.astype(BF16), a_re, a_im


def _ssm_state_in_kernel(x_ref, w_ref, vre_ref, vim_ref):
    v = (jnp.dot(x_ref[0], w_ref[0], preferred_element_type=F32)
         + jnp.dot(x_ref[1], w_ref[1], preferred_element_type=F32))
    vre_ref[...] = v[:, :LANES]
    vim_ref[...] = v[:, LANES:]


SCAN_STEPS = 64
STATE_ROWS = SSM_GROUPS // 2


def _ssm_scan_kernel(are_ref, aim_ref, vre_ref, vim_ref, sre_ref, sim_ref, st_ref):
    @pl.when(pl.program_id(0) == 0)
    def _():
        st_ref[...] = jnp.zeros_like(st_ref)

    ar = are_ref[...]
    ai = aim_ref[...]

    def body(j, carry):
        sre, sim = carry
        rows = pl.ds(pl.multiple_of(j * STATE_ROWS, STATE_ROWS), STATE_ROWS)
        sre_ref[:, rows, :] = sre
        sim_ref[:, rows, :] = sim
        vre = vre_ref[:, rows, :]
        vim = vim_ref[:, rows, :]
        return (ar * sre - ai * sim + vre, ar * sim + ai * sre + vim)

    sre, sim = lax.fori_loop(0, SCAN_STEPS, body, (st_ref[0], st_ref[1]))
    st_ref[0] = sre
    st_ref[1] = sim


def _ssm_out_kernel(x_ref, t_ref, sre_ref, sim_ref, w_ref, d_ref, y_ref):
    x = x_ref[...]
    sp = jnp.concatenate([sre_ref[...], sim_ref[...]], axis=1).astype(BF16)
    y = (jnp.dot(x, t_ref[...], preferred_element_type=F32)
         + jnp.dot(sp, w_ref[...], preferred_element_type=F32)
         + d_ref[...] * x.astype(F32))
    y_ref[...] = jax.nn.gelu(y).astype(BF16)


def _ssm(u, tables, d_tiled, batch, seq):
    toep, w_in, w_out, a_re, a_im = tables
    n = batch * seq
    nchunk = n // CHUNK
    per_batch = seq // CHUNK
    x = (u.reshape(nchunk, CHUNK, SSM_GROUPS, SSM_GROUP)
         .transpose(2, 0, 1, 3).reshape(SSM_GROUPS, nchunk, CHUNK_COLS))
    state_cols = SSM_GROUPS * SSM_STATE

    vre, vim = pl.pallas_call(
        _ssm_state_in_kernel,
        grid=(SSM_GROUPS // 2,),
        in_specs=[pl.BlockSpec((2, nchunk, CHUNK_COLS), lambda q: (q, 0, 0)),
                  pl.BlockSpec((2, CHUNK_COLS, CHUNK_COLS), lambda q: (q, 0, 0))],
        out_specs=[pl.BlockSpec((nchunk, LANES), lambda q: (0, q)),
                   pl.BlockSpec((nchunk, LANES), lambda q: (0, q))],
        out_shape=[jax.ShapeDtypeStruct((nchunk, state_cols), F32)] * 2,
        compiler_params=_params("arbitrary"),
        name="ssm_state_in",
    )(x, w_in)

    scan_rows = SCAN_STEPS * STATE_ROWS
    tile3 = pl.BlockSpec((batch, scan_rows, LANES), lambda i: (0, i, 0))
    sre, sim = pl.pallas_call(
        _ssm_scan_kernel,
        grid=(per_batch // SCAN_STEPS,),
        in_specs=[_resident((STATE_ROWS, LANES)), _resident((STATE_ROWS, LANES)), tile3, tile3],
        out_specs=[tile3, tile3],
        out_shape=[jax.ShapeDtypeStruct((batch, per_batch * STATE_ROWS, LANES), F32)] * 2,
        scratch_shapes=[pltpu.VMEM((2, batch, STATE_ROWS, LANES), F32)],
        compiler_params=_params("arbitrary"),
        name="ssm_scan",
    )(a_re, a_im,
      vre.reshape(batch, per_batch * STATE_ROWS, LANES),
      vim.reshape(batch, per_batch * STATE_ROWS, LANES))
    sre = sre.reshape(nchunk, state_cols)
    sim = sim.reshape(nchunk, state_cols)

    y = pl.pallas_call(
        _ssm_out_kernel,
        grid=(SSM_GROUPS,),
        in_specs=[pl.BlockSpec((None, nchunk, CHUNK_COLS), lambda g: (g, 0, 0)),
                  pl.BlockSpec((None, CHUNK_COLS, CHUNK_COLS), lambda g: (g, 0, 0)),
                  pl.BlockSpec((nchunk, LANES), lambda g: (0, g // 2)),
                  pl.BlockSpec((nchunk, LANES), lambda g: (0, g // 2)),
                  pl.BlockSpec((None, CHUNK_COLS, CHUNK_COLS), lambda g: (g, 0, 0)),
                  pl.BlockSpec((None, 1, CHUNK_COLS), lambda g: (g, 0, 0))],
        out_specs=pl.BlockSpec((None, nchunk, CHUNK_COLS), lambda g: (g, 0, 0)),
        out_shape=jax.ShapeDtypeStruct((SSM_GROUPS, nchunk, CHUNK_COLS), BF16),
        compiler_params=_params("arbitrary"),
        name="ssm_out",
    )(x, toep, sre, sim, w_out, d_tiled)
    return (y.reshape(SSM_GROUPS, nchunk, CHUNK, SSM_GROUP)
            .transpose(1, 2, 0, 3).reshape(n, SSM_WIDTH))


def _mix_kernel(x_ref, o0_ref, o1_ref, o2_ref, l0_ref, l1_ref, l2_ref, y_ref, gate_ref,
                wap_ref, wa_ref, wb_ref, wout_ref, out_ref):
    l0, l1, l2 = l0_ref[...], l1_ref[...], l2_ref[...]
    top = jnp.maximum(jnp.maximum(l0, l1), l2)
    e0, e1, e2 = jnp.exp(l0 - top), jnp.exp(l1 - top), jnp.exp(l2 - top)
    a = (e0 * o0_ref[...] + e1 * o1_ref[...] + e2 * o2_ref[...]) / (e0 + e1 + e2)
    a_out = jnp.dot(a.astype(BF16), wap_ref[...], preferred_element_type=F32)
    y = y_ref[...]
    s_out = (jnp.dot(y, wa_ref[...], preferred_element_type=F32)
             * jax.nn.sigmoid(jnp.dot(y, wb_ref[...], preferred_element_type=F32)))
    mix = (gate_ref[:, :D_MODEL].astype(F32) * a_out
           + gate_ref[:, D_MODEL:].astype(F32) * s_out)
    out_ref[...] = x_ref[...] + jnp.dot(mix.astype(BF16), wout_ref[...],
                                        preferred_element_type=F32)


def _mix(x, o, lse, y, gates, wap, wa, wb, wout):
    n = x.shape[0]
    tm = TOKEN_TILE
    row = lambda i: (i, 0)
    attn_spec = pl.BlockSpec((tm, ATTN_WIDTH), row)
    return pl.pallas_call(
        _mix_kernel,
        grid=(n // tm,),
        in_specs=[pl.BlockSpec((tm, D_MODEL), row)] + [attn_spec] * 7
                 + [pl.BlockSpec((tm, 2 * D_MODEL), row),
                    _resident((ATTN_WIDTH, D_MODEL)), _resident((SSM_WIDTH, D_MODEL)),
                    _resident((SSM_WIDTH, D_MODEL)), _resident((D_MODEL, D_MODEL))],
        out_specs=pl.BlockSpec((tm, D_MODEL), row),
        out_shape=jax.ShapeDtypeStruct((n, D_MODEL), F32),
        compiler_params=_params("arbitrary"),
        name="mix",
    )(x, *o, *lse, y, gates, wap, wa, wb, wout)


def _ffn_kernel(x_ref, g_ref, wg_ref, wu_ref, wd_ref, out_ref):
    x = x_ref[...]
    h = _rms_norm(x, g_ref[...]).astype(BF16)
    gate = jnp.dot(h, wg_ref[...], preferred_element_type=F32)
    up = jnp.dot(h, wu_ref[...], preferred_element_type=F32)
    hidden = (jax.nn.silu(gate) * up).astype(BF16)
    out_ref[...] = x + jnp.dot(hidden, wd_ref[...], preferred_element_type=F32)


def _ffn(x, g, wg, wu, wd):
    n = x.shape[0]
    tm = TOKEN_TILE
    row = lambda i: (i, 0)
    return pl.pallas_call(
        _ffn_kernel,
        grid=(n // tm,),
        in_specs=[pl.BlockSpec((tm, D_MODEL), row), _resident((1, D_MODEL)),
                  _resident((D_MODEL, D_FF)), _resident((D_MODEL, D_FF)),
                  _resident((D_FF, D_MODEL))],
        out_specs=pl.BlockSpec((tm, D_MODEL), row),
        out_shape=jax.ShapeDtypeStruct((n, D_MODEL), F32),
        compiler_params=_params("arbitrary"),
        name="ffn",
    )(x, g, wg, wu, wd)


def kernel(x, g_mix, w_in, g_q, g_k, w_attn_proj, lambda_re, lambda_im, log_dt, b_re, b_im,
           c_re, c_im, d_skip, w_glu_a, w_glu_b, w_out, g_ffn, w_ffn_gate, w_ffn_up, w_ffn_down):
    batch, seq, _ = x.shape
    depth = w_in.shape[0]
    n = batch * seq
    xf = x.reshape(n, D_MODEL).astype(F32)

    blk = jnp.arange(COL_TILE // 2) // HEAD_DIM
    bd = (blk[:, None] == blk[None, :]).astype(BF16)

    for l in range(depth):
        gq = jnp.tile(g_q[l].astype(F32) * (HEAD_DIM ** -0.5), HEADS)[None]
        gk = jnp.tile(g_k[l].astype(F32), HEADS)[None]
        q, k, v, u, gates = _in_proj(xf, g_mix[l].astype(F32)[None], w_in[l].astype(BF16),
                                     gq, gk, bd)
        o, lse = zip(*[_attention_group(q, k, v, batch, seq, g) for g in range(N_GROUPS)])
        tables = _ssm_tables(lambda_re[l], lambda_im[l], log_dt[l], b_re[l], b_im[l],
                             c_re[l], c_im[l])
        d_tiled = jnp.tile(d_skip[l].astype(F32).reshape(SSM_GROUPS, 1, SSM_GROUP),
                           (1, CHUNK, 1)).reshape(SSM_GROUPS, 1, CHUNK_COLS)
        y = _ssm(u, tables, d_tiled, batch, seq)
        xf = _mix(xf, o, lse, y, gates, w_attn_proj[l].astype(BF16), w_glu_a[l].astype(BF16),
                  w_glu_b[l].astype(BF16), w_out[l].astype(BF16))
        xf = _ffn(xf, g_ffn[l].astype(F32)[None], w_ffn_gate[l].astype(BF16),
                  w_ffn_up[l].astype(BF16), w_ffn_down[l].astype(BF16))
    return xf.reshape(batch, seq, D_MODEL).astype(x.dtype)
```

```python
import functools

import jax
import jax.numpy as jnp
from jax import lax
from jax.experimental import pallas as pl
from jax.experimental.pallas import tpu as pltpu

F32 = jnp.float32
BF16 = jnp.bfloat16

D_MODEL = 1024
HEAD_DIM = 64
HEADS = 8
ATTN_WIDTH = HEADS * HEAD_DIM
DILATIONS = (1, 4, 16)
N_GROUPS = len(DILATIONS)
BLK = 128
SSM_WIDTH = 512
SSM_GROUP = 16
SSM_GROUPS = 32
SSM_STATE = 64
CHUNK = 16
CHUNK_COLS = CHUNK * SSM_GROUP
D_FF = 2816
IN_COLS = 3 * N_GROUPS * ATTN_WIDTH + SSM_WIDTH + 2 * D_MODEL
EPS = 1e-6

VMEM_LIMIT_BYTES = 56 * 1024 * 1024
LANES = 128
COL_TILE = 512
TOKEN_TILE = 512
PIECE = SSM_GROUP
PIECES = LANES // PIECE
N_SLABS = ATTN_WIDTH // LANES
STATE_ROWS = SSM_GROUPS // 2


def _params(*semantics):
    return pltpu.CompilerParams(dimension_semantics=semantics,
                                vmem_limit_bytes=VMEM_LIMIT_BYTES)


def _resident(shape):
    return pl.BlockSpec(shape, lambda *_: (0,) * len(shape),
                        pipeline_mode=pl.Buffered(1))


def _rms_norm(x, gain):
    ms = jnp.mean(x * x, axis=-1, keepdims=True)
    return x * lax.rsqrt(ms + EPS) * gain


def _piece_transpose(vs):
    vs = list(vs)
    piece = lax.broadcasted_iota(jnp.int32, vs[0].shape, 1) // PIECE
    for k in (4, 2, 1):
        upper = (piece & k) != 0
        for j in range(PIECES):
            if j & k:
                continue
            a, b = vs[j], vs[j + k]
            vs[j] = jnp.where(upper, pltpu.roll(b, PIECE * k, 1), a)
            vs[j + k] = jnp.where(upper, b, pltpu.roll(a, LANES - PIECE * k, 1))
    return vs


def _in_proj_kernel(x_ref, g_ref, w_ref, gq_ref, gk_ref, bd_ref, *refs):
    qkv_refs = refs[:3 * N_GROUPS]
    xs_ref, gate_ref, h_ref, slab_ref = refs[3 * N_GROUPS:]
    tm = x_ref.shape[0]
    h_ref[...] = _rms_norm(x_ref[...], g_ref[...]).astype(BF16)

    def proj(j):
        return jnp.dot(h_ref[...], w_ref[:, j * COL_TILE:(j + 1) * COL_TILE],
                       preferred_element_type=F32)

    def head_norm(z, gain):
        zz = (z * z).astype(BF16)
        half = COL_TILE // 2
        ss = jnp.concatenate(
            [jnp.dot(zz[:, :half], bd_ref[...], preferred_element_type=F32),
             jnp.dot(zz[:, half:], bd_ref[...], preferred_element_type=F32)], axis=1)
        return z * lax.rsqrt(ss * (1.0 / HEAD_DIM) + EPS) * gain

    def to_slabs(z):
        for c in range(N_SLABS):
            slab_ref[c] = z[:, c * LANES:(c + 1) * LANES]

    def emit_dilated(z, out_ref, d):
        if d == 1:
            out_ref[0] = z.astype(BF16)
            return
        to_slabs(z)
        for r in range(d):
            for c in range(N_SLABS):
                out_ref[r, :, c * LANES:(c + 1) * LANES] = (
                    slab_ref[c, pl.ds(r, tm // d, stride=d), :].astype(BF16))

    for g, d in enumerate(DILATIONS):
        emit_dilated(head_norm(proj(g), gq_ref[...]), qkv_refs[g], d)
        emit_dilated(head_norm(proj(N_GROUPS + g), gk_ref[...]), qkv_refs[N_GROUPS + g], d)
        emit_dilated(proj(2 * N_GROUPS + g), qkv_refs[2 * N_GROUPS + g], d)

    to_slabs(proj(3 * N_GROUPS))
    for a in range(N_SLABS):
        for half in range(CHUNK // PIECES):
            vs = [slab_ref[a, pl.ds(half * PIECES + j, tm // CHUNK, stride=CHUNK), :]
                  for j in range(PIECES)]
            for i, w in enumerate(_piece_transpose(vs)):
                xs_ref[a * PIECES + i, :, half * LANES:(half + 1) * LANES] = w.astype(BF16)

    for j in range(2 * D_MODEL // COL_TILE):
        cols = slice(j * COL_TILE, (j + 1) * COL_TILE)
        gate_ref[:, cols] = jax.nn.sigmoid(proj(3 * N_GROUPS + 1 + j)).astype(BF16)


def _in_proj(x, g_mix, w_in, gq, gk, bd, batch, seq):
    n = x.shape[0]
    tm = TOKEN_TILE
    tiles = seq // tm
    row = lambda i: (i, 0)
    dil_specs, dil_shapes = [], []
    for _ in range(3):
        for d in DILATIONS:
            dil_specs.append(pl.BlockSpec((None, d, tm // d, ATTN_WIDTH),
                                          lambda i: (i // tiles, 0, i % tiles, 0)))
            dil_shapes.append(jax.ShapeDtypeStruct((batch, d, seq // d, ATTN_WIDTH), BF16))
    outs = pl.pallas_call(
        _in_proj_kernel,
        grid=(n // tm,),
        in_specs=[pl.BlockSpec((tm, D_MODEL), row),
                  _resident((1, D_MODEL)),
                  _resident((D_MODEL, IN_COLS)),
                  _resident((1, COL_TILE)),
                  _resident((1, COL_TILE)),
                  _resident((COL_TILE // 2, COL_TILE // 2))],
        out_specs=dil_specs + [
            pl.BlockSpec((SSM_GROUPS, tm // CHUNK, CHUNK_COLS), lambda i: (0, i, 0)),
            pl.BlockSpec((tm, 2 * D_MODEL), row)],
        out_shape=dil_shapes + [
            jax.ShapeDtypeStruct((SSM_GROUPS, n // CHUNK, CHUNK_COLS), BF16),
            jax.ShapeDtypeStruct((n, 2 * D_MODEL), BF16)],
        scratch_shapes=[pltpu.VMEM((tm, D_MODEL), BF16),
                        pltpu.VMEM((N_SLABS, tm, LANES), F32)],
        compiler_params=_params("arbitrary"),
        name="in_proj",
    )(x, g_mix, w_in, gq, gk, bd)
    return outs[0:3], outs[3:6], outs[6:9], outs[9], outs[10]


def _attn_kernel(q_ref, k_ref, v_ref, o_ref, lse_ref, kbuf, vbuf, *, dilation):
    n = pl.program_id(1)
    r = pl.program_id(2)
    kb = kbuf.at[r]
    vb = vbuf.at[r]

    @pl.when(n == 0)
    def _():
        kb[0:BLK, :] = jnp.zeros((BLK, ATTN_WIDTH), BF16)
        vb[0:BLK, :] = jnp.zeros((BLK, ATTN_WIDTH), BF16)

    @pl.when(n > 0)
    def _():
        kb[0:BLK, :] = kb[BLK:2 * BLK, :]
        vb[0:BLK, :] = vb[BLK:2 * BLK, :]

    kb[BLK:2 * BLK, :] = k_ref[...]
    vb[BLK:2 * BLK, :] = v_ref[...]

    qi = lax.broadcasted_iota(jnp.int32, (BLK, 2 * BLK), 0)
    kj = lax.broadcasted_iota(jnp.int32, (BLK, 2 * BLK), 1)
    first_prev = jnp.where(n > 0, qi, 2 * BLK)
    neg = jnp.full((BLK, 2 * BLK), -jnp.inf, F32)
    zero = jnp.zeros((BLK, 2 * BLK), F32)
    bias = jnp.where(kj < BLK, jnp.where(kj >= first_prev, zero, neg),
                     jnp.where(kj - BLK <= qi, zero, neg))
    bias = jnp.concatenate([bias, bias], axis=0)

    lane_q = lax.broadcasted_iota(jnp.int32, (BLK, LANES), 1) < HEAD_DIM
    lane_kv = lax.broadcasted_iota(jnp.int32, (2 * BLK, LANES), 1) < HEAD_DIM
    rows = slice(None) if dilation == 1 else pl.ds(r, BLK, stride=dilation)

    for p in range(HEADS // 2):
        cols = slice(p * LANES, (p + 1) * LANES)
        qp = q_ref[:, cols]
        zq = jnp.zeros_like(qp)
        q2 = jnp.concatenate([jnp.where(lane_q, qp, zq), jnp.where(lane_q, zq, qp)], axis=0)
        s = lax.dot_general(q2, kb[:, cols], (((1,), (1,)), ((), ())),
                            preferred_element_type=F32) + bias
        m = jnp.max(s, axis=-1, keepdims=True)
        e = jnp.exp(s - m)
        den = jnp.sum(e, axis=-1, keepdims=True)
        eb = e.astype(BF16)
        vp = vb[:, cols]
        zv = jnp.zeros_like(vp)
        v2 = jnp.concatenate([jnp.where(lane_kv, vp, zv), jnp.where(lane_kv, zv, vp)], axis=0)
        acc = jnp.dot(jnp.concatenate([eb[:BLK], eb[BLK:]], axis=1), v2,
                      preferred_element_type=F32)
        lse = m + jnp.log(den)
        o_ref[p, rows, :] = acc / jnp.where(lane_q, den[:BLK], den[BLK:])
        lse_ref[p, rows, :] = jnp.where(lane_q, lse[:BLK], lse[BLK:])


def _attention_group(q, k, v, batch, seq, group):
    d = DILATIONS[group]
    m = seq // d
    in_spec = pl.BlockSpec((None, None, BLK, ATTN_WIDTH), lambda b, n, r: (b, r, n, 0))
    out_spec = pl.BlockSpec((None, N_SLABS, BLK * d, LANES), lambda b, n, r: (b, 0, n, 0))
    return pl.pallas_call(
        functools.partial(_attn_kernel, dilation=d),
        grid=(batch, m // BLK, d),
        in_specs=[in_spec, in_spec, in_spec],
        out_specs=[out_spec, out_spec],
        out_shape=[jax.ShapeDtypeStruct((batch, N_SLABS, seq, LANES), F32)] * 2,
        scratch_shapes=[pltpu.VMEM((d, 2 * BLK, ATTN_WIDTH), BF16),
                        pltpu.VMEM((d, 2 * BLK, ATTN_WIDTH), BF16)],
        compiler_params=_params("arbitrary", "arbitrary", "arbitrary"),
        name=f"attn_d{d}",
    )(q, k, v)


def _ssm_tables(lam_re, lam_im, log_dt, b_re, b_im, c_re, c_im):
    hi = lax.Precision.HIGHEST
    depth = lam_re.shape[0]
    lr = lam_re.astype(F32)
    li = lam_im.astype(F32)
    dt = jnp.exp(log_dt.astype(F32))[..., None]
    mag = jnp.exp(lr * dt)
    ang = li * dt
    abar_re = mag * jnp.cos(ang)
    abar_im = mag * jnp.sin(ang)
    nr = abar_re - 1.0
    ni = abar_im
    den = lr * lr + li * li
    cr = ((nr * lr + ni * li) / den)[:, :, None, :]
    ci = ((ni * lr - nr * li) / den)[:, :, None, :]
    brt = b_re.astype(F32).transpose(0, 1, 3, 2)
    bit = b_im.astype(F32).transpose(0, 1, 3, 2)
    bbar_re = cr * brt - ci * bit
    bbar_im = cr * bit + ci * brt

    pw_re = [jnp.ones_like(abar_re)]
    pw_im = [jnp.zeros_like(abar_im)]
    for _ in range(CHUNK):
        pr, pi = pw_re[-1], pw_im[-1]
        pw_re.append(pr * abar_re - pi * abar_im)
        pw_im.append(pr * abar_im + pi * abar_re)
    pw_re = jnp.stack(pw_re, axis=2)
    pw_im = jnp.stack(pw_im, axis=2)

    wide = (CHUNK + 1) * SSM_GROUP
    lane = jnp.arange(wide)
    rep = (lane[None, :] // SSM_GROUP == jnp.arange(CHUNK + 1)[:, None]).astype(F32)
    til = (lane[None, :] % SSM_GROUP == jnp.arange(SSM_GROUP)[:, None]).astype(F32)
    pr_l = jnp.einsum('dgtp,tl->dgpl', pw_re, rep, precision=hi)
    pi_l = jnp.einsum('dgtp,tl->dgpl', pw_im, rep, precision=hi)
    cr_l = jnp.einsum('dgcp,cl->dgpl', c_re.astype(F32), til, precision=hi)
    ci_l = jnp.einsum('dgcp,cl->dgpl', c_im.astype(F32), til, precision=hi)
    cp_re = cr_l * pr_l - ci_l * pi_l
    cp_im = cr_l * pi_l + ci_l * pr_l

    lag = (jnp.einsum('dgcp,dgpl->dgcl', bbar_re, cp_re[..., :CHUNK_COLS], precision=hi)
           - jnp.einsum('dgcp,dgpl->dgcl', bbar_im, cp_im[..., :CHUNK_COLS], precision=hi))
    toep = jnp.stack(
        [jnp.pad(lag[..., :CHUNK_COLS - PIECE * s], ((0, 0), (0, 0), (0, 0), (PIECE * s, 0)))
         for s in range(CHUNK)], axis=2)
    toep = toep.reshape(depth, SSM_GROUPS, CHUNK_COLS, CHUNK_COLS)

    rev_re = pw_re[:, :, CHUNK - 1::-1][:, :, :, None, :]
    rev_im = pw_im[:, :, CHUNK - 1::-1][:, :, :, None, :]
    win_re = (rev_re * bbar_re[:, :, None] - rev_im * bbar_im[:, :, None])
    win_im = (rev_re * bbar_im[:, :, None] + rev_im * bbar_re[:, :, None])
    win_re = win_re.reshape(depth, SSM_GROUPS, CHUNK_COLS, SSM_STATE)
    win_im = win_im.reshape(depth, SSM_GROUPS, CHUNK_COLS, SSM_STATE)

    wout_re = cp_re[..., SSM_GROUP:]
    wout_im = -cp_im[..., SSM_GROUP:]

    odd = (jnp.arange(SSM_GROUPS) % 2 == 1)[None, :, None, None]
    zc = jnp.zeros_like(win_re)
    w_in = jnp.concatenate([jnp.where(odd, zc, win_re), jnp.where(odd, win_re, zc),
                            jnp.where(odd, zc, win_im), jnp.where(odd, win_im, zc)], axis=3)
    zr = jnp.zeros_like(wout_re)
    w_out = jnp.concatenate([jnp.where(odd, zr, wout_re), jnp.where(odd, wout_re, zr),
                             jnp.where(odd, zr, wout_im), jnp.where(odd, wout_im, zr)], axis=2)
    a_re = pw_re[:, :, CHUNK].reshape(depth, STATE_ROWS, LANES)
    a_im = pw_im[:, :, CHUNK].reshape(depth, STATE_ROWS, LANES)
    return toep.astype(BF16), w_in.astype(BF16), w_out.astype(BF16), a_re, a_im


def _ssm_kernel(x_ref, win_ref, toep_ref, wout_ref, are_ref, aim_ref, d_ref, y_ref,
                sre_ref, sim_ref, *, batch, per_batch):
    phase = pl.program_id(0)
    pair = pl.program_id(1)
    nchunk = batch * per_batch
    pair_rows = pl.ds(pair, nchunk, stride=STATE_ROWS)

    @pl.when(phase == 0)
    def _():
        v = (jnp.dot(x_ref[0], win_ref[0], preferred_element_type=F32)
             + jnp.dot(x_ref[1], win_ref[1], preferred_element_type=F32))
        sre_ref[pair_rows, :] = v[:, :LANES]
        sim_ref[pair_rows, :] = v[:, LANES:]

    @pl.when((phase == 0) & (pair == pl.num_programs(1) - 1))
    def _():
        ar = are_ref[...]
        ai = aim_ref[...]

        def body(k, carry):
            new = []
            for b in range(batch):
                sre, sim = carry[b]
                rows = pl.ds(pl.multiple_of((b * per_batch + k) * STATE_ROWS, STATE_ROWS),
                             STATE_ROWS)
                vre = sre_ref[rows, :]
                vim = sim_ref[rows, :]
                sre_ref[rows, :] = sre
                sim_ref[rows, :] = sim
                new.append((ar * sre - ai * sim + vre, ar * sim + ai * sre + vim))
            return tuple(new)

        zero = jnp.zeros((STATE_ROWS, LANES), F32)
        lax.fori_loop(0, per_batch, body, tuple((zero, zero) for _ in range(batch)))

    @pl.when(phase == 1)
    def _():
        sp = jnp.concatenate([sre_ref[pair_rows, :], sim_ref[pair_rows, :]],
                             axis=1).astype(BF16)
        for h in range(2):
            x = x_ref[h]
            y = (jnp.dot(x, toep_ref[h], preferred_element_type=F32)
                 + jnp.dot(sp, wout_ref[h], preferred_element_type=F32)
                 + d_ref[h] * x.astype(F32))
            y_ref[h] = jax.nn.gelu(y).astype(BF16)


def _ssm(x, toep, w_in, w_out, a_re, a_im, d_tiled, batch, seq):
    nchunk = x.shape[1]
    per_batch = seq // CHUNK
    pair3 = lambda shape: pl.BlockSpec((2,) + shape, lambda ph, q: (q, 0, 0))
    return pl.pallas_call(
        functools.partial(_ssm_kernel, batch=batch, per_batch=per_batch),
        grid=(2, SSM_GROUPS // 2),
        in_specs=[pair3((nchunk, CHUNK_COLS)), pair3((CHUNK_COLS, CHUNK_COLS)),
                  pair3((CHUNK_COLS, CHUNK_COLS)), pair3((CHUNK_COLS, CHUNK_COLS)),
                  _resident((STATE_ROWS, LANES)), _resident((STATE_ROWS, LANES)),
                  pair3((1, CHUNK_COLS))],
        out_specs=pl.BlockSpec((2, nchunk, CHUNK_COLS), lambda ph, q: (q * ph, 0, 0)),
        out_shape=jax.ShapeDtypeStruct((SSM_GROUPS, nchunk, CHUNK_COLS), BF16),
        scratch_shapes=[pltpu.VMEM((nchunk * STATE_ROWS, LANES), F32),
                        pltpu.VMEM((nchunk * STATE_ROWS, LANES), F32)],
        compiler_params=_params("arbitrary", "arbitrary"),
        name="ssm",
    )(x, w_in, toep, w_out, a_re, a_im, d_tiled)


def _mix_kernel(x_ref, o0_ref, o1_ref, o2_ref, l0_ref, l1_ref, l2_ref, ys_ref, gate_ref,
                wap_ref, wa_ref, wb_ref, wout_ref, out_ref, slab_ref):
    tm = x_ref.shape[0]
    wide = lambda ref: jnp.concatenate([ref[c] for c in range(N_SLABS)], axis=1)
    l0, l1, l2 = wide(l0_ref), wide(l1_ref), wide(l2_ref)
    top = jnp.maximum(jnp.maximum(l0, l1), l2)
    e0, e1, e2 = jnp.exp(l0 - top), jnp.exp(l1 - top), jnp.exp(l2 - top)
    a = (e0 * wide(o0_ref) + e1 * wide(o1_ref) + e2 * wide(o2_ref)) / (e0 + e1 + e2)
    a_out = jnp.dot(a.astype(BF16), wap_ref[...], preferred_element_type=F32)

    for s in range(N_SLABS):
        for half in range(CHUNK // PIECES):
            ws = [ys_ref[s * PIECES + i, :, half * LANES:(half + 1) * LANES].astype(F32)
                  for i in range(PIECES)]
            for j, v in enumerate(_piece_transpose(ws)):
                slab_ref[s, pl.ds(half * PIECES + j, tm // CHUNK, stride=CHUNK), :] = v
    y = wide(slab_ref).astype(BF16)
    s_out = (jnp.dot(y, wa_ref[...], preferred_element_type=F32)
             * jax.nn.sigmoid(jnp.dot(y, wb_ref[...], preferred_element_type=F32)))
    mix = (gate_ref[:, :D_MODEL].astype(F32) * a_out
           + gate_ref[:, D_MODEL:].astype(F32) * s_out)
    out_ref[...] = x_ref[...] + jnp.dot(mix.astype(BF16), wout_ref[...],
                                        preferred_element_type=F32)


def _mix(x, o, lse, ys, gates, wap, wa, wb, wout, batch, seq):
    n = x.shape[0]
    tm = TOKEN_TILE
    tiles = seq // tm
    row = lambda i: (i, 0)
    attn_spec = pl.BlockSpec((None, N_SLABS, tm, LANES), lambda i: (i // tiles, 0, i % tiles, 0))
    return pl.pallas_call(
        _mix_kernel,
        grid=(n // tm,),
        in_specs=[pl.BlockSpec((tm, D_MODEL), row)] + [attn_spec] * 6
                 + [pl.BlockSpec((SSM_GROUPS, tm // CHUNK, CHUNK_COLS), lambda i: (0, i, 0)),
                    pl.BlockSpec((tm, 2 * D_MODEL), row),
                    _resident((ATTN_WIDTH, D_MODEL)), _resident((SSM_WIDTH, D_MODEL)),
                    _resident((SSM_WIDTH, D_MODEL)), _resident((D_MODEL, D_MODEL))],
        out_specs=pl.BlockSpec((tm, D_MODEL), row),
        out_shape=jax.ShapeDtypeStruct((n, D_MODEL), F32),
        scratch_shapes=[pltpu.VMEM((N_SLABS, tm, LANES), F32)],
        compiler_params=_params("arbitrary"),
        name="mix",
    )(x, *o, *lse, ys, gates, wap, wa, wb, wout)


def _ffn_kernel(x_ref, g_ref, wg_ref, wu_ref, wd_ref, out_ref):
    x = x_ref[...]
    h = _rms_norm(x, g_ref[...]).astype(BF16)
    gate = jnp.dot(h, wg_ref[...], preferred_element_type=F32)
    up = jnp.dot(h, wu_ref[...], preferred_element_type=F32)
    hidden = (jax.nn.silu(gate) * up).astype(BF16)
    out_ref[...] = x + jnp.dot(hidden, wd_ref[...], preferred_element_type=F32)


def _ffn(x, g, wg, wu, wd):
    n = x.shape[0]
    tm = TOKEN_TILE
    row = lambda i: (i, 0)
    return pl.pallas_call(
        _ffn_kernel,
        grid=(n // tm,),
        in_specs=[pl.BlockSpec((tm, D_MODEL), row), _resident((1, D_MODEL)),
                  _resident((D_MODEL, D_FF)), _resident((D_MODEL, D_FF)),
                  _resident((D_FF, D_MODEL))],
        out_specs=pl.BlockSpec((tm, D_MODEL), row),
        out_shape=jax.ShapeDtypeStruct((n, D_MODEL), F32),
        compiler_params=_params("arbitrary"),
        name="ffn",
    )(x, g, wg, wu, wd)


def kernel(x, g_mix, w_in, g_q, g_k, w_attn_proj, lambda_re, lambda_im, log_dt, b_re, b_im,
           c_re, c_im, d_skip, w_glu_a, w_glu_b, w_out, g_ffn, w_ffn_gate, w_ffn_up, w_ffn_down):
    batch, seq, _ = x.shape
    depth = w_in.shape[0]
    n = batch * seq
    xf = x.reshape(n, D_MODEL).astype(F32)

    blk = jnp.arange(COL_TILE // 2) // HEAD_DIM
    bd = (blk[:, None] == blk[None, :]).astype(BF16)
    gq = jnp.tile(g_q.astype(F32) * (HEAD_DIM ** -0.5), (1, HEADS))[:, None]
    gk = jnp.tile(g_k.astype(F32), (1, HEADS))[:, None]
    toep, s_in, s_out, a_re, a_im = _ssm_tables(lambda_re, lambda_im, log_dt, b_re, b_im,
                                                c_re, c_im)
    d_tiled = jnp.tile(d_skip.astype(F32).reshape(depth, SSM_GROUPS, 1, SSM_GROUP),
                       (1, 1, 1, CHUNK))

    for l in range(depth):
        q, k, v, xs, gates = _in_proj(xf, g_mix[l].astype(F32)[None], w_in[l].astype(BF16),
                                      gq[l], gk[l], bd, batch, seq)
        o, lse = zip(*[_attention_group(q[g], k[g], v[g], batch, seq, g)
                       for g in range(N_GROUPS)])
        ys = _ssm(xs, toep[l], s_in[l], s_out[l], a_re[l], a_im[l], d_tiled[l], batch, seq)
        xf = _mix(xf, o, lse, ys, gates, w_attn_proj[l].astype(BF16), w_glu_a[l].astype(BF16),
                  w_glu_b[l].astype(BF16), w_out[l].astype(BF16), batch, seq)
        xf = _ffn(xf, g_ffn[l].astype(F32)[None], w_ffn_gate[l].astype(BF16),
                  w_ffn_up[l].astype(BF16), w_ffn_down[l].astype(BF16))
    return xf.reshape(batch, seq, D_MODEL).astype(x.dtype)
```

```python
import functools

import jax
import jax.numpy as jnp
from jax import lax
from jax.experimental import pallas as pl
from jax.experimental.pallas import tpu as pltpu

F32 = jnp.float32
BF16 = jnp.bfloat16

D_MODEL = 1024
HEAD_DIM = 64
HEADS = 8
ATTN_WIDTH = HEADS * HEAD_DIM
DILATIONS = (1, 4, 16)
N_GROUPS = len(DILATIONS)
BLK = 128
SSM_WIDTH = 512
SSM_GROUP = 16
SSM_GROUPS = 32
SSM_STATE = 64
CHUNK = 16
CHUNK_COLS = CHUNK * SSM_GROUP
D_FF = 2816
IN_COLS = 3 * N_GROUPS * ATTN_WIDTH + SSM_WIDTH + 2 * D_MODEL
EPS = 1e-6
LOG2E = 1.4426950408889634
LN2 = 0.6931471805599453

VMEM_LIMIT_BYTES = 56 * 1024 * 1024
LANES = 128
COL_TILE = 512
TOKEN_TILE = 512
PIECE = SSM_GROUP
PIECES = LANES // PIECE
N_SLABS = ATTN_WIDTH // LANES
STATE_ROWS = SSM_GROUPS // 2


def _params(*semantics):
    return pltpu.CompilerParams(dimension_semantics=semantics,
                                vmem_limit_bytes=VMEM_LIMIT_BYTES)


def _resident(shape):
    return pl.BlockSpec(shape, lambda *_: (0,) * len(shape),
                        pipeline_mode=pl.Buffered(1))


def _rms_norm(x, gain):
    ms = jnp.mean(x * x, axis=-1, keepdims=True)
    return x * lax.rsqrt(ms + EPS) * gain


def _piece_transpose(vs):
    vs = list(vs)
    piece = lax.broadcasted_iota(jnp.int32, vs[0].shape, 1) // PIECE
    for k in (4, 2, 1):
        upper = (piece & k) != 0
        for j in range(PIECES):
            if j & k:
                continue
            a, b = vs[j], vs[j + k]
            vs[j] = jnp.where(upper, pltpu.roll(b, PIECE * k, 1), a)
            vs[j + k] = jnp.where(upper, b, pltpu.roll(a, LANES - PIECE * k, 1))
    return vs


def _in_proj_kernel(x_ref, g_ref, w_ref, gq_ref, gk_ref, bd_ref, *refs):
    qkv_refs = refs[:3 * N_GROUPS]
    xs_ref, gate_ref, h_ref, slab_ref, quad_ref = refs[3 * N_GROUPS:]
    tm = x_ref.shape[0]
    h_ref[...] = _rms_norm(x_ref[...], g_ref[...]).astype(BF16)

    def proj(j):
        return jnp.dot(h_ref[...], w_ref[:, j * COL_TILE:(j + 1) * COL_TILE],
                       preferred_element_type=F32)

    def head_norm(z, gain):
        zz = (z * z).astype(BF16)
        half = COL_TILE // 2
        ss = jnp.concatenate(
            [jnp.dot(zz[:, :half], bd_ref[...], preferred_element_type=F32),
             jnp.dot(zz[:, half:], bd_ref[...], preferred_element_type=F32)], axis=1)
        return z * lax.rsqrt(ss * (1.0 / HEAD_DIM) + EPS) * gain

    def to_slabs(z):
        for c in range(N_SLABS):
            slab_ref[c] = z[:, c * LANES:(c + 1) * LANES]

    def rows_mod16(c, r):
        return quad_ref[c, r % 4, pl.ds(r // 4, tm // 16, stride=4), :]

    def split4():
        for c in range(N_SLABS):
            for rho in range(4):
                quad_ref[c, rho] = slab_ref[c, pl.ds(rho, tm // 4, stride=4), :]

    def emit_dilated(z, out_ref, d):
        if d == 1:
            out_ref[0] = z.astype(BF16)
            return
        to_slabs(z)
        if d == 16:
            split4()
        for r in range(d):
            for c in range(N_SLABS):
                rows = (rows_mod16(c, r) if d == 16
                        else slab_ref[c, pl.ds(r, tm // d, stride=d), :])
                out_ref[r, :, c * LANES:(c + 1) * LANES] = rows.astype(BF16)

    for g, d in enumerate(DILATIONS):
        emit_dilated(head_norm(proj(g), gq_ref[...]), qkv_refs[g], d)
        emit_dilated(head_norm(proj(N_GROUPS + g), gk_ref[...]), qkv_refs[N_GROUPS + g], d)
        emit_dilated(proj(2 * N_GROUPS + g), qkv_refs[2 * N_GROUPS + g], d)

    to_slabs(proj(3 * N_GROUPS))
    split4()
    for a in range(N_SLABS):
        for half in range(CHUNK // PIECES):
            vs = [rows_mod16(a, half * PIECES + j) for j in range(PIECES)]
            for i, w in enumerate(_piece_transpose(vs)):
                xs_ref[a * PIECES + i, :, half * LANES:(half + 1) * LANES] = w.astype(BF16)

    for j in range(2 * D_MODEL // COL_TILE):
        cols = slice(j * COL_TILE, (j + 1) * COL_TILE)
        gate_ref[:, cols] = jax.nn.sigmoid(proj(3 * N_GROUPS + 1 + j)).astype(BF16)


def _in_proj(x, g_mix, w_in, gq, gk, bd, batch, seq):
    n = x.shape[0]
    tm = TOKEN_TILE
    tiles = seq // tm
    row = lambda i: (i, 0)
    dil_specs, dil_shapes = [], []
    for _ in range(3):
        for d in DILATIONS:
            dil_specs.append(pl.BlockSpec((None, d, tm // d, ATTN_WIDTH),
                                          lambda i: (i // tiles, 0, i % tiles, 0)))
            dil_shapes.append(jax.ShapeDtypeStruct((batch, d, seq // d, ATTN_WIDTH), BF16))
    outs = pl.pallas_call(
        _in_proj_kernel,
        grid=(n // tm,),
        in_specs=[pl.BlockSpec((tm, D_MODEL), row),
                  _resident((1, D_MODEL)),
                  _resident((D_MODEL, IN_COLS)),
                  _resident((1, COL_TILE)),
                  _resident((1, COL_TILE)),
                  _resident((COL_TILE // 2, COL_TILE // 2))],
        out_specs=dil_specs + [
            pl.BlockSpec((SSM_GROUPS, tm // CHUNK, CHUNK_COLS), lambda i: (0, i, 0)),
            pl.BlockSpec((tm, 2 * D_MODEL), row)],
        out_shape=dil_shapes + [
            jax.ShapeDtypeStruct((SSM_GROUPS, n // CHUNK, CHUNK_COLS), BF16),
            jax.ShapeDtypeStruct((n, 2 * D_MODEL), BF16)],
        scratch_shapes=[pltpu.VMEM((tm, D_MODEL), BF16),
                        pltpu.VMEM((N_SLABS, tm, LANES), F32),
                        pltpu.VMEM((N_SLABS, 4, tm // 4, LANES), F32)],
        compiler_params=_params("arbitrary"),
        name="in_proj",
    )(x, g_mix, w_in, gq, gk, bd)
    return outs[0:3], outs[3:6], outs[6:9], outs[9], outs[10]


def _attn_kernel(q_ref, k_ref, v_ref, *refs, dilation, merge):
    if merge:
        other = (refs[0], refs[1]), (refs[2], refs[3])
        a_ref, kbuf, vbuf = refs[4:]
    else:
        o_ref, lse_ref, kbuf, vbuf = refs
    n = pl.program_id(1)
    r = pl.program_id(2)
    kb = kbuf.at[r]
    vb = vbuf.at[r]

    @pl.when(n == 0)
    def _():
        kb[0:BLK, :] = jnp.zeros((BLK, ATTN_WIDTH), BF16)
        vb[0:BLK, :] = jnp.zeros((BLK, ATTN_WIDTH), BF16)

    @pl.when(n > 0)
    def _():
        kb[0:BLK, :] = kb[BLK:2 * BLK, :]
        vb[0:BLK, :] = vb[BLK:2 * BLK, :]

    kb[BLK:2 * BLK, :] = k_ref[...]
    vb[BLK:2 * BLK, :] = v_ref[...]

    qi = lax.broadcasted_iota(jnp.int32, (BLK, 2 * BLK), 0)
    kj = lax.broadcasted_iota(jnp.int32, (BLK, 2 * BLK), 1)
    first_prev = jnp.where(n > 0, qi, 2 * BLK)
    neg = jnp.full((BLK, 2 * BLK), -jnp.inf, F32)
    zero = jnp.zeros((BLK, 2 * BLK), F32)
    bias = jnp.where(kj < BLK, jnp.where(kj >= first_prev, zero, neg),
                     jnp.where(kj - BLK <= qi, zero, neg))
    bias = jnp.concatenate([bias, bias], axis=0)

    lane_q = lax.broadcasted_iota(jnp.int32, (BLK, LANES), 1) < HEAD_DIM
    lane_kv = lax.broadcasted_iota(jnp.int32, (2 * BLK, LANES), 1) < HEAD_DIM
    rows = slice(None) if dilation == 1 else pl.ds(r, BLK, stride=dilation)

    for p in range(HEADS // 2):
        cols = slice(p * LANES, (p + 1) * LANES)
        qp = q_ref[:, cols]
        zq = jnp.zeros_like(qp)
        q2 = jnp.concatenate([jnp.where(lane_q, qp, zq), jnp.where(lane_q, zq, qp)], axis=0)
        s = lax.dot_general(q2, kb[:, cols], (((1,), (1,)), ((), ())),
                            preferred_element_type=F32) + bias
        m = jnp.max(s, axis=-1, keepdims=True)
        e = jnp.exp2(s - m)
        den = jnp.sum(e, axis=-1, keepdims=True)
        eb = e.astype(BF16)
        vp = vb[:, cols]
        zv = jnp.zeros_like(vp)
        v2 = jnp.concatenate([jnp.where(lane_kv, vp, zv), jnp.where(lane_kv, zv, vp)], axis=0)
        acc = jnp.dot(jnp.concatenate([eb[:BLK], eb[BLK:]], axis=1), v2,
                      preferred_element_type=F32)
        lse = (m + jnp.log2(den)) * LN2
        o = acc / jnp.where(lane_q, den[:BLK], den[BLK:])
        lse = jnp.where(lane_q, lse[:BLK], lse[BLK:])
        if merge:
            (o_a, l_a), (o_b, l_b) = [(o_r[p], l_r[p]) for o_r, l_r in other]
            top = jnp.maximum(jnp.maximum(lse, l_a), l_b)
            e0, e1, e2 = jnp.exp(lse - top), jnp.exp(l_a - top), jnp.exp(l_b - top)
            a_ref[:, cols] = ((e0 * o + e1 * o_a + e2 * o_b) / (e0 + e1 + e2)).astype(BF16)
        else:
            o_ref[p, rows, :] = o
            lse_ref[p, rows, :] = lse


def _attention_group(q, k, v, batch, seq, group, others=None):
    d = DILATIONS[group]
    m = seq // d
    nblk = m // BLK
    merge = others is not None
    in_spec = pl.BlockSpec((None, None, BLK, ATTN_WIDTH), lambda b, n, r: (b, r, n, 0))
    slab_spec = pl.BlockSpec((None, N_SLABS, BLK * d, LANES), lambda b, n, r: (b, 0, n, 0))
    slab_shape = jax.ShapeDtypeStruct((batch, N_SLABS, seq, LANES), F32)
    if merge:
        assert d == 1
        extra = [t for pair in others for t in pair]
        out_specs = pl.BlockSpec((BLK, ATTN_WIDTH), lambda b, n, r: (b * nblk + n, 0))
        out_shape = jax.ShapeDtypeStruct((batch * seq, ATTN_WIDTH), BF16)
    else:
        extra = []
        out_specs = [slab_spec, slab_spec]
        out_shape = [slab_shape, slab_shape]
    return pl.pallas_call(
        functools.partial(_attn_kernel, dilation=d, merge=merge),
        grid=(batch, nblk, d),
        in_specs=[in_spec, in_spec, in_spec] + [slab_spec] * len(extra),
        out_specs=out_specs,
        out_shape=out_shape,
        scratch_shapes=[pltpu.VMEM((d, 2 * BLK, ATTN_WIDTH), BF16),
                        pltpu.VMEM((d, 2 * BLK, ATTN_WIDTH), BF16)],
        compiler_params=_params("arbitrary", "arbitrary", "arbitrary"),
        name=f"attn_d{d}",
    )(q, k, v, *extra)


def _ssm_tables(lam_re, lam_im, log_dt, b_re, b_im, c_re, c_im):
    hi = lax.Precision.HIGHEST
    depth = lam_re.shape[0]
    lr = lam_re.astype(F32)
    li = lam_im.astype(F32)
    dt = jnp.exp(log_dt.astype(F32))[..., None]
    mag = jnp.exp(lr * dt)
    ang = li * dt
    abar_re = mag * jnp.cos(ang)
    abar_im = mag * jnp.sin(ang)
    nr = abar_re - 1.0
    ni = abar_im
    den = lr * lr + li * li
    cr = ((nr * lr + ni * li) / den)[:, :, None, :]
    ci = ((ni * lr - nr * li) / den)[:, :, None, :]
    brt = b_re.astype(F32).transpose(0, 1, 3, 2)
    bit = b_im.astype(F32).transpose(0, 1, 3, 2)
    bbar_re = cr * brt - ci * bit
    bbar_im = cr * bit + ci * brt

    pw_re = [jnp.ones_like(abar_re)]
    pw_im = [jnp.zeros_like(abar_im)]
    for _ in range(CHUNK):
        pr, pi = pw_re[-1], pw_im[-1]
        pw_re.append(pr * abar_re - pi * abar_im)
        pw_im.append(pr * abar_im + pi * abar_re)
    pw_re = jnp.stack(pw_re, axis=2)
    pw_im = jnp.stack(pw_im, axis=2)

    wide = (CHUNK + 1) * SSM_GROUP
    lane = jnp.arange(wide)
    rep = (lane[None, :] // SSM_GROUP == jnp.arange(CHUNK + 1)[:, None]).astype(F32)
    til = (lane[None, :] % SSM_GROUP == jnp.arange(SSM_GROUP)[:, None]).astype(F32)
    pr_l = jnp.einsum('dgtp,tl->dgpl', pw_re, rep, precision=hi)
    pi_l = jnp.einsum('dgtp,tl->dgpl', pw_im, rep, precision=hi)
    cr_l = jnp.einsum('dgcp,cl->dgpl', c_re.astype(F32), til, precision=hi)
    ci_l = jnp.einsum('dgcp,cl->dgpl', c_im.astype(F32), til, precision=hi)
    cp_re = cr_l * pr_l - ci_l * pi_l
    cp_im = cr_l * pi_l + ci_l * pr_l

    lag = (jnp.einsum('dgcp,dgpl->dgcl', bbar_re, cp_re[..., :CHUNK_COLS], precision=hi)
           - jnp.einsum('dgcp,dgpl->dgcl', bbar_im, cp_im[..., :CHUNK_COLS], precision=hi))
    toep = jnp.stack(
        [jnp.pad(lag[..., :CHUNK_COLS - PIECE * s], ((0, 0), (0, 0), (0, 0), (PIECE * s, 0)))
         for s in range(CHUNK)], axis=2)
    toep = toep.reshape(depth, SSM_GROUPS, CHUNK_COLS, CHUNK_COLS)

    rev_re = pw_re[:, :, CHUNK - 1::-1][:, :, :, None, :]
    rev_im = pw_im[:, :, CHUNK - 1::-1][:, :, :, None, :]
    win_re = (rev_re * bbar_re[:, :, None] - rev_im * bbar_im[:, :, None])
    win_im = (rev_re * bbar_im[:, :, None] + rev_im * bbar_re[:, :, None])
    win_re = win_re.reshape(depth, SSM_GROUPS, CHUNK_COLS, SSM_STATE)
    win_im = win_im.reshape(depth, SSM_GROUPS, CHUNK_COLS, SSM_STATE)

    wout_re = cp_re[..., SSM_GROUP:]
    wout_im = -cp_im[..., SSM_GROUP:]

    odd = (jnp.arange(SSM_GROUPS) % 2 == 1)[None, :, None, None]
    zc = jnp.zeros_like(win_re)
    w_in = jnp.concatenate([jnp.where(odd, zc, win_re), jnp.where(odd, win_re, zc),
                            jnp.where(odd, zc, win_im), jnp.where(odd, win_im, zc)], axis=3)
    zr = jnp.zeros_like(wout_re)
    w_out = jnp.concatenate([jnp.where(odd, zr, wout_re), jnp.where(odd, wout_re, zr),
                             jnp.where(odd, zr, wout_im), jnp.where(odd, wout_im, zr)], axis=2)
    a_re = pw_re[:, :, CHUNK].reshape(depth, STATE_ROWS, LANES)
    a_im = pw_im[:, :, CHUNK].reshape(depth, STATE_ROWS, LANES)
    return toep.astype(BF16), w_in.astype(BF16), w_out.astype(BF16), a_re, a_im


def _ssm_kernel(x_ref, win_ref, toep_ref, wout_ref, are_ref, aim_ref, d_ref, y_ref,
                sre_ref, sim_ref, *, batch, per_batch):
    phase = pl.program_id(0)
    pair = pl.program_id(1)
    nchunk = batch * per_batch
    pair_rows = pl.ds(pair, nchunk, stride=STATE_ROWS)

    @pl.when(phase == 0)
    def _():
        v = (jnp.dot(x_ref[0], win_ref[0], preferred_element_type=F32)
             + jnp.dot(x_ref[1], win_ref[1], preferred_element_type=F32))
        sre_ref[pair_rows, :] = v[:, :LANES]
        sim_ref[pair_rows, :] = v[:, LANES:]

    @pl.when((phase == 0) & (pair == pl.num_programs(1) - 1))
    def _():
        ar = are_ref[...]
        ai = aim_ref[...]

        def body(k, carry):
            new = []
            for b in range(batch):
                sre, sim = carry[b]
                rows = pl.ds(pl.multiple_of((b * per_batch + k) * STATE_ROWS, STATE_ROWS),
                             STATE_ROWS)
                vre = sre_ref[rows, :]
                vim = sim_ref[rows, :]
                sre_ref[rows, :] = sre
                sim_ref[rows, :] = sim
                new.append((ar * sre - ai * sim + vre, ar * sim + ai * sre + vim))
            return tuple(new)

        zero = jnp.zeros((STATE_ROWS, LANES), F32)
        lax.fori_loop(0, per_batch, body, tuple((zero, zero) for _ in range(batch)))

    @pl.when(phase == 1)
    def _():
        sp = jnp.concatenate([sre_ref[pair_rows, :], sim_ref[pair_rows, :]],
                             axis=1).astype(BF16)
        for h in range(2):
            x = x_ref[h]
            y = (jnp.dot(x, toep_ref[h], preferred_element_type=F32)
                 + jnp.dot(sp, wout_ref[h], preferred_element_type=F32)
                 + d_ref[h] * x.astype(F32))
            y_ref[h] = jax.nn.gelu(y).astype(BF16)


def _ssm(x, toep, w_in, w_out, a_re, a_im, d_tiled, batch, seq):
    nchunk = x.shape[1]
    per_batch = seq // CHUNK
    pair3 = lambda shape: pl.BlockSpec((2,) + shape, lambda ph, q: (q, 0, 0))
    return pl.pallas_call(
        functools.partial(_ssm_kernel, batch=batch, per_batch=per_batch),
        grid=(2, SSM_GROUPS // 2),
        in_specs=[pair3((nchunk, CHUNK_COLS)), pair3((CHUNK_COLS, CHUNK_COLS)),
                  pair3((CHUNK_COLS, CHUNK_COLS)), pair3((CHUNK_COLS, CHUNK_COLS)),
                  _resident((STATE_ROWS, LANES)), _resident((STATE_ROWS, LANES)),
                  pair3((1, CHUNK_COLS))],
        out_specs=pl.BlockSpec((2, nchunk, CHUNK_COLS), lambda ph, q: (q * ph, 0, 0)),
        out_shape=jax.ShapeDtypeStruct((SSM_GROUPS, nchunk, CHUNK_COLS), BF16),
        scratch_shapes=[pltpu.VMEM((nchunk * STATE_ROWS, LANES), F32),
                        pltpu.VMEM((nchunk * STATE_ROWS, LANES), F32)],
        compiler_params=_params("arbitrary", "arbitrary"),
        name="ssm",
    )(x, w_in, toep, w_out, a_re, a_im, d_tiled)


FF_CHUNK = D_FF // 2


def _mix_ffn_kernel(x_ref, a_ref, ys_ref, gate_ref, wap_ref, wa_ref, wb_ref, wout_ref,
                    g_ref, wg_ref, wu_ref, wd_ref, out_ref, slab_ref):
    tm = x_ref.shape[0]
    a_out = jnp.dot(a_ref[...], wap_ref[...], preferred_element_type=F32)

    for s in range(N_SLABS):
        for half in range(CHUNK // PIECES):
            ws = [ys_ref[s * PIECES + i, :, half * LANES:(half + 1) * LANES].astype(F32)
                  for i in range(PIECES)]
            for j, v in enumerate(_piece_transpose(ws)):
                slab_ref[s, pl.ds(half * PIECES + j, tm // CHUNK, stride=CHUNK), :] = v
    y = jnp.concatenate([slab_ref[c] for c in range(N_SLABS)], axis=1).astype(BF16)
    s_out = (jnp.dot(y, wa_ref[...], preferred_element_type=F32)
             * jax.nn.sigmoid(jnp.dot(y, wb_ref[...], preferred_element_type=F32)))
    mix = (gate_ref[:, :D_MODEL].astype(F32) * a_out
           + gate_ref[:, D_MODEL:].astype(F32) * s_out)
    x = x_ref[...] + jnp.dot(mix.astype(BF16), wout_ref[...], preferred_element_type=F32)

    h = _rms_norm(x, g_ref[...]).astype(BF16)
    for c in range(D_FF // FF_CHUNK):
        cols = slice(c * FF_CHUNK, (c + 1) * FF_CHUNK)
        gate = jnp.dot(h, wg_ref[:, cols], preferred_element_type=F32)
        up = jnp.dot(h, wu_ref[:, cols], preferred_element_type=F32)
        hidden = (jax.nn.silu(gate) * up).astype(BF16)
        x = x + jnp.dot(hidden, wd_ref[cols, :], preferred_element_type=F32)
    out_ref[...] = x


def _mix_ffn(x, a, ys, gates, wap, wa, wb, wout, g_ffn, wg, wu, wd):
    n = x.shape[0]
    tm = TOKEN_TILE
    row = lambda i: (i, 0)
    return pl.pallas_call(
        _mix_ffn_kernel,
        grid=(n // tm,),
        in_specs=[pl.BlockSpec((tm, D_MODEL), row),
                  pl.BlockSpec((tm, ATTN_WIDTH), row),
                  pl.BlockSpec((SSM_GROUPS, tm // CHUNK, CHUNK_COLS), lambda i: (0, i, 0)),
                  pl.BlockSpec((tm, 2 * D_MODEL), row),
                  _resident((ATTN_WIDTH, D_MODEL)), _resident((SSM_WIDTH, D_MODEL)),
                  _resident((SSM_WIDTH, D_MODEL)), _resident((D_MODEL, D_MODEL)),
                  _resident((1, D_MODEL)), _resident((D_MODEL, D_FF)),
                  _resident((D_MODEL, D_FF)), _resident((D_FF, D_MODEL))],
        out_specs=pl.BlockSpec((tm, D_MODEL), row),
        out_shape=jax.ShapeDtypeStruct((n, D_MODEL), F32),
        scratch_shapes=[pltpu.VMEM((N_SLABS, tm, LANES), F32)],
        compiler_params=_params("arbitrary"),
        name="mix_ffn",
    )(x, a, ys, gates, wap, wa, wb, wout, g_ffn, wg, wu, wd)


def kernel(x, g_mix, w_in, g_q, g_k, w_attn_proj, lambda_re, lambda_im, log_dt, b_re, b_im,
           c_re, c_im, d_skip, w_glu_a, w_glu_b, w_out, g_ffn, w_ffn_gate, w_ffn_up, w_ffn_down):
    batch, seq, _ = x.shape
    depth = w_in.shape[0]
    n = batch * seq
    xf = x.reshape(n, D_MODEL).astype(F32)

    blk = jnp.arange(COL_TILE // 2) // HEAD_DIM
    bd = (blk[:, None] == blk[None, :]).astype(BF16)
    gq = jnp.tile(g_q.astype(F32) * (LOG2E * HEAD_DIM ** -0.5), (1, HEADS))[:, None]
    gk = jnp.tile(g_k.astype(F32), (1, HEADS))[:, None]
    toep, s_in, s_out, a_re, a_im = _ssm_tables(lambda_re, lambda_im, log_dt, b_re, b_im,
                                                c_re, c_im)
    d_tiled = jnp.tile(d_skip.astype(F32).reshape(depth, SSM_GROUPS, 1, SSM_GROUP),
                       (1, 1, 1, CHUNK))

    for l in range(depth):
        q, k, v, xs, gates = _in_proj(xf, g_mix[l].astype(F32)[None], w_in[l].astype(BF16),
                                      gq[l], gk[l], bd, batch, seq)
        dilated = [_attention_group(q[g], k[g], v[g], batch, seq, g)
                   for g in range(1, N_GROUPS)]
        a = _attention_group(q[0], k[0], v[0], batch, seq, 0, others=dilated)
        ys = _ssm(xs, toep[l], s_in[l], s_out[l], a_re[l], a_im[l], d_tiled[l], batch, seq)
        xf = _mix_ffn(xf, a, ys, gates, w_attn_proj[l].astype(BF16), w_glu_a[l].astype(BF16),
                      w_glu_b[l].astype(BF16), w_out[l].astype(BF16),
                      g_ffn[l].astype(F32)[None], w_ffn_gate[l].astype(BF16),
                      w_ffn_up[l].astype(BF16), w_ffn_down[l].astype(BF16))
    return xf.reshape(batch, seq, D_MODEL).astype(x.dtype)
```

```python
import functools

import jax
import jax.numpy as jnp
from jax import lax
from jax.experimental import pallas as pl
from jax.experimental.pallas import tpu as pltpu

F32 = jnp.float32
BF16 = jnp.bfloat16

D_MODEL = 1024
HEAD_DIM = 64
HEADS = 8
ATTN_WIDTH = HEADS * HEAD_DIM
DILATIONS = (1, 4, 16)
N_GROUPS = len(DILATIONS)
BLK = 128
SSM_WIDTH = 512
SSM_GROUP = 16
SSM_GROUPS = 32
SSM_STATE = 64
CHUNK = 16
CHUNK_COLS = CHUNK * SSM_GROUP
D_FF = 2816
IN_COLS = 3 * N_GROUPS * ATTN_WIDTH + SSM_WIDTH + 2 * D_MODEL
EPS = 1e-6
LOG2E = 1.4426950408889634
LN2 = 0.6931471805599453

VMEM_LIMIT_BYTES = 56 * 1024 * 1024
LANES = 128
COL_TILE = 512
TOKEN_TILE = 512
PIECE = SSM_GROUP
PIECES = LANES // PIECE
N_SLABS = ATTN_WIDTH // LANES
STATE_ROWS = SSM_GROUPS // 2


def _params(*semantics):
    return pltpu.CompilerParams(dimension_semantics=semantics,
                                vmem_limit_bytes=VMEM_LIMIT_BYTES)


def _resident(shape):
    return pl.BlockSpec(shape, lambda *_: (0,) * len(shape),
                        pipeline_mode=pl.Buffered(1))


def _rms_norm(x, gain):
    ms = jnp.mean(x * x, axis=-1, keepdims=True)
    return x * lax.rsqrt(ms + EPS) * gain


def _piece_transpose(vs):
    vs = list(vs)
    piece = lax.broadcasted_iota(jnp.int32, vs[0].shape, 1) // PIECE
    for k in (4, 2, 1):
        upper = (piece & k) != 0
        for j in range(PIECES):
            if j & k:
                continue
            a, b = vs[j], vs[j + k]
            vs[j] = jnp.where(upper, pltpu.roll(b, PIECE * k, 1), a)
            vs[j + k] = jnp.where(upper, b, pltpu.roll(a, LANES - PIECE * k, 1))
    return vs


def _in_proj_kernel(x_ref, g_ref, w_ref, gq_ref, gk_ref, bd_ref, *refs):
    qkv_refs = refs[:3 * N_GROUPS]
    xs_ref, gate_ref, h_ref, slab_ref, quad_ref = refs[3 * N_GROUPS:]
    tm = x_ref.shape[0]
    h_ref[...] = _rms_norm(x_ref[...], g_ref[...]).astype(BF16)

    def proj(j):
        return jnp.dot(h_ref[...], w_ref[:, j * COL_TILE:(j + 1) * COL_TILE],
                       preferred_element_type=F32)

    def head_norm(z, gain):
        zz = (z * z).astype(BF16)
        half = COL_TILE // 2
        ss = jnp.concatenate(
            [jnp.dot(zz[:, :half], bd_ref[...], preferred_element_type=F32),
             jnp.dot(zz[:, half:], bd_ref[...], preferred_element_type=F32)], axis=1)
        return z * lax.rsqrt(ss * (1.0 / HEAD_DIM) + EPS) * gain

    def to_slabs(z):
        for c in range(N_SLABS):
            slab_ref[c] = z[:, c * LANES:(c + 1) * LANES]

    def rows_mod16(c, r):
        return quad_ref[c, r % 4, pl.ds(r // 4, tm // 16, stride=4), :]

    def split4():
        for c in range(N_SLABS):
            for rho in range(4):
                quad_ref[c, rho] = slab_ref[c, pl.ds(rho, tm // 4, stride=4), :]

    def emit_dilated(z, out_ref, d):
        if d == 1:
            out_ref[0] = z.astype(BF16)
            return
        to_slabs(z)
        if d == 16:
            split4()
        for r in range(d):
            for c in range(N_SLABS):
                rows = (rows_mod16(c, r) if d == 16
                        else slab_ref[c, pl.ds(r, tm // d, stride=d), :])
                out_ref[r, :, c * LANES:(c + 1) * LANES] = rows.astype(BF16)

    for g, d in enumerate(DILATIONS):
        emit_dilated(head_norm(proj(g), gq_ref[...]), qkv_refs[g], d)
        emit_dilated(head_norm(proj(N_GROUPS + g), gk_ref[...]), qkv_refs[N_GROUPS + g], d)
        emit_dilated(proj(2 * N_GROUPS + g), qkv_refs[2 * N_GROUPS + g], d)

    to_slabs(proj(3 * N_GROUPS))
    split4()
    for a in range(N_SLABS):
        for half in range(CHUNK // PIECES):
            vs = [rows_mod16(a, half * PIECES + j) for j in range(PIECES)]
            for i, w in enumerate(_piece_transpose(vs)):
                xs_ref[a * PIECES + i, :, half * LANES:(half + 1) * LANES] = w.astype(BF16)

    for j in range(2 * D_MODEL // COL_TILE):
        cols = slice(j * COL_TILE, (j + 1) * COL_TILE)
        gate_ref[:, cols] = jax.nn.sigmoid(proj(3 * N_GROUPS + 1 + j)).astype(BF16)


def _in_proj(x, g_mix, w_in, gq, gk, bd, batch, seq):
    n = x.shape[0]
    tm = TOKEN_TILE
    tiles = seq // tm
    row = lambda i: (i, 0)
    dil_specs, dil_shapes = [], []
    for _ in range(3):
        for d in DILATIONS:
            dil_specs.append(pl.BlockSpec((None, d, tm // d, ATTN_WIDTH),
                                          lambda i: (i // tiles, 0, i % tiles, 0)))
            dil_shapes.append(jax.ShapeDtypeStruct((batch, d, seq // d, ATTN_WIDTH), BF16))
    outs = pl.pallas_call(
        _in_proj_kernel,
        grid=(n // tm,),
        in_specs=[pl.BlockSpec((tm, D_MODEL), row),
                  _resident((1, D_MODEL)),
                  _resident((D_MODEL, IN_COLS)),
                  _resident((1, COL_TILE)),
                  _resident((1, COL_TILE)),
                  _resident((COL_TILE // 2, COL_TILE // 2))],
        out_specs=dil_specs + [
            pl.BlockSpec((SSM_GROUPS, tm // CHUNK, CHUNK_COLS), lambda i: (0, i, 0)),
            pl.BlockSpec((tm, 2 * D_MODEL), row)],
        out_shape=dil_shapes + [
            jax.ShapeDtypeStruct((SSM_GROUPS, n // CHUNK, CHUNK_COLS), BF16),
            jax.ShapeDtypeStruct((n, 2 * D_MODEL), BF16)],
        scratch_shapes=[pltpu.VMEM((tm, D_MODEL), BF16),
                        pltpu.VMEM((N_SLABS, tm, LANES), F32),
                        pltpu.VMEM((N_SLABS, 4, tm // 4, LANES), F32)],
        compiler_params=_params("arbitrary"),
        name="in_proj",
    )(x, g_mix, w_in, gq, gk, bd)
    return outs[0:3], outs[3:6], outs[6:9], outs[9], outs[10]


ATTN_BLOCKS = {1: 4, 4: 4, 16: 1}
ATTN_RESIDUES = {1: 1, 4: 1, 16: 4}


def _attn_kernel(q_ref, k_ref, v_ref, *refs, dilation, nb, nr, merge):
    if merge:
        other = (refs[0], refs[1]), (refs[2], refs[3])
        a_ref, kbuf, vbuf = refs[4:]
    else:
        o_ref, lse_ref, kbuf, vbuf = refs
    step = pl.program_id(1)
    last = slice((nb - 1) * BLK, nb * BLK)

    qi = lax.broadcasted_iota(jnp.int32, (BLK, 2 * BLK), 0)
    kj = lax.broadcasted_iota(jnp.int32, (BLK, 2 * BLK), 1)
    neg = jnp.full((BLK, 2 * BLK), -jnp.inf, F32)
    zero = jnp.zeros((BLK, 2 * BLK), F32)
    cur_bias = jnp.where(kj - BLK <= qi, zero, neg)
    band = jnp.where(kj < BLK, jnp.where(kj >= qi, zero, neg), cur_bias)
    head = jnp.where(kj < BLK, jnp.where(step > 0, band, neg), cur_bias)
    band = jnp.concatenate([band, band], axis=0)
    head = jnp.concatenate([head, head], axis=0)

    lane_q = lax.broadcasted_iota(jnp.int32, (BLK, LANES), 1) < HEAD_DIM
    lane_kv = lax.broadcasted_iota(jnp.int32, (2 * BLK, LANES), 1) < HEAD_DIM

    for rr in range(nr):
        r = pl.program_id(2) * nr + rr
        kb = kbuf.at[r]
        vb = vbuf.at[r]

        @pl.when(step == 0)
        def _():
            kb[0:BLK, :] = jnp.zeros((BLK, ATTN_WIDTH), BF16)
            vb[0:BLK, :] = jnp.zeros((BLK, ATTN_WIDTH), BF16)

        kb[BLK:2 * BLK, :] = k_ref[rr, 0:BLK, :]
        vb[BLK:2 * BLK, :] = v_ref[rr, 0:BLK, :]

        for j in range(nb):
            for p in range(HEADS // 2):
                cols = slice(p * LANES, (p + 1) * LANES)
                if j == 0:
                    keys, vals, bias = kb[:, cols], vb[:, cols], head
                else:
                    window = slice((j - 1) * BLK, (j + 1) * BLK)
                    keys, vals, bias = k_ref[rr, window, cols], v_ref[rr, window, cols], band
                qp = q_ref[rr, j * BLK:(j + 1) * BLK, cols]
                zq = jnp.zeros_like(qp)
                q2 = jnp.concatenate([jnp.where(lane_q, qp, zq), jnp.where(lane_q, zq, qp)],
                                     axis=0)
                s = lax.dot_general(q2, keys, (((1,), (1,)), ((), ())),
                                    preferred_element_type=F32) + bias
                m = jnp.max(s, axis=-1, keepdims=True)
                e = jnp.exp2(s - m)
                den = jnp.sum(e, axis=-1, keepdims=True)
                eb = e.astype(BF16)
                zv = jnp.zeros_like(vals)
                v2 = jnp.concatenate([jnp.where(lane_kv, vals, zv),
                                      jnp.where(lane_kv, zv, vals)], axis=0)
                acc = jnp.dot(jnp.concatenate([eb[:BLK], eb[BLK:]], axis=1), v2,
                              preferred_element_type=F32)
                lse = (m + jnp.log2(den)) * LN2
                o = acc / jnp.where(lane_q, den[:BLK], den[BLK:])
                lse = jnp.where(lane_q, lse[:BLK], lse[BLK:])
                if merge:
                    rows = slice(j * BLK, (j + 1) * BLK)
                    (o_a, l_a), (o_b, l_b) = [(o_r[p, rows, :], l_r[p, rows, :])
                                              for o_r, l_r in other]
                    top = jnp.maximum(jnp.maximum(lse, l_a), l_b)
                    e0, e1, e2 = jnp.exp(lse - top), jnp.exp(l_a - top), jnp.exp(l_b - top)
                    a_ref[rows, cols] = ((e0 * o + e1 * o_a + e2 * o_b)
                                         / (e0 + e1 + e2)).astype(BF16)
                else:
                    rows = pl.ds(j * BLK * dilation + r, BLK, stride=dilation)
                    o_ref[p, rows, :] = o
                    lse_ref[p, rows, :] = lse

        kb[0:BLK, :] = k_ref[rr, last, :]
        vb[0:BLK, :] = v_ref[rr, last, :]


def _attention_group(q, k, v, batch, seq, group, others=None):
    d = DILATIONS[group]
    nb, nr = ATTN_BLOCKS[d], ATTN_RESIDUES[d]
    steps = seq // d // (BLK * nb)
    merge = others is not None
    in_spec = pl.BlockSpec((None, nr, BLK * nb, ATTN_WIDTH), lambda b, n, r: (b, r, n, 0))
    slab_spec = pl.BlockSpec((None, N_SLABS, BLK * nb * d, LANES), lambda b, n, r: (b, 0, n, 0))
    slab_shape = jax.ShapeDtypeStruct((batch, N_SLABS, seq, LANES), F32)
    if merge:
        assert d == 1
        extra = [t for pair in others for t in pair]
        out_specs = pl.BlockSpec((BLK * nb, ATTN_WIDTH), lambda b, n, r: (b * steps + n, 0))
        out_shape = jax.ShapeDtypeStruct((batch * seq, ATTN_WIDTH), BF16)
    else:
        extra = []
        out_specs = [slab_spec, slab_spec]
        out_shape = [slab_shape, slab_shape]
    return pl.pallas_call(
        functools.partial(_attn_kernel, dilation=d, nb=nb, nr=nr, merge=merge),
        grid=(batch, steps, d // nr),
        in_specs=[in_spec, in_spec, in_spec] + [slab_spec] * len(extra),
        out_specs=out_specs,
        out_shape=out_shape,
        scratch_shapes=[pltpu.VMEM((d, 2 * BLK, ATTN_WIDTH), BF16),
                        pltpu.VMEM((d, 2 * BLK, ATTN_WIDTH), BF16)],
        compiler_params=_params("arbitrary", "arbitrary", "arbitrary"),
        name=f"attn_d{d}",
    )(q, k, v, *extra)


def _ssm_tables(lam_re, lam_im, log_dt, b_re, b_im, c_re, c_im):
    hi = lax.Precision.HIGHEST
    depth = lam_re.shape[0]
    lr = lam_re.astype(F32)
    li = lam_im.astype(F32)
    dt = jnp.exp(log_dt.astype(F32))[..., None]
    mag = jnp.exp(lr * dt)
    ang = li * dt
    abar_re = mag * jnp.cos(ang)
    abar_im = mag * jnp.sin(ang)
    nr = abar_re - 1.0
    ni = abar_im
    den = lr * lr + li * li
    cr = ((nr * lr + ni * li) / den)[:, :, None, :]
    ci = ((ni * lr - nr * li) / den)[:, :, None, :]
    brt = b_re.astype(F32).transpose(0, 1, 3, 2)
    bit = b_im.astype(F32).transpose(0, 1, 3, 2)
    bbar_re = cr * brt - ci * bit
    bbar_im = cr * bit + ci * brt

    pw_re = [jnp.ones_like(abar_re)]
    pw_im = [jnp.zeros_like(abar_im)]
    for _ in range(CHUNK):
        pr, pi = pw_re[-1], pw_im[-1]
        pw_re.append(pr * abar_re - pi * abar_im)
        pw_im.append(pr * abar_im + pi * abar_re)
    pw_re = jnp.stack(pw_re, axis=2)
    pw_im = jnp.stack(pw_im, axis=2)

    wide = (CHUNK + 1) * SSM_GROUP
    lane = jnp.arange(wide)
    rep = (lane[None, :] // SSM_GROUP == jnp.arange(CHUNK + 1)[:, None]).astype(F32)
    til = (lane[None, :] % SSM_GROUP == jnp.arange(SSM_GROUP)[:, None]).astype(F32)
    pr_l = jnp.einsum('dgtp,tl->dgpl', pw_re, rep, precision=hi)
    pi_l = jnp.einsum('dgtp,tl->dgpl', pw_im, rep, precision=hi)
    cr_l = jnp.einsum('dgcp,cl->dgpl', c_re.astype(F32), til, precision=hi)
    ci_l = jnp.einsum('dgcp,cl->dgpl', c_im.astype(F32), til, precision=hi)
    cp_re = cr_l * pr_l - ci_l * pi_l
    cp_im = cr_l * pi_l + ci_l * pr_l

    lag = (jnp.einsum('dgcp,dgpl->dgcl', bbar_re, cp_re[..., :CHUNK_COLS], precision=hi)
           - jnp.einsum('dgcp,dgpl->dgcl', bbar_im, cp_im[..., :CHUNK_COLS], precision=hi))
    toep = jnp.stack(
        [jnp.pad(lag[..., :CHUNK_COLS - PIECE * s], ((0, 0), (0, 0), (0, 0), (PIECE * s, 0)))
         for s in range(CHUNK)], axis=2)
    toep = toep.reshape(depth, SSM_GROUPS, CHUNK_COLS, CHUNK_COLS)

    rev_re = pw_re[:, :, CHUNK - 1::-1][:, :, :, None, :]
    rev_im = pw_im[:, :, CHUNK - 1::-1][:, :, :, None, :]
    win_re = (rev_re * bbar_re[:, :, None] - rev_im * bbar_im[:, :, None])
    win_im = (rev_re * bbar_im[:, :, None] + rev_im * bbar_re[:, :, None])
    win_re = win_re.reshape(depth, SSM_GROUPS, CHUNK_COLS, SSM_STATE)
    win_im = win_im.reshape(depth, SSM_GROUPS, CHUNK_COLS, SSM_STATE)

    wout_re = cp_re[..., SSM_GROUP:]
    wout_im = -cp_im[..., SSM_GROUP:]

    odd = (jnp.arange(SSM_GROUPS) % 2 == 1)[None, :, None, None]
    zc = jnp.zeros_like(win_re)
    w_in = jnp.concatenate([jnp.where(odd, zc, win_re), jnp.where(odd, win_re, zc),
                            jnp.where(odd, zc, win_im), jnp.where(odd, win_im, zc)], axis=3)
    zr = jnp.zeros_like(wout_re)
    w_out = jnp.concatenate([jnp.where(odd, zr, wout_re), jnp.where(odd, wout_re, zr),
                             jnp.where(odd, zr, wout_im), jnp.where(odd, wout_im, zr)], axis=2)
    a_re = pw_re[:, :, CHUNK].reshape(depth, STATE_ROWS, LANES)
    a_im = pw_im[:, :, CHUNK].reshape(depth, STATE_ROWS, LANES)
    return toep.astype(BF16), w_in.astype(BF16), w_out.astype(BF16), a_re, a_im


def _ssm_kernel(x_ref, win_ref, toep_ref, wout_ref, are_ref, aim_ref, d_ref, y_ref,
                sre_ref, sim_ref, *, batch, per_batch):
    phase = pl.program_id(0)
    pair = pl.program_id(1)
    nchunk = batch * per_batch
    pair_rows = pl.ds(pair, nchunk, stride=STATE_ROWS)

    @pl.when(phase == 0)
    def _():
        v = (jnp.dot(x_ref[0], win_ref[0], preferred_element_type=F32)
             + jnp.dot(x_ref[1], win_ref[1], preferred_element_type=F32))
        sre_ref[pair_rows, :] = v[:, :LANES]
        sim_ref[pair_rows, :] = v[:, LANES:]

    @pl.when((phase == 0) & (pair == pl.num_programs(1) - 1))
    def _():
        ar = are_ref[...]
        ai = aim_ref[...]

        def body(k, carry):
            new = []
            for b in range(batch):
                sre, sim = carry[b]
                rows = pl.ds(pl.multiple_of((b * per_batch + k) * STATE_ROWS, STATE_ROWS),
                             STATE_ROWS)
                vre = sre_ref[rows, :]
                vim = sim_ref[rows, :]
                sre_ref[rows, :] = sre
                sim_ref[rows, :] = sim
                new.append((ar * sre - ai * sim + vre, ar * sim + ai * sre + vim))
            return tuple(new)

        zero = jnp.zeros((STATE_ROWS, LANES), F32)
        lax.fori_loop(0, per_batch, body, tuple((zero, zero) for _ in range(batch)))

    @pl.when(phase == 1)
    def _():
        sp = jnp.concatenate([sre_ref[pair_rows, :], sim_ref[pair_rows, :]],
                             axis=1).astype(BF16)
        for h in range(2):
            x = x_ref[h]
            y = (jnp.dot(x, toep_ref[h], preferred_element_type=F32)
                 + jnp.dot(sp, wout_ref[h], preferred_element_type=F32)
                 + d_ref[h] * x.astype(F32))
            y_ref[h] = jax.nn.gelu(y).astype(BF16)


def _ssm(x, toep, w_in, w_out, a_re, a_im, d_tiled, batch, seq):
    nchunk = x.shape[1]
    per_batch = seq // CHUNK
    pair3 = lambda shape: pl.BlockSpec((2,) + shape, lambda ph, q: (q, 0, 0))
    return pl.pallas_call(
        functools.partial(_ssm_kernel, batch=batch, per_batch=per_batch),
        grid=(2, SSM_GROUPS // 2),
        in_specs=[pair3((nchunk, CHUNK_COLS)), pair3((CHUNK_COLS, CHUNK_COLS)),
                  pair3((CHUNK_COLS, CHUNK_COLS)), pair3((CHUNK_COLS, CHUNK_COLS)),
                  _resident((STATE_ROWS, LANES)), _resident((STATE_ROWS, LANES)),
                  pair3((1, CHUNK_COLS))],
        out_specs=pl.BlockSpec((2, nchunk, CHUNK_COLS), lambda ph, q: (q * ph, 0, 0)),
        out_shape=jax.ShapeDtypeStruct((SSM_GROUPS, nchunk, CHUNK_COLS), BF16),
        scratch_shapes=[pltpu.VMEM((nchunk * STATE_ROWS, LANES), F32),
                        pltpu.VMEM((nchunk * STATE_ROWS, LANES), F32)],
        compiler_params=_params("arbitrary", "arbitrary"),
        name="ssm",
    )(x, w_in, toep, w_out, a_re, a_im, d_tiled)


FF_CHUNK = D_FF // 2


def _mix_ffn_kernel(x_ref, a_ref, ys_ref, gate_ref, wap_ref, wa_ref, wb_ref, wout_ref,
                    g_ref, wg_ref, wu_ref, wd_ref, out_ref, slab_ref):
    tm = x_ref.shape[0]
    a_out = jnp.dot(a_ref[...], wap_ref[...], preferred_element_type=F32)

    for s in range(N_SLABS):
        for half in range(CHUNK // PIECES):
            ws = [ys_ref[s * PIECES + i, :, half * LANES:(half + 1) * LANES].astype(F32)
                  for i in range(PIECES)]
            for j, v in enumerate(_piece_transpose(ws)):
                slab_ref[s, pl.ds(half * PIECES + j, tm // CHUNK, stride=CHUNK), :] = v
    y = jnp.concatenate([slab_ref[c] for c in range(N_SLABS)], axis=1).astype(BF16)
    s_out = (jnp.dot(y, wa_ref[...], preferred_element_type=F32)
             * jax.nn.sigmoid(jnp.dot(y, wb_ref[...], preferred_element_type=F32)))
    mix = (gate_ref[:, :D_MODEL].astype(F32) * a_out
           + gate_ref[:, D_MODEL:].astype(F32) * s_out)
    x = x_ref[...] + jnp.dot(mix.astype(BF16), wout_ref[...], preferred_element_type=F32)

    h = _rms_norm(x, g_ref[...]).astype(BF16)
    for c in range(D_FF // FF_CHUNK):
        cols = slice(c * FF_CHUNK, (c + 1) * FF_CHUNK)
        gate = jnp.dot(h, wg_ref[:, cols], preferred_element_type=F32)
        up = jnp.dot(h, wu_ref[:, cols], preferred_element_type=F32)
        hidden = (jax.nn.silu(gate) * up).astype(BF16)
        x = x + jnp.dot(hidden, wd_ref[cols, :], preferred_element_type=F32)
    out_ref[...] = x


def _mix_ffn(x, a, ys, gates, wap, wa, wb, wout, g_ffn, wg, wu, wd):
    n = x.shape[0]
    tm = TOKEN_TILE
    row = lambda i: (i, 0)
    return pl.pallas_call(
        _mix_ffn_kernel,
        grid=(n // tm,),
        in_specs=[pl.BlockSpec((tm, D_MODEL), row),
                  pl.BlockSpec((tm, ATTN_WIDTH), row),
                  pl.BlockSpec((SSM_GROUPS, tm // CHUNK, CHUNK_COLS), lambda i: (0, i, 0)),
                  pl.BlockSpec((tm, 2 * D_MODEL), row),
                  _resident((ATTN_WIDTH, D_MODEL)), _resident((SSM_WIDTH, D_MODEL)),
                  _resident((SSM_WIDTH, D_MODEL)), _resident((D_MODEL, D_MODEL)),
                  _resident((1, D_MODEL)), _resident((D_MODEL, D_FF)),
                  _resident((D_MODEL, D_FF)), _resident((D_FF, D_MODEL))],
        out_specs=pl.BlockSpec((tm, D_MODEL), row),
        out_shape=jax.ShapeDtypeStruct((n, D_MODEL), F32),
        scratch_shapes=[pltpu.VMEM((N_SLABS, tm, LANES), F32)],
        compiler_params=_params("arbitrary"),
        name="mix_ffn",
    )(x, a, ys, gates, wap, wa, wb, wout, g_ffn, wg, wu, wd)


def kernel(x, g_mix, w_in, g_q, g_k, w_attn_proj, lambda_re, lambda_im, log_dt, b_re, b_im,
           c_re, c_im, d_skip, w_glu_a, w_glu_b, w_out, g_ffn, w_ffn_gate, w_ffn_up, w_ffn_down):
    batch, seq, _ = x.shape
    depth = w_in.shape[0]
    n = batch * seq
    xf = x.reshape(n, D_MODEL).astype(F32)

    blk = jnp.arange(COL_TILE // 2) // HEAD_DIM
    bd = (blk[:, None] == blk[None, :]).astype(BF16)
    gq = jnp.tile(g_q.astype(F32) * (LOG2E * HEAD_DIM ** -0.5), (1, HEADS))[:, None]
    gk = jnp.tile(g_k.astype(F32), (1, HEADS))[:, None]
    toep, s_in, s_out, a_re, a_im = _ssm_tables(lambda_re, lambda_im, log_dt, b_re, b_im,
                                                c_re, c_im)
    d_tiled = jnp.tile(d_skip.astype(F32).reshape(depth, SSM_GROUPS, 1, SSM_GROUP),
                       (1, 1, 1, CHUNK))

    for l in range(depth):
        q, k, v, xs, gates = _in_proj(xf, g_mix[l].astype(F32)[None], w_in[l].astype(BF16),
                                      gq[l], gk[l], bd, batch, seq)
        dilated = [_attention_group(q[g], k[g], v[g], batch, seq, g)
                   for g in range(1, N_GROUPS)]
        a = _attention_group(q[0], k[0], v[0], batch, seq, 0, others=dilated)
        ys = _ssm(xs, toep[l], s_in[l], s_out[l], a_re[l], a_im[l], d_tiled[l], batch, seq)
        xf = _mix_ffn(xf, a, ys, gates, w_attn_proj[l].astype(BF16), w_glu_a[l].astype(BF16),
                      w_glu_b[l].astype(BF16), w_out[l].astype(BF16),
                      g_ffn[l].astype(F32)[None], w_ffn_gate[l].astype(BF16),
                      w_ffn_up[l].astype(BF16), w_ffn_down[l].astype(BF16))
    return xf.reshape(batch, seq, D_MODEL).astype(x.dtype)
```

```python
import functools

import jax
import jax.numpy as jnp
from jax import lax
from jax.experimental import pallas as pl
from jax.experimental.pallas import tpu as pltpu

F32 = jnp.float32
BF16 = jnp.bfloat16

D_MODEL = 1024
HEAD_DIM = 64
HEADS = 8
ATTN_WIDTH = HEADS * HEAD_DIM
DILATIONS = (1, 4, 16)
N_GROUPS = len(DILATIONS)
BLK = 128
SSM_WIDTH = 512
SSM_GROUP = 16
SSM_GROUPS = 32
SSM_STATE = 64
CHUNK = 16
CHUNK_COLS = CHUNK * SSM_GROUP
D_FF = 2816
IN_COLS = 3 * N_GROUPS * ATTN_WIDTH + SSM_WIDTH + 2 * D_MODEL
EPS = 1e-6
LOG2E = 1.4426950408889634
LN2 = 0.6931471805599453

VMEM_LIMIT_BYTES = 56 * 1024 * 1024
LANES = 128
COL_TILE = 512
TOKEN_TILE = 512
PIECE = SSM_GROUP
PIECES = LANES // PIECE
N_SLABS = ATTN_WIDTH // LANES
STATE_ROWS = SSM_GROUPS // 2


def _params(*semantics):
    return pltpu.CompilerParams(dimension_semantics=semantics,
                                vmem_limit_bytes=VMEM_LIMIT_BYTES)


def _resident(shape):
    return pl.BlockSpec(shape, lambda *_: (0,) * len(shape),
                        pipeline_mode=pl.Buffered(1))


def _rms_norm(x, gain):
    ms = jnp.mean(x * x, axis=-1, keepdims=True)
    return x * lax.rsqrt(ms + EPS) * gain


def _piece_transpose(vs):
    vs = list(vs)
    piece = lax.broadcasted_iota(jnp.int32, vs[0].shape, 1) // PIECE
    for k in (4, 2, 1):
        upper = (piece & k) != 0
        for j in range(PIECES):
            if j & k:
                continue
            a, b = vs[j], vs[j + k]
            vs[j] = jnp.where(upper, pltpu.roll(b, PIECE * k, 1), a)
            vs[j + k] = jnp.where(upper, b, pltpu.roll(a, LANES - PIECE * k, 1))
    return vs


def _in_proj_kernel(x_ref, g_ref, w_ref, gq_ref, gk_ref, bd_ref, *refs):
    qkv_refs = refs[:3 * N_GROUPS]
    xs_ref, gate_ref, h_ref, slab_ref, quad_ref = refs[3 * N_GROUPS:]
    tm = x_ref.shape[0]
    h_ref[...] = _rms_norm(x_ref[...], g_ref[...]).astype(BF16)

    def proj(j):
        return jnp.dot(h_ref[...], w_ref[:, j * COL_TILE:(j + 1) * COL_TILE],
                       preferred_element_type=F32)

    def head_norm(z, gain):
        zz = (z * z).astype(BF16)
        half = COL_TILE // 2
        ss = jnp.concatenate(
            [jnp.dot(zz[:, :half], bd_ref[...], preferred_element_type=F32),
             jnp.dot(zz[:, half:], bd_ref[...], preferred_element_type=F32)], axis=1)
        return z * lax.rsqrt(ss * (1.0 / HEAD_DIM) + EPS) * gain

    def to_slabs(z):
        for c in range(N_SLABS):
            slab_ref[c] = z[:, c * LANES:(c + 1) * LANES]

    def rows_mod16(c, r):
        return quad_ref[c, r % 4, pl.ds(r // 4, tm // 16, stride=4), :]

    def split4():
        for c in range(N_SLABS):
            for rho in range(4):
                quad_ref[c, rho] = slab_ref[c, pl.ds(rho, tm // 4, stride=4), :]

    def emit_dilated(z, out_ref, d):
        if d == 1:
            out_ref[0] = z.astype(BF16)
            return
        to_slabs(z)
        if d == 16:
            split4()
        for r in range(d):
            for c in range(N_SLABS):
                rows = (rows_mod16(c, r) if d == 16
                        else slab_ref[c, pl.ds(r, tm // d, stride=d), :])
                out_ref[r, :, c * LANES:(c + 1) * LANES] = rows.astype(BF16)

    for g, d in enumerate(DILATIONS):
        emit_dilated(head_norm(proj(g), gq_ref[...]), qkv_refs[g], d)
        emit_dilated(head_norm(proj(N_GROUPS + g), gk_ref[...]), qkv_refs[N_GROUPS + g], d)
        emit_dilated(proj(2 * N_GROUPS + g), qkv_refs[2 * N_GROUPS + g], d)

    to_slabs(proj(3 * N_GROUPS))
    split4()
    for a in range(N_SLABS):
        for half in range(CHUNK // PIECES):
            vs = [rows_mod16(a, half * PIECES + j) for j in range(PIECES)]
            for i, w in enumerate(_piece_transpose(vs)):
                xs_ref[a * PIECES + i, :, half * LANES:(half + 1) * LANES] = w.astype(BF16)

    for j in range(2 * D_MODEL // COL_TILE):
        cols = slice(j * COL_TILE, (j + 1) * COL_TILE)
        gate_ref[:, cols] = jax.nn.sigmoid(proj(3 * N_GROUPS + 1 + j)).astype(BF16)


def _in_proj(x, g_mix, w_in, gq, gk, bd, batch, seq):
    n = x.shape[0]
    tm = TOKEN_TILE
    tiles = seq // tm
    row = lambda i: (i, 0)
    dil_specs, dil_shapes = [], []
    for _ in range(3):
        for d in DILATIONS:
            dil_specs.append(pl.BlockSpec((None, d, tm // d, ATTN_WIDTH),
                                          lambda i: (i // tiles, 0, i % tiles, 0)))
            dil_shapes.append(jax.ShapeDtypeStruct((batch, d, seq // d, ATTN_WIDTH), BF16))
    outs = pl.pallas_call(
        _in_proj_kernel,
        grid=(n // tm,),
        in_specs=[pl.BlockSpec((tm, D_MODEL), row),
                  _resident((1, D_MODEL)),
                  _resident((D_MODEL, IN_COLS)),
                  _resident((1, COL_TILE)),
                  _resident((1, COL_TILE)),
                  _resident((COL_TILE // 2, COL_TILE // 2))],
        out_specs=dil_specs + [
            pl.BlockSpec((SSM_GROUPS, tm // CHUNK, CHUNK_COLS), lambda i: (0, i, 0)),
            pl.BlockSpec((tm, 2 * D_MODEL), row)],
        out_shape=dil_shapes + [
            jax.ShapeDtypeStruct((SSM_GROUPS, n // CHUNK, CHUNK_COLS), BF16),
            jax.ShapeDtypeStruct((n, 2 * D_MODEL), BF16)],
        scratch_shapes=[pltpu.VMEM((tm, D_MODEL), BF16),
                        pltpu.VMEM((N_SLABS, tm, LANES), F32),
                        pltpu.VMEM((N_SLABS, 4, tm // 4, LANES), F32)],
        compiler_params=_params("arbitrary"),
        name="in_proj",
    )(x, g_mix, w_in, gq, gk, bd)
    return outs[0:3], outs[3:6], outs[6:9], outs[9], outs[10]


ATTN_BLOCKS = {1: 8, 4: 8, 16: 1}
ATTN_RESIDUES = {1: 1, 4: 1, 16: 8}


def _attn_kernel(q_ref, k_ref, v_ref, *refs, dilation, nb, nr, merge):
    if merge:
        other = (refs[0], refs[1]), (refs[2], refs[3])
        a_ref, kbuf, vbuf = refs[4:]
    else:
        o_ref, lse_ref, kbuf, vbuf = refs
    step = pl.program_id(1)
    last = slice((nb - 1) * BLK, nb * BLK)

    qi = lax.broadcasted_iota(jnp.int32, (BLK, 2 * BLK), 0)
    kj = lax.broadcasted_iota(jnp.int32, (BLK, 2 * BLK), 1)
    neg = jnp.full((BLK, 2 * BLK), -jnp.inf, F32)
    zero = jnp.zeros((BLK, 2 * BLK), F32)
    cur_bias = jnp.where(kj - BLK <= qi, zero, neg)
    band = jnp.where(kj < BLK, jnp.where(kj >= qi, zero, neg), cur_bias)
    head = jnp.where(kj < BLK, jnp.where(step > 0, band, neg), cur_bias)
    band = jnp.concatenate([band, band], axis=0)
    head = jnp.concatenate([head, head], axis=0)

    lane_q = lax.broadcasted_iota(jnp.int32, (BLK, LANES), 1) < HEAD_DIM
    lane_kv = lax.broadcasted_iota(jnp.int32, (2 * BLK, LANES), 1) < HEAD_DIM

    for rr in range(nr):
        r = pl.program_id(2) * nr + rr
        kb = kbuf.at[r]
        vb = vbuf.at[r]

        @pl.when(step == 0)
        def _():
            kb[0:BLK, :] = jnp.zeros((BLK, ATTN_WIDTH), BF16)
            vb[0:BLK, :] = jnp.zeros((BLK, ATTN_WIDTH), BF16)

        kb[BLK:2 * BLK, :] = k_ref[rr, 0:BLK, :]
        vb[BLK:2 * BLK, :] = v_ref[rr, 0:BLK, :]

        for j in range(nb):
            for p in range(HEADS // 2):
                cols = slice(p * LANES, (p + 1) * LANES)
                if j == 0:
                    keys, vals, bias = kb[:, cols], vb[:, cols], head
                else:
                    window = slice((j - 1) * BLK, (j + 1) * BLK)
                    keys, vals, bias = k_ref[rr, window, cols], v_ref[rr, window, cols], band
                qp = q_ref[rr, j * BLK:(j + 1) * BLK, cols]
                zq = jnp.zeros_like(qp)
                q2 = jnp.concatenate([jnp.where(lane_q, qp, zq), jnp.where(lane_q, zq, qp)],
                                     axis=0)
                s = lax.dot_general(q2, keys, (((1,), (1,)), ((), ())),
                                    preferred_element_type=F32) + bias
                m = jnp.max(s, axis=-1, keepdims=True)
                e = jnp.exp2(s - m)
                den = jnp.sum(e, axis=-1, keepdims=True)
                eb = e.astype(BF16)
                zv = jnp.zeros_like(vals)
                v2 = jnp.concatenate([jnp.where(lane_kv, vals, zv),
                                      jnp.where(lane_kv, zv, vals)], axis=0)
                acc = jnp.dot(jnp.concatenate([eb[:BLK], eb[BLK:]], axis=1), v2,
                              preferred_element_type=F32)
                m_lanes = jnp.where(lane_q, m[:BLK], m[BLK:])
                den_lanes = jnp.where(lane_q, den[:BLK], den[BLK:])
                lse = (m_lanes + jnp.log2(den_lanes)) * LN2
                o = acc * (1.0 / den_lanes)
                if merge:
                    rows = slice(j * BLK, (j + 1) * BLK)
                    (o_a, l_a), (o_b, l_b) = [(o_r[p, rows, :], l_r[p, rows, :])
                                              for o_r, l_r in other]
                    top = jnp.maximum(jnp.maximum(lse, l_a), l_b)
                    e0, e1, e2 = jnp.exp(lse - top), jnp.exp(l_a - top), jnp.exp(l_b - top)
                    a_ref[rows, cols] = ((e0 * o + e1 * o_a + e2 * o_b)
                                         * (1.0 / (e0 + e1 + e2))).astype(BF16)
                else:
                    rows = pl.ds(j * BLK * dilation + r, BLK, stride=dilation)
                    o_ref[p, rows, :] = o
                    lse_ref[p, rows, :] = lse

        kb[0:BLK, :] = k_ref[rr, last, :]
        vb[0:BLK, :] = v_ref[rr, last, :]


def _attention_group(q, k, v, batch, seq, group, others=None):
    d = DILATIONS[group]
    nb, nr = ATTN_BLOCKS[d], ATTN_RESIDUES[d]
    steps = seq // d // (BLK * nb)
    merge = others is not None
    in_spec = pl.BlockSpec((None, nr, BLK * nb, ATTN_WIDTH), lambda b, n, r: (b, r, n, 0))
    slab_spec = pl.BlockSpec((None, N_SLABS, BLK * nb * d, LANES), lambda b, n, r: (b, 0, n, 0))
    slab_shape = jax.ShapeDtypeStruct((batch, N_SLABS, seq, LANES), F32)
    if merge:
        assert d == 1
        extra = [t for pair in others for t in pair]
        out_specs = pl.BlockSpec((BLK * nb, ATTN_WIDTH), lambda b, n, r: (b * steps + n, 0))
        out_shape = jax.ShapeDtypeStruct((batch * seq, ATTN_WIDTH), BF16)
    else:
        extra = []
        out_specs = [slab_spec, slab_spec]
        out_shape = [slab_shape, slab_shape]
    return pl.pallas_call(
        functools.partial(_attn_kernel, dilation=d, nb=nb, nr=nr, merge=merge),
        grid=(batch, steps, d // nr),
        in_specs=[in_spec, in_spec, in_spec] + [slab_spec] * len(extra),
        out_specs=out_specs,
        out_shape=out_shape,
        scratch_shapes=[pltpu.VMEM((d, 2 * BLK, ATTN_WIDTH), BF16),
                        pltpu.VMEM((d, 2 * BLK, ATTN_WIDTH), BF16)],
        compiler_params=_params("arbitrary", "arbitrary", "arbitrary"),
        name=f"attn_d{d}",
    )(q, k, v, *extra)


def _ssm_tables(lam_re, lam_im, log_dt, b_re, b_im, c_re, c_im):
    hi = lax.Precision.HIGHEST
    depth = lam_re.shape[0]
    lr = lam_re.astype(F32)
    li = lam_im.astype(F32)
    dt = jnp.exp(log_dt.astype(F32))[..., None]
    mag = jnp.exp(lr * dt)
    ang = li * dt
    abar_re = mag * jnp.cos(ang)
    abar_im = mag * jnp.sin(ang)
    nr = abar_re - 1.0
    ni = abar_im
    den = lr * lr + li * li
    cr = ((nr * lr + ni * li) / den)[:, :, None, :]
    ci = ((ni * lr - nr * li) / den)[:, :, None, :]
    brt = b_re.astype(F32).transpose(0, 1, 3, 2)
    bit = b_im.astype(F32).transpose(0, 1, 3, 2)
    bbar_re = cr * brt - ci * bit
    bbar_im = cr * bit + ci * brt

    pw_re = [jnp.ones_like(abar_re)]
    pw_im = [jnp.zeros_like(abar_im)]
    for _ in range(CHUNK):
        pr, pi = pw_re[-1], pw_im[-1]
        pw_re.append(pr * abar_re - pi * abar_im)
        pw_im.append(pr * abar_im + pi * abar_re)
    pw_re = jnp.stack(pw_re, axis=2)
    pw_im = jnp.stack(pw_im, axis=2)

    wide = (CHUNK + 1) * SSM_GROUP
    lane = jnp.arange(wide)
    rep = (lane[None, :] // SSM_GROUP == jnp.arange(CHUNK + 1)[:, None]).astype(F32)
    til = (lane[None, :] % SSM_GROUP == jnp.arange(SSM_GROUP)[:, None]).astype(F32)
    pr_l = jnp.einsum('dgtp,tl->dgpl', pw_re, rep, precision=hi)
    pi_l = jnp.einsum('dgtp,tl->dgpl', pw_im, rep, precision=hi)
    cr_l = jnp.einsum('dgcp,cl->dgpl', c_re.astype(F32), til, precision=hi)
    ci_l = jnp.einsum('dgcp,cl->dgpl', c_im.astype(F32), til, precision=hi)
    cp_re = cr_l * pr_l - ci_l * pi_l
    cp_im = cr_l * pi_l + ci_l * pr_l

    lag = (jnp.einsum('dgcp,dgpl->dgcl', bbar_re, cp_re[..., :CHUNK_COLS], precision=hi)
           - jnp.einsum('dgcp,dgpl->dgcl', bbar_im, cp_im[..., :CHUNK_COLS], precision=hi))
    toep = jnp.stack(
        [jnp.pad(lag[..., :CHUNK_COLS - PIECE * s], ((0, 0), (0, 0), (0, 0), (PIECE * s, 0)))
         for s in range(CHUNK)], axis=2)
    toep = toep.reshape(depth, SSM_GROUPS, CHUNK_COLS, CHUNK_COLS)

    rev_re = pw_re[:, :, CHUNK - 1::-1][:, :, :, None, :]
    rev_im = pw_im[:, :, CHUNK - 1::-1][:, :, :, None, :]
    win_re = (rev_re * bbar_re[:, :, None] - rev_im * bbar_im[:, :, None])
    win_im = (rev_re * bbar_im[:, :, None] + rev_im * bbar_re[:, :, None])
    win_re = win_re.reshape(depth, SSM_GROUPS, CHUNK_COLS, SSM_STATE)
    win_im = win_im.reshape(depth, SSM_GROUPS, CHUNK_COLS, SSM_STATE)

    wout_re = cp_re[..., SSM_GROUP:]
    wout_im = -cp_im[..., SSM_GROUP:]

    odd = (jnp.arange(SSM_GROUPS) % 2 == 1)[None, :, None, None]
    zc = jnp.zeros_like(win_re)
    w_in = jnp.concatenate([jnp.where(odd, zc, win_re), jnp.where(odd, win_re, zc),
                            jnp.where(odd, zc, win_im), jnp.where(odd, win_im, zc)], axis=3)
    zr = jnp.zeros_like(wout_re)
    w_out = jnp.concatenate([jnp.where(odd, zr, wout_re), jnp.where(odd, wout_re, zr),
                             jnp.where(odd, zr, wout_im), jnp.where(odd, wout_im, zr)], axis=2)
    a_re = pw_re[:, :, CHUNK].reshape(depth, STATE_ROWS, LANES)
    a_im = pw_im[:, :, CHUNK].reshape(depth, STATE_ROWS, LANES)
    return toep.astype(BF16), w_in.astype(BF16), w_out.astype(BF16), a_re, a_im


def _ssm_kernel(x_ref, win_ref, toep_ref, wout_ref, are_ref, aim_ref, d_ref, y_ref,
                sre_ref, sim_ref, *, batch, per_batch):
    phase = pl.program_id(0)
    pair = pl.program_id(1)
    nchunk = batch * per_batch
    pair_rows = pl.ds(pair, nchunk, stride=STATE_ROWS)

    @pl.when(phase == 0)
    def _():
        v = (jnp.dot(x_ref[0], win_ref[0], preferred_element_type=F32)
             + jnp.dot(x_ref[1], win_ref[1], preferred_element_type=F32))
        sre_ref[pair_rows, :] = v[:, :LANES]
        sim_ref[pair_rows, :] = v[:, LANES:]

    @pl.when((phase == 0) & (pair == pl.num_programs(1) - 1))
    def _():
        ar = are_ref[...]
        ai = aim_ref[...]

        def body(k, carry):
            new = []
            for b in range(batch):
                sre, sim = carry[b]
                rows = pl.ds(pl.multiple_of((b * per_batch + k) * STATE_ROWS, STATE_ROWS),
                             STATE_ROWS)
                vre = sre_ref[rows, :]
                vim = sim_ref[rows, :]
                sre_ref[rows, :] = sre
                sim_ref[rows, :] = sim
                new.append((ar * sre - ai * sim + vre, ar * sim + ai * sre + vim))
            return tuple(new)

        zero = jnp.zeros((STATE_ROWS, LANES), F32)
        lax.fori_loop(0, per_batch, body, tuple((zero, zero) for _ in range(batch)))

    @pl.when(phase == 1)
    def _():
        sp = jnp.concatenate([sre_ref[pair_rows, :], sim_ref[pair_rows, :]],
                             axis=1).astype(BF16)
        for h in range(2):
            x = x_ref[h]
            y = (jnp.dot(x, toep_ref[h], preferred_element_type=F32)
                 + jnp.dot(sp, wout_ref[h], preferred_element_type=F32)
                 + d_ref[h] * x.astype(F32))
            y_ref[h] = jax.nn.gelu(y).astype(BF16)


def _ssm(x, toep, w_in, w_out, a_re, a_im, d_tiled, batch, seq):
    nchunk = x.shape[1]
    per_batch = seq // CHUNK
    pair3 = lambda shape: pl.BlockSpec((2,) + shape, lambda ph, q: (q, 0, 0))
    return pl.pallas_call(
        functools.partial(_ssm_kernel, batch=batch, per_batch=per_batch),
        grid=(2, SSM_GROUPS // 2),
        in_specs=[pair3((nchunk, CHUNK_COLS)), pair3((CHUNK_COLS, CHUNK_COLS)),
                  pair3((CHUNK_COLS, CHUNK_COLS)), pair3((CHUNK_COLS, CHUNK_COLS)),
                  _resident((STATE_ROWS, LANES)), _resident((STATE_ROWS, LANES)),
                  pair3((1, CHUNK_COLS))],
        out_specs=pl.BlockSpec((2, nchunk, CHUNK_COLS), lambda ph, q: (q * ph, 0, 0)),
        out_shape=jax.ShapeDtypeStruct((SSM_GROUPS, nchunk, CHUNK_COLS), BF16),
        scratch_shapes=[pltpu.VMEM((nchunk * STATE_ROWS, LANES), F32),
                        pltpu.VMEM((nchunk * STATE_ROWS, LANES), F32)],
        compiler_params=_params("arbitrary", "arbitrary"),
        name="ssm",
    )(x, w_in, toep, w_out, a_re, a_im, d_tiled)


FF_CHUNK = D_FF


def _mix_ffn_kernel(x_ref, a_ref, ys_ref, gate_ref, wap_ref, wa_ref, wb_ref, wout_ref,
                    g_ref, wg_ref, wu_ref, wd_ref, out_ref, slab_ref):
    tm = x_ref.shape[0]
    a_out = jnp.dot(a_ref[...], wap_ref[...], preferred_element_type=F32)

    for s in range(N_SLABS):
        for half in range(CHUNK // PIECES):
            ws = [ys_ref[s * PIECES + i, :, half * LANES:(half + 1) * LANES].astype(F32)
                  for i in range(PIECES)]
            for j, v in enumerate(_piece_transpose(ws)):
                slab_ref[s, pl.ds(half * PIECES + j, tm // CHUNK, stride=CHUNK), :] = v
    y = jnp.concatenate([slab_ref[c] for c in range(N_SLABS)], axis=1).astype(BF16)
    s_out = (jnp.dot(y, wa_ref[...], preferred_element_type=F32)
             * jax.nn.sigmoid(jnp.dot(y, wb_ref[...], preferred_element_type=F32)))
    mix = (gate_ref[:, :D_MODEL].astype(F32) * a_out
           + gate_ref[:, D_MODEL:].astype(F32) * s_out)
    x = x_ref[...] + jnp.dot(mix.astype(BF16), wout_ref[...], preferred_element_type=F32)

    h = _rms_norm(x, g_ref[...]).astype(BF16)
    for c in range(D_FF // FF_CHUNK):
        cols = slice(c * FF_CHUNK, (c + 1) * FF_CHUNK)
        gate = jnp.dot(h, wg_ref[:, cols], preferred_element_type=F32)
        up = jnp.dot(h, wu_ref[:, cols], preferred_element_type=F32)
        hidden = (jax.nn.silu(gate) * up).astype(BF16)
        x = x + jnp.dot(hidden, wd_ref[cols, :], preferred_element_type=F32)
    out_ref[...] = x


def _mix_ffn(x, a, ys, gates, wap, wa, wb, wout, g_ffn, wg, wu, wd):
    n = x.shape[0]
    tm = TOKEN_TILE
    row = lambda i: (i, 0)
    return pl.pallas_call(
        _mix_ffn_kernel,
        grid=(n // tm,),
        in_specs=[pl.BlockSpec((tm, D_MODEL), row),
                  pl.BlockSpec((tm, ATTN_WIDTH), row),
                  pl.BlockSpec((SSM_GROUPS, tm // CHUNK, CHUNK_COLS), lambda i: (0, i, 0)),
                  pl.BlockSpec((tm, 2 * D_MODEL), row),
                  _resident((ATTN_WIDTH, D_MODEL)), _resident((SSM_WIDTH, D_MODEL)),
                  _resident((SSM_WIDTH, D_MODEL)), _resident((D_MODEL, D_MODEL)),
                  _resident((1, D_MODEL)), _resident((D_MODEL, D_FF)),
                  _resident((D_MODEL, D_FF)), _resident((D_FF, D_MODEL))],
        out_specs=pl.BlockSpec((tm, D_MODEL), row),
        out_shape=jax.ShapeDtypeStruct((n, D_MODEL), F32),
        scratch_shapes=[pltpu.VMEM((N_SLABS, tm, LANES), F32)],
        compiler_params=_params("arbitrary"),
        name="mix_ffn",
    )(x, a, ys, gates, wap, wa, wb, wout, g_ffn, wg, wu, wd)


def kernel(x, g_mix, w_in, g_q, g_k, w_attn_proj, lambda_re, lambda_im, log_dt, b_re, b_im,
           c_re, c_im, d_skip, w_glu_a, w_glu_b, w_out, g_ffn, w_ffn_gate, w_ffn_up, w_ffn_down):
    batch, seq, _ = x.shape
    depth = w_in.shape[0]
    n = batch * seq
    xf = x.reshape(n, D_MODEL).astype(F32)

    blk = jnp.arange(COL_TILE // 2) // HEAD_DIM
    bd = (blk[:, None] == blk[None, :]).astype(BF16)
    gq = jnp.tile(g_q.astype(F32) * (LOG2E * HEAD_DIM ** -0.5), (1, HEADS))[:, None]
    gk = jnp.tile(g_k.astype(F32), (1, HEADS))[:, None]
    toep, s_in, s_out, a_re, a_im = _ssm_tables(lambda_re, lambda_im, log_dt, b_re, b_im,
                                                c_re, c_im)
    d_tiled = jnp.tile(d_skip.astype(F32).reshape(depth, SSM_GROUPS, 1, SSM_GROUP),
                       (1, 1, 1, CHUNK))

    for l in range(depth):
        q, k, v, xs, gates = _in_proj(xf, g_mix[l].astype(F32)[None], w_in[l].astype(BF16),
                                      gq[l], gk[l], bd, batch, seq)
        dilated = [_attention_group(q[g], k[g], v[g], batch, seq, g)
                   for g in range(1, N_GROUPS)]
        a = _attention_group(q[0], k[0], v[0], batch, seq, 0, others=dilated)
        ys = _ssm(xs, toep[l], s_in[l], s_out[l], a_re[l], a_im[l], d_tiled[l], batch, seq)
        xf = _mix_ffn(xf, a, ys, gates, w_attn_proj[l].astype(BF16), w_glu_a[l].astype(BF16),
                      w_glu_b[l].astype(BF16), w_out[l].astype(BF16),
                      g_ffn[l].astype(F32)[None], w_ffn_gate[l].astype(BF16),
                      w_ffn_up[l].astype(BF16), w_ffn_down[l].astype(BF16))
    return xf.reshape(batch, seq, D_MODEL).astype(x.dtype)
```

```python
import functools

import jax
import jax.numpy as jnp
from jax import lax
from jax.experimental import pallas as pl
from jax.experimental.pallas import tpu as pltpu

F32 = jnp.float32
BF16 = jnp.bfloat16

D_MODEL = 1024
HEAD_DIM = 64
HEADS = 8
ATTN_WIDTH = HEADS * HEAD_DIM
DILATIONS = (1, 4, 16)
N_GROUPS = len(DILATIONS)
BLK = 128
SSM_WIDTH = 512
SSM_GROUP = 16
SSM_GROUPS = 32
SSM_STATE = 64
CHUNK = 16
CHUNK_COLS = CHUNK * SSM_GROUP
D_FF = 2816
IN_COLS = 3 * N_GROUPS * ATTN_WIDTH + SSM_WIDTH + 2 * D_MODEL
EPS = 1e-6
LOG2E = 1.4426950408889634
LN2 = 0.6931471805599453

VMEM_LIMIT_BYTES = 56 * 1024 * 1024
LANES = 128
COL_TILE = 512
TOKEN_TILE = 512
PIECE = SSM_GROUP
PIECES = LANES // PIECE
N_SLABS = ATTN_WIDTH // LANES
STATE_ROWS = SSM_GROUPS // 2


def _params(*semantics):
    return pltpu.CompilerParams(dimension_semantics=semantics,
                                vmem_limit_bytes=VMEM_LIMIT_BYTES)


def _resident(shape):
    return pl.BlockSpec(shape, lambda *_: (0,) * len(shape),
                        pipeline_mode=pl.Buffered(1))


def _rms_norm(x, gain):
    ms = jnp.mean(x * x, axis=-1, keepdims=True)
    return x * lax.rsqrt(ms + EPS) * gain


def _piece_transpose(vs):
    vs = list(vs)
    piece = lax.broadcasted_iota(jnp.int32, vs[0].shape, 1) // PIECE
    for k in (4, 2, 1):
        upper = (piece & k) != 0
        for j in range(PIECES):
            if j & k:
                continue
            a, b = vs[j], vs[j + k]
            vs[j] = jnp.where(upper, pltpu.roll(b, PIECE * k, 1), a)
            vs[j + k] = jnp.where(upper, b, pltpu.roll(a, LANES - PIECE * k, 1))
    return vs


def _in_proj_kernel(x_ref, g_ref, w_ref, gq_ref, gk_ref, bd_ref, *refs):
    qkv_refs = refs[:3 * N_GROUPS]
    xs_ref, gate_ref, h_ref, slab_ref, quad_ref = refs[3 * N_GROUPS:]
    tm = x_ref.shape[0]
    h_ref[...] = _rms_norm(x_ref[...], g_ref[...]).astype(BF16)

    def proj(j):
        return jnp.dot(h_ref[...], w_ref[:, j * COL_TILE:(j + 1) * COL_TILE],
                       preferred_element_type=F32)

    def head_norm(z, gain):
        zz = (z * z).astype(BF16)
        half = COL_TILE // 2
        ss = jnp.concatenate(
            [jnp.dot(zz[:, :half], bd_ref[...], preferred_element_type=F32),
             jnp.dot(zz[:, half:], bd_ref[...], preferred_element_type=F32)], axis=1)
        return z * lax.rsqrt(ss * (1.0 / HEAD_DIM) + EPS) * gain

    def to_slabs(z):
        for c in range(N_SLABS):
            slab_ref[c] = z[:, c * LANES:(c + 1) * LANES]

    def rows_mod16(c, r):
        return quad_ref[c, r % 4, pl.ds(r // 4, tm // 16, stride=4), :]

    def split4():
        for c in range(N_SLABS):
            for rho in range(4):
                quad_ref[c, rho] = slab_ref[c, pl.ds(rho, tm // 4, stride=4), :]

    def emit_dilated(z, out_ref, d):
        if d == 1:
            out_ref[0] = z.astype(BF16)
            return
        to_slabs(z)
        if d == 16:
            split4()
        for r in range(d):
            for c in range(N_SLABS):
                rows = (rows_mod16(c, r) if d == 16
                        else slab_ref[c, pl.ds(r, tm // d, stride=d), :])
                out_ref[r, :, c * LANES:(c + 1) * LANES] = rows.astype(BF16)

    for g, d in enumerate(DILATIONS):
        emit_dilated(head_norm(proj(g), gq_ref[...]), qkv_refs[g], d)
        emit_dilated(head_norm(proj(N_GROUPS + g), gk_ref[...]), qkv_refs[N_GROUPS + g], d)
        emit_dilated(proj(2 * N_GROUPS + g), qkv_refs[2 * N_GROUPS + g], d)

    to_slabs(proj(3 * N_GROUPS))
    split4()
    for a in range(N_SLABS):
        for half in range(CHUNK // PIECES):
            vs = [rows_mod16(a, half * PIECES + j) for j in range(PIECES)]
            for i, w in enumerate(_piece_transpose(vs)):
                xs_ref[a * PIECES + i, :, half * LANES:(half + 1) * LANES] = w.astype(BF16)

    for j in range(2 * D_MODEL // COL_TILE):
        cols = slice(j * COL_TILE, (j + 1) * COL_TILE)
        gate_ref[:, cols] = jax.nn.sigmoid(proj(3 * N_GROUPS + 1 + j)).astype(BF16)


def _in_proj(x, g_mix, w_in, gq, gk, bd, batch, seq):
    n = x.shape[0]
    tm = TOKEN_TILE
    tiles = seq // tm
    row = lambda i: (i, 0)
    dil_specs, dil_shapes = [], []
    for _ in range(3):
        for d in DILATIONS:
            dil_specs.append(pl.BlockSpec((None, d, tm // d, ATTN_WIDTH),
                                          lambda i: (i // tiles, 0, i % tiles, 0)))
            dil_shapes.append(jax.ShapeDtypeStruct((batch, d, seq // d, ATTN_WIDTH), BF16))
    outs = pl.pallas_call(
        _in_proj_kernel,
        grid=(n // tm,),
        in_specs=[pl.BlockSpec((tm, D_MODEL), row),
                  _resident((1, D_MODEL)),
                  _resident((D_MODEL, IN_COLS)),
                  _resident((1, COL_TILE)),
                  _resident((1, COL_TILE)),
                  _resident((COL_TILE // 2, COL_TILE // 2))],
        out_specs=dil_specs + [
            pl.BlockSpec((SSM_GROUPS, tm // CHUNK, CHUNK_COLS), lambda i: (0, i, 0)),
            pl.BlockSpec((tm, 2 * D_MODEL), row)],
        out_shape=dil_shapes + [
            jax.ShapeDtypeStruct((SSM_GROUPS, n // CHUNK, CHUNK_COLS), BF16),
            jax.ShapeDtypeStruct((n, 2 * D_MODEL), BF16)],
        scratch_shapes=[pltpu.VMEM((tm, D_MODEL), BF16),
                        pltpu.VMEM((N_SLABS, tm, LANES), F32),
                        pltpu.VMEM((N_SLABS, 4, tm // 4, LANES), F32)],
        compiler_params=_params("arbitrary"),
        name="in_proj",
    )(x, g_mix, w_in, gq, gk, bd)
    return outs[0:3], outs[3:6], outs[6:9], outs[9], outs[10]


ATTN_BLOCKS = {1: 8, 4: 8, 16: 1}
ATTN_RESIDUES = {1: 1, 4: 1, 16: 8}


def _attn_kernel(q_ref, k_ref, v_ref, *refs, dilation, nb, nr, merge):
    if merge:
        other = (refs[0], refs[1]), (refs[2], refs[3])
        a_ref, kbuf, vbuf = refs[4:]
    else:
        o_ref, lse_ref, kbuf, vbuf = refs
    step = pl.program_id(1)
    last = slice((nb - 1) * BLK, nb * BLK)

    qi = lax.broadcasted_iota(jnp.int32, (BLK, 2 * BLK), 0)
    kj = lax.broadcasted_iota(jnp.int32, (BLK, 2 * BLK), 1)
    neg = jnp.full((BLK, 2 * BLK), -jnp.inf, F32)
    zero = jnp.zeros((BLK, 2 * BLK), F32)
    cur_bias = jnp.where(kj - BLK <= qi, zero, neg)
    band = jnp.where(kj < BLK, jnp.where(kj >= qi, zero, neg), cur_bias)
    head = jnp.where(kj < BLK, jnp.where(step > 0, band, neg), cur_bias)
    band = jnp.concatenate([band, band], axis=0)
    head = jnp.concatenate([head, head], axis=0)

    lane_q = lax.broadcasted_iota(jnp.int32, (BLK, LANES), 1) < HEAD_DIM
    lane_kv = lax.broadcasted_iota(jnp.int32, (2 * BLK, LANES), 1) < HEAD_DIM
    head_of_row = lax.broadcasted_iota(jnp.int32, (4 * BLK, LANES), 0) // (2 * BLK)
    head_of_lane = lax.broadcasted_iota(jnp.int32, (4 * BLK, LANES), 1) // HEAD_DIM
    den_cols = jnp.where(head_of_row == head_of_lane, 1.0, 0.0).astype(BF16)

    for rr in range(nr):
        r = pl.program_id(2) * nr + rr
        kb = kbuf.at[r]
        vb = vbuf.at[r]

        @pl.when(step == 0)
        def _():
            kb[0:BLK, :] = jnp.zeros((BLK, ATTN_WIDTH), BF16)
            vb[0:BLK, :] = jnp.zeros((BLK, ATTN_WIDTH), BF16)

        kb[BLK:2 * BLK, :] = k_ref[rr, 0:BLK, :]
        vb[BLK:2 * BLK, :] = v_ref[rr, 0:BLK, :]

        for j in range(nb):
            for p in range(HEADS // 2):
                cols = slice(p * LANES, (p + 1) * LANES)
                if j == 0:
                    keys, vals, bias = kb[:, cols], vb[:, cols], head
                else:
                    window = slice((j - 1) * BLK, (j + 1) * BLK)
                    keys, vals, bias = k_ref[rr, window, cols], v_ref[rr, window, cols], band
                qp = q_ref[rr, j * BLK:(j + 1) * BLK, cols]
                zq = jnp.zeros_like(qp)
                q2 = jnp.concatenate([jnp.where(lane_q, qp, zq), jnp.where(lane_q, zq, qp)],
                                     axis=0)
                s = lax.dot_general(q2, keys, (((1,), (1,)), ((), ())),
                                    preferred_element_type=F32) + bias
                m = jnp.max(s, axis=-1, keepdims=True)
                eb = jnp.exp2(s - m).astype(BF16)
                zv = jnp.zeros_like(vals)
                v2 = jnp.concatenate([jnp.where(lane_kv, vals, zv),
                                      jnp.where(lane_kv, zv, vals)], axis=0)
                acc = jnp.dot(jnp.concatenate([eb[:BLK], eb[BLK:]], axis=1),
                              jnp.concatenate([v2, den_cols], axis=1),
                              preferred_element_type=F32)
                den_lanes = acc[:, LANES:]
                m_lanes = jnp.where(lane_q, m[:BLK], m[BLK:])
                lse = (m_lanes + jnp.log2(den_lanes)) * LN2
                o = acc[:, :LANES] * (1.0 / den_lanes)
                if merge:
                    rows = slice(j * BLK, (j + 1) * BLK)
                    (o_a, l_a), (o_b, l_b) = [(o_r[p, rows, :], l_r[p, rows, :])
                                              for o_r, l_r in other]
                    top = jnp.maximum(jnp.maximum(lse, l_a), l_b)
                    e0, e1, e2 = jnp.exp(lse - top), jnp.exp(l_a - top), jnp.exp(l_b - top)
                    a_ref[rows, cols] = ((e0 * o + e1 * o_a + e2 * o_b)
                                         * (1.0 / (e0 + e1 + e2))).astype(BF16)
                else:
                    rows = pl.ds(j * BLK * dilation + r, BLK, stride=dilation)
                    o_ref[p, rows, :] = o
                    lse_ref[p, rows, :] = lse

        kb[0:BLK, :] = k_ref[rr, last, :]
        vb[0:BLK, :] = v_ref[rr, last, :]


def _attention_group(q, k, v, batch, seq, group, others=None):
    d = DILATIONS[group]
    nb, nr = ATTN_BLOCKS[d], ATTN_RESIDUES[d]
    steps = seq // d // (BLK * nb)
    merge = others is not None
    in_spec = pl.BlockSpec((None, nr, BLK * nb, ATTN_WIDTH), lambda b, n, r: (b, r, n, 0))
    slab_spec = pl.BlockSpec((None, N_SLABS, BLK * nb * d, LANES), lambda b, n, r: (b, 0, n, 0))
    slab_shape = jax.ShapeDtypeStruct((batch, N_SLABS, seq, LANES), F32)
    if merge:
        assert d == 1
        extra = [t for pair in others for t in pair]
        out_specs = pl.BlockSpec((BLK * nb, ATTN_WIDTH), lambda b, n, r: (b * steps + n, 0))
        out_shape = jax.ShapeDtypeStruct((batch * seq, ATTN_WIDTH), BF16)
    else:
        extra = []
        out_specs = [slab_spec, slab_spec]
        out_shape = [slab_shape, slab_shape]
    return pl.pallas_call(
        functools.partial(_attn_kernel, dilation=d, nb=nb, nr=nr, merge=merge),
        grid=(batch, steps, d // nr),
        in_specs=[in_spec, in_spec, in_spec] + [slab_spec] * len(extra),
        out_specs=out_specs,
        out_shape=out_shape,
        scratch_shapes=[pltpu.VMEM((d, 2 * BLK, ATTN_WIDTH), BF16),
                        pltpu.VMEM((d, 2 * BLK, ATTN_WIDTH), BF16)],
        compiler_params=_params("arbitrary", "arbitrary", "arbitrary"),
        name=f"attn_d{d}",
    )(q, k, v, *extra)


def _ssm_tables(lam_re, lam_im, log_dt, b_re, b_im, c_re, c_im):
    hi = lax.Precision.HIGHEST
    depth = lam_re.shape[0]
    lr = lam_re.astype(F32)
    li = lam_im.astype(F32)
    dt = jnp.exp(log_dt.astype(F32))[..., None]
    mag = jnp.exp(lr * dt)
    ang = li * dt
    abar_re = mag * jnp.cos(ang)
    abar_im = mag * jnp.sin(ang)
    nr = abar_re - 1.0
    ni = abar_im
    den = lr * lr + li * li
    cr = ((nr * lr + ni * li) / den)[:, :, None, :]
    ci = ((ni * lr - nr * li) / den)[:, :, None, :]
    brt = b_re.astype(F32).transpose(0, 1, 3, 2)
    bit = b_im.astype(F32).transpose(0, 1, 3, 2)
    bbar_re = cr * brt - ci * bit
    bbar_im = cr * bit + ci * brt

    def powers(tau):
        tau = tau.astype(F32)[None, None, :, None]
        pmag = jnp.exp((lr * dt)[:, :, None, :] * tau)
        pang = ang[:, :, None, :] * tau
        return pmag * jnp.cos(pang), pmag * jnp.sin(pang)

    pw_re, pw_im = powers(jnp.arange(CHUNK + 1))

    wide = (CHUNK + 1) * SSM_GROUP
    lane = jnp.arange(wide)
    rep = (lane[None, :] // SSM_GROUP == jnp.arange(CHUNK + 1)[:, None]).astype(F32)
    til = (lane[None, :] % SSM_GROUP == jnp.arange(SSM_GROUP)[:, None]).astype(F32)
    pr_l = jnp.einsum('dgtp,tl->dgpl', pw_re, rep, precision=hi)
    pi_l = jnp.einsum('dgtp,tl->dgpl', pw_im, rep, precision=hi)
    cr_l = jnp.einsum('dgcp,cl->dgpl', c_re.astype(F32), til, precision=hi)
    ci_l = jnp.einsum('dgcp,cl->dgpl', c_im.astype(F32), til, precision=hi)
    cp_re = cr_l * pr_l - ci_l * pi_l
    cp_im = cr_l * pi_l + ci_l * pr_l

    lag = (jnp.einsum('dgcp,dgpl->dgcl', bbar_re, cp_re[..., :CHUNK_COLS], precision=hi)
           - jnp.einsum('dgcp,dgpl->dgcl', bbar_im, cp_im[..., :CHUNK_COLS], precision=hi))
    col = jnp.arange(CHUNK_COLS)
    shift = (col[None, None, :] == col[None, :, None]
             + PIECE * jnp.arange(CHUNK)[:, None, None]).astype(BF16)
    toep = jnp.einsum('dgcl,slm->dgscm', lag.astype(BF16), shift)
    toep = toep.reshape(depth, SSM_GROUPS, CHUNK_COLS, CHUNK_COLS)

    rev_re, rev_im = powers(CHUNK - 1 - jnp.arange(CHUNK))
    rev_re = rev_re[:, :, :, None, :]
    rev_im = rev_im[:, :, :, None, :]
    win_re = (rev_re * bbar_re[:, :, None] - rev_im * bbar_im[:, :, None])
    win_im = (rev_re * bbar_im[:, :, None] + rev_im * bbar_re[:, :, None])
    win_re = win_re.reshape(depth, SSM_GROUPS, CHUNK_COLS, SSM_STATE)
    win_im = win_im.reshape(depth, SSM_GROUPS, CHUNK_COLS, SSM_STATE)

    wout_re = cp_re[..., SSM_GROUP:]
    wout_im = -cp_im[..., SSM_GROUP:]

    odd = (jnp.arange(SSM_GROUPS) % 2 == 1)[None, :, None, None]
    zc = jnp.zeros_like(win_re)
    w_in = jnp.concatenate([jnp.where(odd, zc, win_re), jnp.where(odd, win_re, zc),
                            jnp.where(odd, zc, win_im), jnp.where(odd, win_im, zc)], axis=3)
    zr = jnp.zeros_like(wout_re)
    w_out = jnp.concatenate([jnp.where(odd, zr, wout_re), jnp.where(odd, wout_re, zr),
                             jnp.where(odd, zr, wout_im), jnp.where(odd, wout_im, zr)], axis=2)
    a_re, a_im = abar_re, abar_im
    for _ in range(CHUNK.bit_length() - 1):
        a_re, a_im = a_re * a_re - a_im * a_im, 2.0 * a_re * a_im
    a_re = a_re.reshape(depth, STATE_ROWS, LANES)
    a_im = a_im.reshape(depth, STATE_ROWS, LANES)
    return toep, w_in.astype(BF16), w_out.astype(BF16), a_re, a_im


def _ssm_kernel(x_ref, win_ref, toep_ref, wout_ref, are_ref, aim_ref, d_ref, y_ref,
                sre_ref, sim_ref, *, batch, per_batch):
    phase = pl.program_id(0)
    pair = pl.program_id(1)
    nchunk = batch * per_batch
    pair_rows = pl.ds(pair, nchunk, stride=STATE_ROWS)

    @pl.when(phase == 0)
    def _():
        v = (jnp.dot(x_ref[0], win_ref[0], preferred_element_type=F32)
             + jnp.dot(x_ref[1], win_ref[1], preferred_element_type=F32))
        sre_ref[pair_rows, :] = v[:, :LANES]
        sim_ref[pair_rows, :] = v[:, LANES:]

    @pl.when((phase == 0) & (pair == pl.num_programs(1) - 1))
    def _():
        ar = are_ref[...]
        ai = aim_ref[...]

        def body(k, carry):
            new = []
            for b in range(batch):
                sre, sim = carry[b]
                rows = pl.ds(pl.multiple_of((b * per_batch + k) * STATE_ROWS, STATE_ROWS),
                             STATE_ROWS)
                vre = sre_ref[rows, :]
                vim = sim_ref[rows, :]
                sre_ref[rows, :] = sre
                sim_ref[rows, :] = sim
                new.append((ar * sre - ai * sim + vre, ar * sim + ai * sre + vim))
            return tuple(new)

        zero = jnp.zeros((STATE_ROWS, LANES), F32)
        lax.fori_loop(0, per_batch, body, tuple((zero, zero) for _ in range(batch)))

    @pl.when(phase == 1)
    def _():
        sp = jnp.concatenate([sre_ref[pair_rows, :], sim_ref[pair_rows, :]],
                             axis=1).astype(BF16)
        for h in range(2):
            x = x_ref[h]
            y = (jnp.dot(x, toep_ref[h], preferred_element_type=F32)
                 + jnp.dot(sp, wout_ref[h], preferred_element_type=F32)
                 + d_ref[h] * x.astype(F32))
            y_ref[h] = jax.nn.gelu(y).astype(BF16)


def _ssm(x, toep, w_in, w_out, a_re, a_im, d_tiled, batch, seq):
    nchunk = x.shape[1]
    per_batch = seq // CHUNK
    pair3 = lambda shape: pl.BlockSpec((2,) + shape, lambda ph, q: (q, 0, 0))
    return pl.pallas_call(
        functools.partial(_ssm_kernel, batch=batch, per_batch=per_batch),
        grid=(2, SSM_GROUPS // 2),
        in_specs=[pair3((nchunk, CHUNK_COLS)), pair3((CHUNK_COLS, CHUNK_COLS)),
                  pair3((CHUNK_COLS, CHUNK_COLS)), pair3((CHUNK_COLS, CHUNK_COLS)),
                  _resident((STATE_ROWS, LANES)), _resident((STATE_ROWS, LANES)),
                  pair3((1, CHUNK_COLS))],
        out_specs=pl.BlockSpec((2, nchunk, CHUNK_COLS), lambda ph, q: (q * ph, 0, 0)),
        out_shape=jax.ShapeDtypeStruct((SSM_GROUPS, nchunk, CHUNK_COLS), BF16),
        scratch_shapes=[pltpu.VMEM((nchunk * STATE_ROWS, LANES), F32),
                        pltpu.VMEM((nchunk * STATE_ROWS, LANES), F32)],
        compiler_params=_params("arbitrary", "arbitrary"),
        name="ssm",
    )(x, w_in, toep, w_out, a_re, a_im, d_tiled)


FF_CHUNK = D_FF


def _mix_ffn_kernel(x_ref, a_ref, ys_ref, gate_ref, wap_ref, wa_ref, wb_ref, wout_ref,
                    g_ref, wg_ref, wu_ref, wd_ref, out_ref, slab_ref):
    tm = x_ref.shape[0]
    a_out = jnp.dot(a_ref[...], wap_ref[...], preferred_element_type=F32)

    for s in range(N_SLABS):
        for half in range(CHUNK // PIECES):
            ws = [ys_ref[s * PIECES + i, :, half * LANES:(half + 1) * LANES].astype(F32)
                  for i in range(PIECES)]
            for j, v in enumerate(_piece_transpose(ws)):
                slab_ref[s, pl.ds(half * PIECES + j, tm // CHUNK, stride=CHUNK), :] = v
    y = jnp.concatenate([slab_ref[c] for c in range(N_SLABS)], axis=1).astype(BF16)
    s_out = (jnp.dot(y, wa_ref[...], preferred_element_type=F32)
             * jax.nn.sigmoid(jnp.dot(y, wb_ref[...], preferred_element_type=F32)))
    mix = (gate_ref[:, :D_MODEL].astype(F32) * a_out
           + gate_ref[:, D_MODEL:].astype(F32) * s_out)
    x = x_ref[...] + jnp.dot(mix.astype(BF16), wout_ref[...], preferred_element_type=F32)

    h = _rms_norm(x, g_ref[...]).astype(BF16)
    for c in range(D_FF // FF_CHUNK):
        cols = slice(c * FF_CHUNK, (c + 1) * FF_CHUNK)
        gate = jnp.dot(h, wg_ref[:, cols], preferred_element_type=F32)
        up = jnp.dot(h, wu_ref[:, cols], preferred_element_type=F32)
        hidden = (jax.nn.silu(gate) * up).astype(BF16)
        x = x + jnp.dot(hidden, wd_ref[cols, :], preferred_element_type=F32)
    out_ref[...] = x


def _mix_ffn(x, a, ys, gates, wap, wa, wb, wout, g_ffn, wg, wu, wd):
    n = x.shape[0]
    tm = TOKEN_TILE
    row = lambda i: (i, 0)
    return pl.pallas_call(
        _mix_ffn_kernel,
        grid=(n // tm,),
        in_specs=[pl.BlockSpec((tm, D_MODEL), row),
                  pl.BlockSpec((tm, ATTN_WIDTH), row),
                  pl.BlockSpec((SSM_GROUPS, tm // CHUNK, CHUNK_COLS), lambda i: (0, i, 0)),
                  pl.BlockSpec((tm, 2 * D_MODEL), row),
                  _resident((ATTN_WIDTH, D_MODEL)), _resident((SSM_WIDTH, D_MODEL)),
                  _resident((SSM_WIDTH, D_MODEL)), _resident((D_MODEL, D_MODEL)),
                  _resident((1, D_MODEL)), _resident((D_MODEL, D_FF)),
                  _resident((D_MODEL, D_FF)), _resident((D_FF, D_MODEL))],
        out_specs=pl.BlockSpec((tm, D_MODEL), row),
        out_shape=jax.ShapeDtypeStruct((n, D_MODEL), F32),
        scratch_shapes=[pltpu.VMEM((N_SLABS, tm, LANES), F32)],
        compiler_params=_params("arbitrary"),
        name="mix_ffn",
    )(x, a, ys, gates, wap, wa, wb, wout, g_ffn, wg, wu, wd)


def kernel(x, g_mix, w_in, g_q, g_k, w_attn_proj, lambda_re, lambda_im, log_dt, b_re, b_im,
           c_re, c_im, d_skip, w_glu_a, w_glu_b, w_out, g_ffn, w_ffn_gate, w_ffn_up, w_ffn_down):
    batch, seq, _ = x.shape
    depth = w_in.shape[0]
    n = batch * seq
    xf = x.reshape(n, D_MODEL).astype(F32)

    blk = jnp.arange(COL_TILE // 2) // HEAD_DIM
    bd = (blk[:, None] == blk[None, :]).astype(BF16)
    gq = jnp.tile(g_q.astype(F32) * (LOG2E * HEAD_DIM ** -0.5), (1, HEADS))[:, None]
    gk = jnp.tile(g_k.astype(F32), (1, HEADS))[:, None]
    toep, s_in, s_out, a_re, a_im = _ssm_tables(lambda_re, lambda_im, log_dt, b_re, b_im,
                                                c_re, c_im)
    d_tiled = jnp.tile(d_skip.astype(F32).reshape(depth, SSM_GROUPS, 1, SSM_GROUP),
                       (1, 1, 1, CHUNK))

    for l in range(depth):
        q, k, v, xs, gates = _in_proj(xf, g_mix[l].astype(F32)[None], w_in[l].astype(BF16),
                                      gq[l], gk[l], bd, batch, seq)
        dilated = [_attention_group(q[g], k[g], v[g], batch, seq, g)
                   for g in range(1, N_GROUPS)]
        a = _attention_group(q[0], k[0], v[0], batch, seq, 0, others=dilated)
        ys = _ssm(xs, toep[l], s_in[l], s_out[l], a_re[l], a_im[l], d_tiled[l], batch, seq)
        xf = _mix_ffn(xf, a, ys, gates, w_attn_proj[l].astype(BF16), w_glu_a[l].astype(BF16),
                      w_glu_b[l].astype(BF16), w_out[l].astype(BF16),
                      g_ffn[l].astype(F32)[None], w_ffn_gate[l].astype(BF16),
                      w_ffn_up[l].astype(BF16), w_ffn_down[l].astype(BF16))
    return xf.reshape(batch, seq, D_MODEL).astype(x.dtype)
```

```python
import functools

import jax
import jax.numpy as jnp
from jax import lax
from jax.experimental import pallas as pl
from jax.experimental.pallas import tpu as pltpu

F32 = jnp.float32
BF16 = jnp.bfloat16

D_MODEL = 1024
HEAD_DIM = 64
HEADS = 8
ATTN_WIDTH = HEADS * HEAD_DIM
DILATIONS = (1, 4, 16)
N_GROUPS = len(DILATIONS)
BLK = 128
SSM_WIDTH = 512
SSM_GROUP = 16
SSM_GROUPS = 32
SSM_STATE = 64
CHUNK = 16
CHUNK_COLS = CHUNK * SSM_GROUP
D_FF = 2816
IN_COLS = 3 * N_GROUPS * ATTN_WIDTH + SSM_WIDTH + 2 * D_MODEL
EPS = 1e-6
LOG2E = 1.4426950408889634
LN2 = 0.6931471805599453

VMEM_LIMIT_BYTES = 56 * 1024 * 1024
LANES = 128
COL_TILE = 512
TOKEN_TILE = 512
PIECE = SSM_GROUP
PIECES = LANES // PIECE
N_SLABS = ATTN_WIDTH // LANES
STATE_ROWS = SSM_GROUPS // 2


def _params(*semantics):
    return pltpu.CompilerParams(dimension_semantics=semantics,
                                vmem_limit_bytes=VMEM_LIMIT_BYTES)


def _resident(shape):
    return pl.BlockSpec(shape, lambda *_: (0,) * len(shape),
                        pipeline_mode=pl.Buffered(1))


def _layer(shape, l):
    return pl.BlockSpec((None,) + shape, lambda *_: (l,) + (0,) * len(shape),
                        pipeline_mode=pl.Buffered(1))


def _rms_norm(x, gain):
    ms = jnp.mean(x * x, axis=-1, keepdims=True)
    return x * lax.rsqrt(ms + EPS) * gain


def _piece_transpose(vs):
    vs = list(vs)
    piece = lax.broadcasted_iota(jnp.int32, vs[0].shape, 1) // PIECE
    for k in (4, 2, 1):
        upper = (piece & k) != 0
        for j in range(PIECES):
            if j & k:
                continue
            a, b = vs[j], vs[j + k]
            vs[j] = jnp.where(upper, pltpu.roll(b, PIECE * k, 1), a)
            vs[j + k] = jnp.where(upper, b, pltpu.roll(a, LANES - PIECE * k, 1))
    return vs


def _in_proj_kernel(x_ref, g_ref, w_ref, gq_ref, gk_ref, bd_ref, *refs):
    qkv_refs = refs[:3 * N_GROUPS]
    xs_ref, gate_ref, h_ref, slab_ref, quad_ref = refs[3 * N_GROUPS:]
    tm = x_ref.shape[0]
    h_ref[...] = _rms_norm(x_ref[...], g_ref[...]).astype(BF16)

    def proj(j):
        return jnp.dot(h_ref[...], w_ref[:, j * COL_TILE:(j + 1) * COL_TILE],
                       preferred_element_type=F32)

    def head_norm(z, gain):
        zz = (z * z).astype(BF16)
        half = COL_TILE // 2
        ss = jnp.concatenate(
            [jnp.dot(zz[:, :half], bd_ref[...], preferred_element_type=F32),
             jnp.dot(zz[:, half:], bd_ref[...], preferred_element_type=F32)], axis=1)
        return z * lax.rsqrt(ss * (1.0 / HEAD_DIM) + EPS) * gain

    def to_slabs(z):
        for c in range(N_SLABS):
            slab_ref[c] = z[:, c * LANES:(c + 1) * LANES]

    def rows_mod16(c, r):
        return quad_ref[c, r % 4, pl.ds(r // 4, tm // 16, stride=4), :]

    def split4():
        for c in range(N_SLABS):
            for rho in range(4):
                quad_ref[c, rho] = slab_ref[c, pl.ds(rho, tm // 4, stride=4), :]

    def emit_dilated(z, out_ref, d):
        if d == 1:
            out_ref[0] = z.astype(BF16)
            return
        to_slabs(z)
        if d == 16:
            split4()
        for r in range(d):
            for c in range(N_SLABS):
                rows = (rows_mod16(c, r) if d == 16
                        else slab_ref[c, pl.ds(r, tm // d, stride=d), :])
                out_ref[r, :, c * LANES:(c + 1) * LANES] = rows.astype(BF16)

    for g, d in enumerate(DILATIONS):
        emit_dilated(head_norm(proj(g), gq_ref[...]), qkv_refs[g], d)
        emit_dilated(head_norm(proj(N_GROUPS + g), gk_ref[...]), qkv_refs[N_GROUPS + g], d)
        emit_dilated(proj(2 * N_GROUPS + g), qkv_refs[2 * N_GROUPS + g], d)

    to_slabs(proj(3 * N_GROUPS))
    split4()
    for a in range(N_SLABS):
        for half in range(CHUNK // PIECES):
            vs = [rows_mod16(a, half * PIECES + j) for j in range(PIECES)]
            for i, w in enumerate(_piece_transpose(vs)):
                xs_ref[a * PIECES + i, :, half * LANES:(half + 1) * LANES] = w.astype(BF16)

    for j in range(2 * D_MODEL // COL_TILE):
        cols = slice(j * COL_TILE, (j + 1) * COL_TILE)
        gate_ref[:, cols] = jax.nn.sigmoid(proj(3 * N_GROUPS + 1 + j)).astype(BF16)


def _in_proj(x, g_mix, w_in, gq, gk, bd, layer, batch, seq):
    n = x.shape[0]
    tm = TOKEN_TILE
    tiles = seq // tm
    row = lambda i: (i, 0)
    dil_specs, dil_shapes = [], []
    for _ in range(3):
        for d in DILATIONS:
            dil_specs.append(pl.BlockSpec((None, d, tm // d, ATTN_WIDTH),
                                          lambda i: (i // tiles, 0, i % tiles, 0)))
            dil_shapes.append(jax.ShapeDtypeStruct((batch, d, seq // d, ATTN_WIDTH), BF16))
    outs = pl.pallas_call(
        _in_proj_kernel,
        grid=(n // tm,),
        in_specs=[pl.BlockSpec((tm, D_MODEL), row),
                  _layer((1, D_MODEL), layer),
                  _layer((D_MODEL, IN_COLS), layer),
                  _layer((1, COL_TILE), layer),
                  _layer((1, COL_TILE), layer),
                  _resident((COL_TILE // 2, COL_TILE // 2))],
        out_specs=dil_specs + [
            pl.BlockSpec((SSM_GROUPS, tm // CHUNK, CHUNK_COLS), lambda i: (0, i, 0)),
            pl.BlockSpec((tm, 2 * D_MODEL), row)],
        out_shape=dil_shapes + [
            jax.ShapeDtypeStruct((SSM_GROUPS, n // CHUNK, CHUNK_COLS), BF16),
            jax.ShapeDtypeStruct((n, 2 * D_MODEL), BF16)],
        scratch_shapes=[pltpu.VMEM((tm, D_MODEL), BF16),
                        pltpu.VMEM((N_SLABS, tm, LANES), F32),
                        pltpu.VMEM((N_SLABS, 4, tm // 4, LANES), F32)],
        compiler_params=_params("arbitrary"),
        name="in_proj",
    )(x, g_mix, w_in, gq, gk, bd)
    return outs[0:3], outs[3:6], outs[6:9], outs[9], outs[10]


ATTN_BLOCKS = {1: 8, 4: 8, 16: 1}
ATTN_RESIDUES = {1: 1, 4: 1, 16: 8}


def _attn_kernel(q_ref, k_ref, v_ref, *refs, dilation, nb, nr, merge):
    if merge:
        other = (refs[0], refs[1]), (refs[2], refs[3])
        a_ref, kbuf, vbuf = refs[4:]
    else:
        o_ref, lse_ref, kbuf, vbuf = refs
    step = pl.program_id(1)
    last = slice((nb - 1) * BLK, nb * BLK)

    qi = lax.broadcasted_iota(jnp.int32, (BLK, 2 * BLK), 0)
    kj = lax.broadcasted_iota(jnp.int32, (BLK, 2 * BLK), 1)
    neg = jnp.full((BLK, 2 * BLK), -jnp.inf, F32)
    zero = jnp.zeros((BLK, 2 * BLK), F32)
    cur_bias = jnp.where(kj - BLK <= qi, zero, neg)
    band = jnp.where(kj < BLK, jnp.where(kj >= qi, zero, neg), cur_bias)
    head = jnp.where(kj < BLK, jnp.where(step > 0, band, neg), cur_bias)
    band = jnp.concatenate([band, band], axis=0)
    head = jnp.concatenate([head, head], axis=0)

    lane_q = lax.broadcasted_iota(jnp.int32, (BLK, LANES), 1) < HEAD_DIM
    lane_kv = lax.broadcasted_iota(jnp.int32, (2 * BLK, LANES), 1) < HEAD_DIM
    head_of_row = lax.broadcasted_iota(jnp.int32, (4 * BLK, LANES), 0) // (2 * BLK)
    head_of_lane = lax.broadcasted_iota(jnp.int32, (4 * BLK, LANES), 1) // HEAD_DIM
    den_cols = jnp.where(head_of_row == head_of_lane, 1.0, 0.0).astype(BF16)

    for rr in range(nr):
        r = pl.program_id(2) * nr + rr
        kb = kbuf.at[r]
        vb = vbuf.at[r]

        @pl.when(step == 0)
        def _():
            kb[0:BLK, :] = jnp.zeros((BLK, ATTN_WIDTH), BF16)
            vb[0:BLK, :] = jnp.zeros((BLK, ATTN_WIDTH), BF16)

        kb[BLK:2 * BLK, :] = k_ref[rr, 0:BLK, :]
        vb[BLK:2 * BLK, :] = v_ref[rr, 0:BLK, :]

        for j in range(nb):
            for p in range(HEADS // 2):
                cols = slice(p * LANES, (p + 1) * LANES)
                if j == 0:
                    keys, vals, bias = kb[:, cols], vb[:, cols], head
                else:
                    window = slice((j - 1) * BLK, (j + 1) * BLK)
                    keys, vals, bias = k_ref[rr, window, cols], v_ref[rr, window, cols], band
                qp = q_ref[rr, j * BLK:(j + 1) * BLK, cols]
                zq = jnp.zeros_like(qp)
                q2 = jnp.concatenate([jnp.where(lane_q, qp, zq), jnp.where(lane_q, zq, qp)],
                                     axis=0)
                s = lax.dot_general(q2, keys, (((1,), (1,)), ((), ())),
                                    preferred_element_type=F32) + bias
                m = jnp.max(s, axis=-1, keepdims=True)
                eb = jnp.exp2(s - m).astype(BF16)
                zv = jnp.zeros_like(vals)
                v2 = jnp.concatenate([jnp.where(lane_kv, vals, zv),
                                      jnp.where(lane_kv, zv, vals)], axis=0)
                acc = jnp.dot(jnp.concatenate([eb[:BLK], eb[BLK:]], axis=1),
                              jnp.concatenate([v2, den_cols], axis=1),
                              preferred_element_type=F32)
                den_lanes = acc[:, LANES:]
                m_lanes = jnp.where(lane_q, m[:BLK], m[BLK:])
                lse = (m_lanes + jnp.log2(den_lanes)) * LN2
                o = acc[:, :LANES] * (1.0 / den_lanes)
                if merge:
                    rows = slice(j * BLK, (j + 1) * BLK)
                    (o_a, l_a), (o_b, l_b) = [(o_r[p, rows, :], l_r[p, rows, :])
                                              for o_r, l_r in other]
                    top = jnp.maximum(jnp.maximum(lse, l_a), l_b)
                    e0, e1, e2 = jnp.exp(lse - top), jnp.exp(l_a - top), jnp.exp(l_b - top)
                    a_ref[rows, cols] = ((e0 * o + e1 * o_a + e2 * o_b)
                                         * (1.0 / (e0 + e1 + e2))).astype(BF16)
                else:
                    rows = pl.ds(j * BLK * dilation + r, BLK, stride=dilation)
                    o_ref[p, rows, :] = o
                    lse_ref[p, rows, :] = lse

        kb[0:BLK, :] = k_ref[rr, last, :]
        vb[0:BLK, :] = v_ref[rr, last, :]


def _attention_group(q, k, v, batch, seq, group, others=None):
    d = DILATIONS[group]
    nb, nr = ATTN_BLOCKS[d], ATTN_RESIDUES[d]
    steps = seq // d // (BLK * nb)
    merge = others is not None
    in_spec = pl.BlockSpec((None, nr, BLK * nb, ATTN_WIDTH), lambda b, n, r: (b, r, n, 0))
    slab_spec = pl.BlockSpec((None, N_SLABS, BLK * nb * d, LANES), lambda b, n, r: (b, 0, n, 0))
    slab_shape = jax.ShapeDtypeStruct((batch, N_SLABS, seq, LANES), F32)
    if merge:
        assert d == 1
        extra = [t for pair in others for t in pair]
        out_specs = pl.BlockSpec((BLK * nb, ATTN_WIDTH), lambda b, n, r: (b * steps + n, 0))
        out_shape = jax.ShapeDtypeStruct((batch * seq, ATTN_WIDTH), BF16)
    else:
        extra = []
        out_specs = [slab_spec, slab_spec]
        out_shape = [slab_shape, slab_shape]
    return pl.pallas_call(
        functools.partial(_attn_kernel, dilation=d, nb=nb, nr=nr, merge=merge),
        grid=(batch, steps, d // nr),
        in_specs=[in_spec, in_spec, in_spec] + [slab_spec] * len(extra),
        out_specs=out_specs,
        out_shape=out_shape,
        scratch_shapes=[pltpu.VMEM((d, 2 * BLK, ATTN_WIDTH), BF16),
                        pltpu.VMEM((d, 2 * BLK, ATTN_WIDTH), BF16)],
        compiler_params=_params("arbitrary", "arbitrary", "arbitrary"),
        name=f"attn_d{d}",
    )(q, k, v, *extra)


def _ssm_tables(lam_re, lam_im, log_dt, b_re, b_im, c_re, c_im):
    hi = lax.Precision.HIGHEST
    depth = lam_re.shape[0]
    lr = lam_re.astype(F32)
    li = lam_im.astype(F32)
    dt = jnp.exp(log_dt.astype(F32))[..., None]
    mag = jnp.exp(lr * dt)
    ang = li * dt
    abar_re = mag * jnp.cos(ang)
    abar_im = mag * jnp.sin(ang)
    nr = abar_re - 1.0
    ni = abar_im
    den = lr * lr + li * li
    cr = ((nr * lr + ni * li) / den)[:, :, None, :]
    ci = ((ni * lr - nr * li) / den)[:, :, None, :]
    brt = b_re.astype(F32).transpose(0, 1, 3, 2)
    bit = b_im.astype(F32).transpose(0, 1, 3, 2)
    bbar_re = cr * brt - ci * bit
    bbar_im = cr * bit + ci * brt

    def powers(tau):
        tau = tau.astype(F32)[None, None, :, None]
        pmag = jnp.exp((lr * dt)[:, :, None, :] * tau)
        pang = ang[:, :, None, :] * tau
        return pmag * jnp.cos(pang), pmag * jnp.sin(pang)

    pw_re, pw_im = powers(jnp.arange(CHUNK + 1))

    wide = (CHUNK + 1) * SSM_GROUP
    lane = jnp.arange(wide)
    rep = (lane[None, :] // SSM_GROUP == jnp.arange(CHUNK + 1)[:, None]).astype(F32)
    til = (lane[None, :] % SSM_GROUP == jnp.arange(SSM_GROUP)[:, None]).astype(F32)
    pr_l = jnp.einsum('dgtp,tl->dgpl', pw_re, rep, precision=hi)
    pi_l = jnp.einsum('dgtp,tl->dgpl', pw_im, rep, precision=hi)
    cr_l = jnp.einsum('dgcp,cl->dgpl', c_re.astype(F32), til, precision=hi)
    ci_l = jnp.einsum('dgcp,cl->dgpl', c_im.astype(F32), til, precision=hi)
    cp_re = cr_l * pr_l - ci_l * pi_l
    cp_im = cr_l * pi_l + ci_l * pr_l

    lag = (jnp.einsum('dgcp,dgpl->dgcl', bbar_re, cp_re[..., :CHUNK_COLS], precision=hi)
           - jnp.einsum('dgcp,dgpl->dgcl', bbar_im, cp_im[..., :CHUNK_COLS], precision=hi))
    col = jnp.arange(CHUNK_COLS)
    shift = (col[None, None, :] == col[None, :, None]
             + PIECE * jnp.arange(CHUNK)[:, None, None]).astype(BF16)
    toep = jnp.einsum('dgcl,slm->dgscm', lag.astype(BF16), shift)
    toep = toep.reshape(depth, SSM_GROUPS, CHUNK_COLS, CHUNK_COLS)

    rev_re, rev_im = powers(CHUNK - 1 - jnp.arange(CHUNK))
    rev_re = rev_re[:, :, :, None, :]
    rev_im = rev_im[:, :, :, None, :]
    win_re = (rev_re * bbar_re[:, :, None] - rev_im * bbar_im[:, :, None])
    win_im = (rev_re * bbar_im[:, :, None] + rev_im * bbar_re[:, :, None])
    win_re = win_re.reshape(depth, SSM_GROUPS, CHUNK_COLS, SSM_STATE)
    win_im = win_im.reshape(depth, SSM_GROUPS, CHUNK_COLS, SSM_STATE)

    wout_re = cp_re[..., SSM_GROUP:]
    wout_im = -cp_im[..., SSM_GROUP:]

    odd = (jnp.arange(SSM_GROUPS) % 2 == 1)[None, :, None, None]
    zc = jnp.zeros_like(win_re)
    w_in = jnp.concatenate([jnp.where(odd, zc, win_re), jnp.where(odd, win_re, zc),
                            jnp.where(odd, zc, win_im), jnp.where(odd, win_im, zc)], axis=3)
    zr = jnp.zeros_like(wout_re)
    w_out = jnp.concatenate([jnp.where(odd, zr, wout_re), jnp.where(odd, wout_re, zr),
                             jnp.where(odd, zr, wout_im), jnp.where(odd, wout_im, zr)], axis=2)
    a_re, a_im = abar_re, abar_im
    for _ in range(CHUNK.bit_length() - 1):
        a_re, a_im = a_re * a_re - a_im * a_im, 2.0 * a_re * a_im
    a_re = a_re.reshape(depth, STATE_ROWS, LANES)
    a_im = a_im.reshape(depth, STATE_ROWS, LANES)
    return toep, w_in.astype(BF16), w_out.astype(BF16), a_re, a_im


def _ssm_kernel(x_ref, win_ref, toep_ref, wout_ref, are_ref, aim_ref, d_ref, y_ref,
                sre_ref, sim_ref, *, batch, per_batch):
    phase = pl.program_id(0)
    pair = pl.program_id(1)
    nchunk = batch * per_batch
    pair_rows = pl.ds(pair, nchunk, stride=STATE_ROWS)

    @pl.when(phase == 0)
    def _():
        v = (jnp.dot(x_ref[0], win_ref[0], preferred_element_type=F32)
             + jnp.dot(x_ref[1], win_ref[1], preferred_element_type=F32))
        sre_ref[pair_rows, :] = v[:, :LANES]
        sim_ref[pair_rows, :] = v[:, LANES:]

    @pl.when((phase == 0) & (pair == pl.num_programs(1) - 1))
    def _():
        ar = are_ref[...]
        ai = aim_ref[...]

        def body(k, carry):
            new = []
            for b in range(batch):
                sre, sim = carry[b]
                rows = pl.ds(pl.multiple_of((b * per_batch + k) * STATE_ROWS, STATE_ROWS),
                             STATE_ROWS)
                vre = sre_ref[rows, :]
                vim = sim_ref[rows, :]
                sre_ref[rows, :] = sre
                sim_ref[rows, :] = sim
                new.append((ar * sre - ai * sim + vre, ar * sim + ai * sre + vim))
            return tuple(new)

        zero = jnp.zeros((STATE_ROWS, LANES), F32)
        lax.fori_loop(0, per_batch, body, tuple((zero, zero) for _ in range(batch)))

    @pl.when(phase == 1)
    def _():
        sp = jnp.concatenate([sre_ref[pair_rows, :], sim_ref[pair_rows, :]],
                             axis=1).astype(BF16)
        for h in range(2):
            x = x_ref[h]
            y = (jnp.dot(x, toep_ref[h], preferred_element_type=F32)
                 + jnp.dot(sp, wout_ref[h], preferred_element_type=F32)
                 + d_ref[h] * x.astype(F32))
            y_ref[h] = jax.nn.gelu(y).astype(BF16)


def _ssm(x, toep, w_in, w_out, a_re, a_im, d_tiled, layer, batch, seq):
    nchunk = x.shape[1]
    per_batch = seq // CHUNK
    pair3 = lambda shape: pl.BlockSpec((2,) + shape, lambda ph, q: (q, 0, 0))
    table = lambda shape: pl.BlockSpec((None, 2) + shape, lambda ph, q: (layer, q, 0, 0))
    return pl.pallas_call(
        functools.partial(_ssm_kernel, batch=batch, per_batch=per_batch),
        grid=(2, SSM_GROUPS // 2),
        in_specs=[pair3((nchunk, CHUNK_COLS)), table((CHUNK_COLS, CHUNK_COLS)),
                  table((CHUNK_COLS, CHUNK_COLS)), table((CHUNK_COLS, CHUNK_COLS)),
                  _layer((STATE_ROWS, LANES), layer), _layer((STATE_ROWS, LANES), layer),
                  table((1, CHUNK_COLS))],
        out_specs=pl.BlockSpec((2, nchunk, CHUNK_COLS), lambda ph, q: (q * ph, 0, 0)),
        out_shape=jax.ShapeDtypeStruct((SSM_GROUPS, nchunk, CHUNK_COLS), BF16),
        scratch_shapes=[pltpu.VMEM((nchunk * STATE_ROWS, LANES), F32),
                        pltpu.VMEM((nchunk * STATE_ROWS, LANES), F32)],
        compiler_params=_params("arbitrary", "arbitrary"),
        name="ssm",
    )(x, w_in, toep, w_out, a_re, a_im, d_tiled)


FF_CHUNK = D_FF


def _mix_ffn_kernel(x_ref, a_ref, ys_ref, gate_ref, wap_ref, wa_ref, wb_ref, wout_ref,
                    g_ref, wg_ref, wu_ref, wd_ref, out_ref, slab_ref):
    tm = x_ref.shape[0]
    a_out = jnp.dot(a_ref[...], wap_ref[...], preferred_element_type=F32)

    for s in range(N_SLABS):
        for half in range(CHUNK // PIECES):
            ws = [ys_ref[s * PIECES + i, :, half * LANES:(half + 1) * LANES].astype(F32)
                  for i in range(PIECES)]
            for j, v in enumerate(_piece_transpose(ws)):
                slab_ref[s, pl.ds(half * PIECES + j, tm // CHUNK, stride=CHUNK), :] = v
    y = jnp.concatenate([slab_ref[c] for c in range(N_SLABS)], axis=1).astype(BF16)
    s_out = (jnp.dot(y, wa_ref[...], preferred_element_type=F32)
             * jax.nn.sigmoid(jnp.dot(y, wb_ref[...], preferred_element_type=F32)))
    mix = (gate_ref[:, :D_MODEL].astype(F32) * a_out
           + gate_ref[:, D_MODEL:].astype(F32) * s_out)
    x = x_ref[...] + jnp.dot(mix.astype(BF16), wout_ref[...], preferred_element_type=F32)

    h = _rms_norm(x, g_ref[...]).astype(BF16)
    for c in range(D_FF // FF_CHUNK):
        cols = slice(c * FF_CHUNK, (c + 1) * FF_CHUNK)
        gate = jnp.dot(h, wg_ref[:, cols], preferred_element_type=F32)
        up = jnp.dot(h, wu_ref[:, cols], preferred_element_type=F32)
        hidden = (jax.nn.silu(gate) * up).astype(BF16)
        x = x + jnp.dot(hidden, wd_ref[cols, :], preferred_element_type=F32)
    out_ref[...] = x


def _mix_ffn(x, a, ys, gates, wap, wa, wb, wout, g_ffn, wg, wu, wd, layer):
    n = x.shape[0]
    tm = TOKEN_TILE
    row = lambda i: (i, 0)
    return pl.pallas_call(
        _mix_ffn_kernel,
        grid=(n // tm,),
        in_specs=[pl.BlockSpec((tm, D_MODEL), row),
                  pl.BlockSpec((tm, ATTN_WIDTH), row),
                  pl.BlockSpec((SSM_GROUPS, tm // CHUNK, CHUNK_COLS), lambda i: (0, i, 0)),
                  pl.BlockSpec((tm, 2 * D_MODEL), row),
                  _layer((ATTN_WIDTH, D_MODEL), layer), _layer((SSM_WIDTH, D_MODEL), layer),
                  _layer((SSM_WIDTH, D_MODEL), layer), _layer((D_MODEL, D_MODEL), layer),
                  _layer((1, D_MODEL), layer), _layer((D_MODEL, D_FF), layer),
                  _layer((D_MODEL, D_FF), layer), _layer((D_FF, D_MODEL), layer)],
        out_specs=pl.BlockSpec((tm, D_MODEL), row),
        out_shape=jax.ShapeDtypeStruct((n, D_MODEL), F32),
        scratch_shapes=[pltpu.VMEM((N_SLABS, tm, LANES), F32)],
        compiler_params=_params("arbitrary"),
        name="mix_ffn",
    )(x, a, ys, gates, wap, wa, wb, wout, g_ffn, wg, wu, wd)


def kernel(x, g_mix, w_in, g_q, g_k, w_attn_proj, lambda_re, lambda_im, log_dt, b_re, b_im,
           c_re, c_im, d_skip, w_glu_a, w_glu_b, w_out, g_ffn, w_ffn_gate, w_ffn_up, w_ffn_down):
    batch, seq, _ = x.shape
    depth = w_in.shape[0]
    n = batch * seq
    xf = x.reshape(n, D_MODEL).astype(F32)

    blk = jnp.arange(COL_TILE // 2) // HEAD_DIM
    bd = (blk[:, None] == blk[None, :]).astype(BF16)
    gq = jnp.tile(g_q.astype(F32) * (LOG2E * HEAD_DIM ** -0.5), (1, HEADS))[:, None]
    gk = jnp.tile(g_k.astype(F32), (1, HEADS))[:, None]
    toep, s_in, s_out, a_re, a_im = _ssm_tables(lambda_re, lambda_im, log_dt, b_re, b_im,
                                                c_re, c_im)
    d_tiled = jnp.tile(d_skip.astype(F32).reshape(depth, SSM_GROUPS, 1, SSM_GROUP),
                       (1, 1, 1, CHUNK))

    row3 = lambda g: g.astype(F32)[:, None, :]
    g_mix, g_ffn = row3(g_mix), row3(g_ffn)
    (w_in, w_attn_proj, w_glu_a, w_glu_b, w_out, w_ffn_gate, w_ffn_up, w_ffn_down) = (
        w.astype(BF16) for w in (w_in, w_attn_proj, w_glu_a, w_glu_b, w_out,
                                 w_ffn_gate, w_ffn_up, w_ffn_down))

    for l in range(depth):
        q, k, v, xs, gates = _in_proj(xf, g_mix, w_in, gq, gk, bd, l, batch, seq)
        dilated = [_attention_group(q[g], k[g], v[g], batch, seq, g)
                   for g in range(1, N_GROUPS)]
        a = _attention_group(q[0], k[0], v[0], batch, seq, 0, others=dilated)
        ys = _ssm(xs, toep, s_in, s_out, a_re, a_im, d_tiled, l, batch, seq)
        xf = _mix_ffn(xf, a, ys, gates, w_attn_proj, w_glu_a, w_glu_b, w_out, g_ffn,
                      w_ffn_gate, w_ffn_up, w_ffn_down, l)
    return xf.reshape(batch, seq, D_MODEL).astype(x.dtype)
```

```python
import functools

import jax
import jax.numpy as jnp
from jax import lax
from jax.experimental import pallas as pl
from jax.experimental.pallas import tpu as pltpu

F32 = jnp.float32
BF16 = jnp.bfloat16

D_MODEL = 1024
HEAD_DIM = 64
HEADS = 8
ATTN_WIDTH = HEADS * HEAD_DIM
DILATIONS = (1, 4, 16)
N_GROUPS = len(DILATIONS)
BLK = 128
SSM_WIDTH = 512
SSM_GROUP = 16
SSM_GROUPS = 32
SSM_STATE = 64
CHUNK = 16
CHUNK_COLS = CHUNK * SSM_GROUP
D_FF = 2816
IN_COLS = 3 * N_GROUPS * ATTN_WIDTH + SSM_WIDTH + 2 * D_MODEL
EPS = 1e-6
LOG2E = 1.4426950408889634
LN2 = 0.6931471805599453

VMEM_LIMIT_BYTES = 56 * 1024 * 1024
LANES = 128
COL_TILE = 512
TOKEN_TILE = 512
PIECE = SSM_GROUP
PIECES = LANES // PIECE
N_SLABS = ATTN_WIDTH // LANES
STATE_ROWS = SSM_GROUPS // 2


def _params(*semantics):
    return pltpu.CompilerParams(dimension_semantics=semantics,
                                vmem_limit_bytes=VMEM_LIMIT_BYTES)


def _resident(shape):
    return pl.BlockSpec(shape, lambda *_: (0,) * len(shape),
                        pipeline_mode=pl.Buffered(1))


def _layer(shape, l):
    return pl.BlockSpec((None,) + shape, lambda *_: (l,) + (0,) * len(shape),
                        pipeline_mode=pl.Buffered(1))


def _rms_norm(x, gain):
    ms = jnp.mean(x * x, axis=-1, keepdims=True)
    return x * lax.rsqrt(ms + EPS) * gain


def _piece_transpose(vs):
    vs = list(vs)
    piece = lax.broadcasted_iota(jnp.int32, vs[0].shape, 1) // PIECE
    for k in (4, 2, 1):
        upper = (piece & k) != 0
        for j in range(PIECES):
            if j & k:
                continue
            a, b = vs[j], vs[j + k]
            vs[j] = jnp.where(upper, pltpu.roll(b, PIECE * k, 1), a)
            vs[j + k] = jnp.where(upper, b, pltpu.roll(a, LANES - PIECE * k, 1))
    return vs


def _in_proj_kernel(x_ref, g_ref, w_ref, gq_ref, gk_ref, bd_ref, *refs):
    qkv_refs = refs[:3 * N_GROUPS]
    xs_ref, gate_ref, h_ref, slab_ref, quad_ref = refs[3 * N_GROUPS:]
    tm = x_ref.shape[0]
    h_ref[...] = _rms_norm(x_ref[...], g_ref[...]).astype(BF16)

    def proj(j):
        return jnp.dot(h_ref[...], w_ref[:, j * COL_TILE:(j + 1) * COL_TILE],
                       preferred_element_type=F32)

    def head_norm(z, gain):
        zz = (z * z).astype(BF16)
        half = COL_TILE // 2
        ss = jnp.concatenate(
            [jnp.dot(zz[:, :half], bd_ref[...], preferred_element_type=F32),
             jnp.dot(zz[:, half:], bd_ref[...], preferred_element_type=F32)], axis=1)
        return z * lax.rsqrt(ss * (1.0 / HEAD_DIM) + EPS) * gain

    def to_slabs(z):
        for c in range(N_SLABS):
            slab_ref[c] = z[:, c * LANES:(c + 1) * LANES]

    def rows_mod16(c, r):
        return quad_ref[c, r % 4, pl.ds(r // 4, tm // 16, stride=4), :]

    def split4():
        for c in range(N_SLABS):
            for rho in range(4):
                quad_ref[c, rho] = slab_ref[c, pl.ds(rho, tm // 4, stride=4), :]

    def emit_dilated(z, out_ref, d):
        if d == 1:
            out_ref[0] = z.astype(BF16)
            return
        to_slabs(z)
        if d == 16:
            split4()
        for r in range(d):
            for c in range(N_SLABS):
                rows = (rows_mod16(c, r) if d == 16
                        else slab_ref[c, pl.ds(r, tm // d, stride=d), :])
                out_ref[r, :, c * LANES:(c + 1) * LANES] = rows.astype(BF16)

    for g, d in enumerate(DILATIONS):
        emit_dilated(head_norm(proj(g), gq_ref[...]), qkv_refs[g], d)
        emit_dilated(head_norm(proj(N_GROUPS + g), gk_ref[...]), qkv_refs[N_GROUPS + g], d)
        emit_dilated(proj(2 * N_GROUPS + g), qkv_refs[2 * N_GROUPS + g], d)

    to_slabs(proj(3 * N_GROUPS))
    split4()
    for a in range(N_SLABS):
        for half in range(CHUNK // PIECES):
            vs = [rows_mod16(a, half * PIECES + j) for j in range(PIECES)]
            for i, w in enumerate(_piece_transpose(vs)):
                xs_ref[a * PIECES + i, :, half * LANES:(half + 1) * LANES] = w.astype(BF16)

    for j in range(2 * D_MODEL // COL_TILE):
        cols = slice(j * COL_TILE, (j + 1) * COL_TILE)
        gate_ref[:, cols] = jax.nn.sigmoid(proj(3 * N_GROUPS + 1 + j)).astype(BF16)


def _in_proj(x, g_mix, w_in, gq, gk, bd, layer, batch, seq):
    n = x.shape[0]
    tm = TOKEN_TILE
    tiles = seq // tm
    row = lambda i: (i, 0)
    dil_specs, dil_shapes = [], []
    for _ in range(3):
        for d in DILATIONS:
            dil_specs.append(pl.BlockSpec((None, d, tm // d, ATTN_WIDTH),
                                          lambda i: (i // tiles, 0, i % tiles, 0)))
            dil_shapes.append(jax.ShapeDtypeStruct((batch, d, seq // d, ATTN_WIDTH), BF16))
    outs = pl.pallas_call(
        _in_proj_kernel,
        grid=(n // tm,),
        in_specs=[pl.BlockSpec((tm, D_MODEL), row),
                  _layer((1, D_MODEL), layer),
                  _layer((D_MODEL, IN_COLS), layer),
                  _layer((1, COL_TILE), layer),
                  _layer((1, COL_TILE), layer),
                  _resident((COL_TILE // 2, COL_TILE // 2))],
        out_specs=dil_specs + [
            pl.BlockSpec((SSM_GROUPS, tm // CHUNK, CHUNK_COLS), lambda i: (0, i, 0)),
            pl.BlockSpec((tm, 2 * D_MODEL), row)],
        out_shape=dil_shapes + [
            jax.ShapeDtypeStruct((SSM_GROUPS, n // CHUNK, CHUNK_COLS), BF16),
            jax.ShapeDtypeStruct((n, 2 * D_MODEL), BF16)],
        scratch_shapes=[pltpu.VMEM((tm, D_MODEL), BF16),
                        pltpu.VMEM((N_SLABS, tm, LANES), F32),
                        pltpu.VMEM((N_SLABS, 4, tm // 4, LANES), F32)],
        compiler_params=_params("arbitrary"),
        name="in_proj",
    )(x, g_mix, w_in, gq, gk, bd)
    return outs[0:3], outs[3:6], outs[6:9], outs[9], outs[10]


ATTN_BLOCKS = {1: 8, 4: 8, 16: 1}
ATTN_RESIDUES = {1: 1, 4: 1, 16: 16}
STAGE = 4


def _attn_kernel(q_ref, k_ref, v_ref, *refs, dilation, nb, nr, merge):
    if merge:
        other = (refs[0], refs[1]), (refs[2], refs[3])
        a_ref, kbuf, vbuf = refs[4:]
        stage_refs = ()
    else:
        o_ref, lse_ref, kbuf, vbuf = refs[:4]
        stage_refs = refs[4:]
    step = pl.program_id(1)
    last = slice((nb - 1) * BLK, nb * BLK)

    qi = lax.broadcasted_iota(jnp.int32, (BLK, 2 * BLK), 0)
    kj = lax.broadcasted_iota(jnp.int32, (BLK, 2 * BLK), 1)
    neg = jnp.full((BLK, 2 * BLK), -jnp.inf, F32)
    zero = jnp.zeros((BLK, 2 * BLK), F32)
    cur_bias = jnp.where(kj - BLK <= qi, zero, neg)
    band = jnp.where(kj < BLK, jnp.where(kj >= qi, zero, neg), cur_bias)
    head = jnp.where(kj < BLK, jnp.where(step > 0, band, neg), cur_bias)
    band = jnp.concatenate([band, band], axis=0)
    head = jnp.concatenate([head, head], axis=0)

    lane_q = lax.broadcasted_iota(jnp.int32, (BLK, LANES), 1) < HEAD_DIM
    lane_kv = lax.broadcasted_iota(jnp.int32, (2 * BLK, LANES), 1) < HEAD_DIM
    head_of_row = lax.broadcasted_iota(jnp.int32, (4 * BLK, LANES), 0) // (2 * BLK)
    head_of_lane = lax.broadcasted_iota(jnp.int32, (4 * BLK, LANES), 1) // HEAD_DIM
    den_cols = jnp.where(head_of_row == head_of_lane, 1.0, 0.0).astype(BF16)

    residues = [rr if nr == dilation else pl.program_id(2) * nr + rr for rr in range(nr)]

    @pl.when(step == 0)
    def _():
        for r in residues:
            kbuf[r, 0:BLK, :] = jnp.zeros((BLK, ATTN_WIDTH), BF16)
            vbuf[r, 0:BLK, :] = jnp.zeros((BLK, ATTN_WIDTH), BF16)

    for rr, r in enumerate(residues):
        kb = kbuf.at[r]
        vb = vbuf.at[r]
        kb[BLK:2 * BLK, :] = k_ref[rr, 0:BLK, :]
        vb[BLK:2 * BLK, :] = v_ref[rr, 0:BLK, :]

        for j in range(nb):
            for p in range(HEADS // 2):
                cols = slice(p * LANES, (p + 1) * LANES)
                if j == 0:
                    keys, vals, bias = kb[:, cols], vb[:, cols], head
                else:
                    window = slice((j - 1) * BLK, (j + 1) * BLK)
                    keys, vals, bias = k_ref[rr, window, cols], v_ref[rr, window, cols], band
                qp = q_ref[rr, j * BLK:(j + 1) * BLK, cols]
                zq = jnp.zeros_like(qp)
                q2 = jnp.concatenate([jnp.where(lane_q, qp, zq), jnp.where(lane_q, zq, qp)],
                                     axis=0)
                s = lax.dot_general(q2, keys, (((1,), (1,)), ((), ())),
                                    preferred_element_type=F32) + bias
                m = jnp.max(s, axis=-1, keepdims=True)
                eb = jnp.exp2(s - m).astype(BF16)
                zv = jnp.zeros_like(vals)
                v2 = jnp.concatenate([jnp.where(lane_kv, vals, zv),
                                      jnp.where(lane_kv, zv, vals)], axis=0)
                acc = jnp.dot(jnp.concatenate([eb[:BLK], eb[BLK:]], axis=1),
                              jnp.concatenate([v2, den_cols], axis=1),
                              preferred_element_type=F32)
                den_lanes = acc[:, LANES:]
                m_lanes = jnp.where(lane_q, m[:BLK], m[BLK:])
                lse = (m_lanes + jnp.log2(den_lanes)) * LN2
                o = acc[:, :LANES] * (1.0 / den_lanes)
                if merge:
                    rows = slice(j * BLK, (j + 1) * BLK)
                    (o_a, l_a), (o_b, l_b) = [(o_r[p, rows, :], l_r[p, rows, :])
                                              for o_r, l_r in other]
                    top = jnp.maximum(jnp.maximum(lse, l_a), l_b)
                    e0, e1, e2 = jnp.exp(lse - top), jnp.exp(l_a - top), jnp.exp(l_b - top)
                    a_ref[rows, cols] = ((e0 * o + e1 * o_a + e2 * o_b)
                                         * (1.0 / (e0 + e1 + e2))).astype(BF16)
                elif stage_refs:
                    rows = pl.ds(r // STAGE, BLK, stride=STAGE)
                    stage_refs[0][p, r % STAGE, rows, :] = o
                    stage_refs[1][p, r % STAGE, rows, :] = lse
                else:
                    rows = pl.ds(j * BLK * dilation + r, BLK, stride=dilation)
                    o_ref[p, rows, :] = o
                    lse_ref[p, rows, :] = lse

        kb[0:BLK, :] = k_ref[rr, last, :]
        vb[0:BLK, :] = v_ref[rr, last, :]

    if not merge and stage_refs:
        for staged, out_ref in zip(stage_refs, (o_ref, lse_ref)):
            for p in range(HEADS // 2):
                for rho in range(STAGE):
                    out_ref[p, pl.ds(rho, STAGE * BLK, stride=STAGE), :] = staged[p, rho]


def _attention_group(q, k, v, batch, seq, group, others=None):
    d = DILATIONS[group]
    nb, nr = ATTN_BLOCKS[d], ATTN_RESIDUES[d]
    steps = seq // d // (BLK * nb)
    merge = others is not None
    two_pass = d == STAGE * STAGE and nr == d and nb == 1
    staging = [pltpu.VMEM((N_SLABS, STAGE, STAGE * BLK, LANES), F32)] * 2 if two_pass else []
    in_spec = pl.BlockSpec((None, nr, BLK * nb, ATTN_WIDTH), lambda b, n, r: (b, r, n, 0))
    slab_spec = pl.BlockSpec((None, N_SLABS, BLK * nb * d, LANES), lambda b, n, r: (b, 0, n, 0))
    slab_shape = jax.ShapeDtypeStruct((batch, N_SLABS, seq, LANES), F32)
    if merge:
        assert d == 1
        extra = [t for pair in others for t in pair]
        out_specs = pl.BlockSpec((BLK * nb, ATTN_WIDTH), lambda b, n, r: (b * steps + n, 0))
        out_shape = jax.ShapeDtypeStruct((batch * seq, ATTN_WIDTH), BF16)
    else:
        extra = []
        out_specs = [slab_spec, slab_spec]
        out_shape = [slab_shape, slab_shape]
    return pl.pallas_call(
        functools.partial(_attn_kernel, dilation=d, nb=nb, nr=nr, merge=merge),
        grid=(batch, steps, d // nr),
        in_specs=[in_spec, in_spec, in_spec] + [slab_spec] * len(extra),
        out_specs=out_specs,
        out_shape=out_shape,
        scratch_shapes=[pltpu.VMEM((d, 2 * BLK, ATTN_WIDTH), BF16),
                        pltpu.VMEM((d, 2 * BLK, ATTN_WIDTH), BF16)] + staging,
        compiler_params=_params("arbitrary", "arbitrary", "arbitrary"),
        name=f"attn_d{d}",
    )(q, k, v, *extra)


def _ssm_tables(lam_re, lam_im, log_dt, b_re, b_im, c_re, c_im):
    hi = lax.Precision.HIGHEST
    depth = lam_re.shape[0]
    lr = lam_re.astype(F32)
    li = lam_im.astype(F32)
    dt = jnp.exp(log_dt.astype(F32))[..., None]
    mag = jnp.exp(lr * dt)
    ang = li * dt
    abar_re = mag * jnp.cos(ang)
    abar_im = mag * jnp.sin(ang)
    nr = abar_re - 1.0
    ni = abar_im
    den = lr * lr + li * li
    cr = ((nr * lr + ni * li) / den)[:, :, None, :]
    ci = ((ni * lr - nr * li) / den)[:, :, None, :]
    brt = b_re.astype(F32).transpose(0, 1, 3, 2)
    bit = b_im.astype(F32).transpose(0, 1, 3, 2)
    bbar_re = cr * brt - ci * bit
    bbar_im = cr * bit + ci * brt

    def powers(tau):
        tau = tau.astype(F32)[None, None, :, None]
        pmag = jnp.exp((lr * dt)[:, :, None, :] * tau)
        pang = ang[:, :, None, :] * tau
        return pmag * jnp.cos(pang), pmag * jnp.sin(pang)

    pw_re, pw_im = powers(jnp.arange(CHUNK + 1))

    wide = (CHUNK + 1) * SSM_GROUP
    lane = jnp.arange(wide)
    rep = (lane[None, :] // SSM_GROUP == jnp.arange(CHUNK + 1)[:, None]).astype(F32)
    til = (lane[None, :] % SSM_GROUP == jnp.arange(SSM_GROUP)[:, None]).astype(F32)
    pr_l = jnp.einsum('dgtp,tl->dgpl', pw_re, rep, precision=hi)
    pi_l = jnp.einsum('dgtp,tl->dgpl', pw_im, rep, precision=hi)
    cr_l = jnp.einsum('dgcp,cl->dgpl', c_re.astype(F32), til, precision=hi)
    ci_l = jnp.einsum('dgcp,cl->dgpl', c_im.astype(F32), til, precision=hi)
    cp_re = cr_l * pr_l - ci_l * pi_l
    cp_im = cr_l * pi_l + ci_l * pr_l

    lag = (jnp.einsum('dgcp,dgpl->dgcl', bbar_re, cp_re[..., :CHUNK_COLS], precision=hi)
           - jnp.einsum('dgcp,dgpl->dgcl', bbar_im, cp_im[..., :CHUNK_COLS], precision=hi))
    col = jnp.arange(CHUNK_COLS)
    shift = (col[None, None, :] == col[None, :, None]
             + PIECE * jnp.arange(CHUNK)[:, None, None]).astype(BF16)
    toep = jnp.einsum('dgcl,slm->dgscm', lag.astype(BF16), shift)
    toep = toep.reshape(depth, SSM_GROUPS, CHUNK_COLS, CHUNK_COLS)

    rev_re, rev_im = powers(CHUNK - 1 - jnp.arange(CHUNK))
    rev_re = rev_re[:, :, :, None, :]
    rev_im = rev_im[:, :, :, None, :]
    win_re = (rev_re * bbar_re[:, :, None] - rev_im * bbar_im[:, :, None])
    win_im = (rev_re * bbar_im[:, :, None] + rev_im * bbar_re[:, :, None])
    win_re = win_re.reshape(depth, SSM_GROUPS, CHUNK_COLS, SSM_STATE)
    win_im = win_im.reshape(depth, SSM_GROUPS, CHUNK_COLS, SSM_STATE)

    wout_re = cp_re[..., SSM_GROUP:]
    wout_im = -cp_im[..., SSM_GROUP:]

    odd = (jnp.arange(SSM_GROUPS) % 2 == 1)[None, :, None, None]
    zc = jnp.zeros_like(win_re)
    w_in = jnp.concatenate([jnp.where(odd, zc, win_re), jnp.where(odd, win_re, zc),
                            jnp.where(odd, zc, win_im), jnp.where(odd, win_im, zc)], axis=3)
    zr = jnp.zeros_like(wout_re)
    w_out = jnp.concatenate([jnp.where(odd, zr, wout_re), jnp.where(odd, wout_re, zr),
                             jnp.where(odd, zr, wout_im), jnp.where(odd, wout_im, zr)], axis=2)
    a_re, a_im = abar_re, abar_im
    for _ in range(CHUNK.bit_length() - 1):
        a_re, a_im = a_re * a_re - a_im * a_im, 2.0 * a_re * a_im
    a_re = a_re.reshape(depth, STATE_ROWS, LANES)
    a_im = a_im.reshape(depth, STATE_ROWS, LANES)
    return toep, w_in.astype(BF16), w_out.astype(BF16), a_re, a_im


def _ssm_kernel(x_ref, win_ref, toep_ref, wout_ref, are_ref, aim_ref, d_ref, y_ref,
                sre_ref, sim_ref, *, batch, per_batch):
    phase = pl.program_id(0)
    pair = pl.program_id(1)
    nchunk = batch * per_batch
    pair_rows = pl.ds(pair, nchunk, stride=STATE_ROWS)

    @pl.when(phase == 0)
    def _():
        v = (jnp.dot(x_ref[0], win_ref[0], preferred_element_type=F32)
             + jnp.dot(x_ref[1], win_ref[1], preferred_element_type=F32))
        sre_ref[pair_rows, :] = v[:, :LANES]
        sim_ref[pair_rows, :] = v[:, LANES:]

    @pl.when((phase == 0) & (pair == pl.num_programs(1) - 1))
    def _():
        ar = are_ref[...]
        ai = aim_ref[...]

        def body(k, carry):
            new = []
            for b in range(batch):
                sre, sim = carry[b]
                rows = pl.ds(pl.multiple_of((b * per_batch + k) * STATE_ROWS, STATE_ROWS),
                             STATE_ROWS)
                vre = sre_ref[rows, :]
                vim = sim_ref[rows, :]
                sre_ref[rows, :] = sre
                sim_ref[rows, :] = sim
                new.append((ar * sre - ai * sim + vre, ar * sim + ai * sre + vim))
            return tuple(new)

        zero = jnp.zeros((STATE_ROWS, LANES), F32)
        lax.fori_loop(0, per_batch, body, tuple((zero, zero) for _ in range(batch)))

    @pl.when(phase == 1)
    def _():
        sp = jnp.concatenate([sre_ref[pair_rows, :], sim_ref[pair_rows, :]],
                             axis=1).astype(BF16)
        for h in range(2):
            x = x_ref[h]
            y = (jnp.dot(x, toep_ref[h], preferred_element_type=F32)
                 + jnp.dot(sp, wout_ref[h], preferred_element_type=F32)
                 + d_ref[h] * x.astype(F32))
            y_ref[h] = jax.nn.gelu(y).astype(BF16)


def _ssm(x, toep, w_in, w_out, a_re, a_im, d_tiled, layer, batch, seq):
    nchunk = x.shape[1]
    per_batch = seq // CHUNK
    pair3 = lambda shape: pl.BlockSpec((2,) + shape, lambda ph, q: (q, 0, 0))
    table = lambda shape: pl.BlockSpec((None, 2) + shape, lambda ph, q: (layer, q, 0, 0))
    return pl.pallas_call(
        functools.partial(_ssm_kernel, batch=batch, per_batch=per_batch),
        grid=(2, SSM_GROUPS // 2),
        in_specs=[pair3((nchunk, CHUNK_COLS)), table((CHUNK_COLS, CHUNK_COLS)),
                  table((CHUNK_COLS, CHUNK_COLS)), table((CHUNK_COLS, CHUNK_COLS)),
                  _layer((STATE_ROWS, LANES), layer), _layer((STATE_ROWS, LANES), layer),
                  table((1, CHUNK_COLS))],
        out_specs=pl.BlockSpec((2, nchunk, CHUNK_COLS), lambda ph, q: (q * ph, 0, 0)),
        out_shape=jax.ShapeDtypeStruct((SSM_GROUPS, nchunk, CHUNK_COLS), BF16),
        scratch_shapes=[pltpu.VMEM((nchunk * STATE_ROWS, LANES), F32),
                        pltpu.VMEM((nchunk * STATE_ROWS, LANES), F32)],
        compiler_params=_params("arbitrary", "arbitrary"),
        name="ssm",
    )(x, w_in, toep, w_out, a_re, a_im, d_tiled)


FF_CHUNK = D_FF


def _mix_ffn_kernel(x_ref, a_ref, ys_ref, gate_ref, wap_ref, wa_ref, wb_ref, wout_ref,
                    g_ref, wg_ref, wu_ref, wd_ref, out_ref, slab_ref):
    tm = x_ref.shape[0]
    a_out = jnp.dot(a_ref[...], wap_ref[...], preferred_element_type=F32)

    for s in range(N_SLABS):
        for half in range(CHUNK // PIECES):
            ws = [ys_ref[s * PIECES + i, :, half * LANES:(half + 1) * LANES].astype(F32)
                  for i in range(PIECES)]
            for j, v in enumerate(_piece_transpose(ws)):
                slab_ref[s, pl.ds(half * PIECES + j, tm // CHUNK, stride=CHUNK), :] = v
    y = jnp.concatenate([slab_ref[c] for c in range(N_SLABS)], axis=1).astype(BF16)
    s_out = (jnp.dot(y, wa_ref[...], preferred_element_type=F32)
             * jax.nn.sigmoid(jnp.dot(y, wb_ref[...], preferred_element_type=F32)))
    mix = (gate_ref[:, :D_MODEL].astype(F32) * a_out
           + gate_ref[:, D_MODEL:].astype(F32) * s_out)
    x = x_ref[...] + jnp.dot(mix.astype(BF16), wout_ref[...], preferred_element_type=F32)

    h = _rms_norm(x, g_ref[...]).astype(BF16)
    for c in range(D_FF // FF_CHUNK):
        cols = slice(c * FF_CHUNK, (c + 1) * FF_CHUNK)
        gate = jnp.dot(h, wg_ref[:, cols], preferred_element_type=F32)
        up = jnp.dot(h, wu_ref[:, cols], preferred_element_type=F32)
        hidden = (jax.nn.silu(gate) * up).astype(BF16)
        x = x + jnp.dot(hidden, wd_ref[cols, :], preferred_element_type=F32)
    out_ref[...] = x


def _mix_ffn(x, a, ys, gates, wap, wa, wb, wout, g_ffn, wg, wu, wd, layer):
    n = x.shape[0]
    tm = TOKEN_TILE
    row = lambda i: (i, 0)
    return pl.pallas_call(
        _mix_ffn_kernel,
        grid=(n // tm,),
        in_specs=[pl.BlockSpec((tm, D_MODEL), row),
                  pl.BlockSpec((tm, ATTN_WIDTH), row),
                  pl.BlockSpec((SSM_GROUPS, tm // CHUNK, CHUNK_COLS), lambda i: (0, i, 0)),
                  pl.BlockSpec((tm, 2 * D_MODEL), row),
                  _layer((ATTN_WIDTH, D_MODEL), layer), _layer((SSM_WIDTH, D_MODEL), layer),
                  _layer((SSM_WIDTH, D_MODEL), layer), _layer((D_MODEL, D_MODEL), layer),
                  _layer((1, D_MODEL), layer), _layer((D_MODEL, D_FF), layer),
                  _layer((D_MODEL, D_FF), layer), _layer((D_FF, D_MODEL), layer)],
        out_specs=pl.BlockSpec((tm, D_MODEL), row),
        out_shape=jax.ShapeDtypeStruct((n, D_MODEL), F32),
        scratch_shapes=[pltpu.VMEM((N_SLABS, tm, LANES), F32)],
        compiler_params=_params("arbitrary"),
        name="mix_ffn",
    )(x, a, ys, gates, wap, wa, wb, wout, g_ffn, wg, wu, wd)


def kernel(x, g_mix, w_in, g_q, g_k, w_attn_proj, lambda_re, lambda_im, log_dt, b_re, b_im,
           c_re, c_im, d_skip, w_glu_a, w_glu_b, w_out, g_ffn, w_ffn_gate, w_ffn_up, w_ffn_down):
    batch, seq, _ = x.shape
    depth = w_in.shape[0]
    n = batch * seq
    xf = x.reshape(n, D_MODEL).astype(F32)

    blk = jnp.arange(COL_TILE // 2) // HEAD_DIM
    bd = (blk[:, None] == blk[None, :]).astype(BF16)
    gq = jnp.tile(g_q.astype(F32) * (LOG2E * HEAD_DIM ** -0.5), (1, HEADS))[:, None]
    gk = jnp.tile(g_k.astype(F32), (1, HEADS))[:, None]
    toep, s_in, s_out, a_re, a_im = _ssm_tables(lambda_re, lambda_im, log_dt, b_re, b_im,
                                                c_re, c_im)
    d_tiled = jnp.tile(d_skip.astype(F32).reshape(depth, SSM_GROUPS, 1, SSM_GROUP),
                       (1, 1, 1, CHUNK))

    row3 = lambda g: g.astype(F32)[:, None, :]
    g_mix, g_ffn = row3(g_mix), row3(g_ffn)
    (w_in, w_attn_proj, w_glu_a, w_glu_b, w_out, w_ffn_gate, w_ffn_up, w_ffn_down) = (
        w.astype(BF16) for w in (w_in, w_attn_proj, w_glu_a, w_glu_b, w_out,
                                 w_ffn_gate, w_ffn_up, w_ffn_down))

    for l in range(depth):
        q, k, v, xs, gates = _in_proj(xf, g_mix, w_in, gq, gk, bd, l, batch, seq)
        dilated = [_attention_group(q[g], k[g], v[g], batch, seq, g)
                   for g in range(1, N_GROUPS)]
        a = _attention_group(q[0], k[0], v[0], batch, seq, 0, others=dilated)
        ys = _ssm(xs, toep, s_in, s_out, a_re, a_im, d_tiled, l, batch, seq)
        xf = _mix_ffn(xf, a, ys, gates, w_attn_proj, w_glu_a, w_glu_b, w_out, g_ffn,
                      w_ffn_gate, w_ffn_up, w_ffn_down, l)
    return xf.reshape(batch, seq, D_MODEL).astype(x.dtype)
```

```python
import functools

import jax
import jax.numpy as jnp
from jax import lax
from jax.experimental import pallas as pl
from jax.experimental.pallas import tpu as pltpu

F32 = jnp.float32
BF16 = jnp.bfloat16

D_MODEL = 1024
HEAD_DIM = 64
HEADS = 8
ATTN_WIDTH = HEADS * HEAD_DIM
DILATIONS = (1, 4, 16)
N_GROUPS = len(DILATIONS)
BLK = 128
SSM_WIDTH = 512
SSM_GROUP = 16
SSM_GROUPS = 32
SSM_STATE = 64
CHUNK = 16
CHUNK_COLS = CHUNK * SSM_GROUP
D_FF = 2816
IN_COLS = 3 * N_GROUPS * ATTN_WIDTH + SSM_WIDTH + 2 * D_MODEL
EPS = 1e-6
LOG2E = 1.4426950408889634

VMEM_LIMIT_BYTES = 56 * 1024 * 1024
LANES = 128
COL_TILE = 512
TOKEN_TILE = 512
PIECE = SSM_GROUP
PIECES = LANES // PIECE
N_SLABS = ATTN_WIDTH // LANES
STATE_ROWS = SSM_GROUPS // 2
STAGE = 4


def _params(*semantics):
    return pltpu.CompilerParams(dimension_semantics=semantics,
                                vmem_limit_bytes=VMEM_LIMIT_BYTES)


def _resident(shape):
    return pl.BlockSpec(shape, lambda *_: (0,) * len(shape),
                        pipeline_mode=pl.Buffered(1))


def _layer(shape, l):
    return pl.BlockSpec((None,) + shape, lambda *_: (l,) + (0,) * len(shape),
                        pipeline_mode=pl.Buffered(1))


def _rms_norm(x, gain):
    ms = jnp.mean(x * x, axis=-1, keepdims=True)
    return x * lax.rsqrt(ms + EPS) * gain


def _piece_transpose(vs):
    vs = list(vs)
    piece = lax.broadcasted_iota(jnp.int32, vs[0].shape, 1) // PIECE
    for k in (4, 2, 1):
        upper = (piece & k) != 0
        for j in range(PIECES):
            if j & k:
                continue
            a, b = vs[j], vs[j + k]
            vs[j] = jnp.where(upper, pltpu.roll(b, PIECE * k, 1), a)
            vs[j + k] = jnp.where(upper, b, pltpu.roll(a, LANES - PIECE * k, 1))
    return vs


def _in_proj_kernel(x_ref, g_ref, w_ref, gq_ref, gk_ref, bd_ref, *refs):
    qkv_refs = refs[:3 * N_GROUPS]
    xs_ref, gate_ref, h_ref, slab_ref, quad_ref = refs[3 * N_GROUPS:]
    tm = x_ref.shape[0]
    n_slabs = D_MODEL // LANES

    h = _rms_norm(x_ref[...], g_ref[...])
    h_ref[0] = h.astype(BF16)
    for c in range(n_slabs):
        slab_ref[c] = h[:, c * LANES:(c + 1) * LANES]
    for c in range(n_slabs):
        cols = slice(c * LANES, (c + 1) * LANES)
        for rho in range(STAGE):
            quarter = slab_ref[c, pl.ds(rho, tm // STAGE, stride=STAGE), :]
            quad_ref[c, rho] = quarter
            h_ref[1, rho * (tm // STAGE):(rho + 1) * (tm // STAGE), cols] = quarter.astype(BF16)
    for c in range(n_slabs):
        cols = slice(c * LANES, (c + 1) * LANES)
        for r in range(CHUNK):
            rows = quad_ref[c, r % STAGE, pl.ds(r // STAGE, tm // CHUNK, stride=STAGE), :]
            h_ref[2, r * (tm // CHUNK):(r + 1) * (tm // CHUNK), cols] = rows.astype(BF16)
    order = {1: 0, STAGE: 1, STAGE * STAGE: 2}

    def proj(j, d=1):
        return jnp.dot(h_ref[order[d]], w_ref[:, j * COL_TILE:(j + 1) * COL_TILE],
                       preferred_element_type=F32)

    def head_norm(z, gain):
        zz = (z * z).astype(BF16)
        half = COL_TILE // 2
        ss = jnp.concatenate(
            [jnp.dot(zz[:, :half], bd_ref[...], preferred_element_type=F32),
             jnp.dot(zz[:, half:], bd_ref[...], preferred_element_type=F32)], axis=1)
        return z * lax.rsqrt(ss * (1.0 / HEAD_DIM) + EPS) * gain

    def emit(z, out_ref, d):
        for r in range(d):
            out_ref[r] = z[r * (tm // d):(r + 1) * (tm // d)].astype(BF16)

    for g, d in enumerate(DILATIONS):
        emit(head_norm(proj(g, d), gq_ref[...]), qkv_refs[g], d)
        emit(head_norm(proj(N_GROUPS + g, d), gk_ref[...]), qkv_refs[N_GROUPS + g], d)
        emit(proj(2 * N_GROUPS + g, d), qkv_refs[2 * N_GROUPS + g], d)

    u = proj(3 * N_GROUPS, CHUNK)
    per = tm // CHUNK
    for a in range(N_SLABS):
        for half in range(CHUNK // PIECES):
            vs = [u[(half * PIECES + j) * per:(half * PIECES + j + 1) * per,
                    a * LANES:(a + 1) * LANES] for j in range(PIECES)]
            for i, w in enumerate(_piece_transpose(vs)):
                xs_ref[a * PIECES + i, :, half * LANES:(half + 1) * LANES] = w.astype(BF16)

    for j in range(2 * D_MODEL // COL_TILE):
        cols = slice(j * COL_TILE, (j + 1) * COL_TILE)
        gate_ref[:, cols] = jax.nn.sigmoid(proj(3 * N_GROUPS + 1 + j)).astype(BF16)


def _in_proj(x, g_mix, w_in, gq, gk, bd, layer, batch, seq):
    n = x.shape[0]
    tm = TOKEN_TILE
    tiles = seq // tm
    row = lambda i: (i, 0)
    dil_specs, dil_shapes = [], []
    for _ in range(3):
        for d in DILATIONS:
            dil_specs.append(pl.BlockSpec((None, d, tm // d, ATTN_WIDTH),
                                          lambda i: (i // tiles, 0, i % tiles, 0)))
            dil_shapes.append(jax.ShapeDtypeStruct((batch, d, seq // d, ATTN_WIDTH), BF16))
    outs = pl.pallas_call(
        _in_proj_kernel,
        grid=(n // tm,),
        in_specs=[pl.BlockSpec((tm, D_MODEL), row),
                  _layer((1, D_MODEL), layer),
                  _layer((D_MODEL, IN_COLS), layer),
                  _layer((1, COL_TILE), layer),
                  _layer((1, COL_TILE), layer),
                  _resident((COL_TILE // 2, COL_TILE // 2))],
        out_specs=dil_specs + [
            pl.BlockSpec((SSM_GROUPS, tm // CHUNK, CHUNK_COLS), lambda i: (0, i, 0)),
            pl.BlockSpec((tm, 2 * D_MODEL), row)],
        out_shape=dil_shapes + [
            jax.ShapeDtypeStruct((SSM_GROUPS, n // CHUNK, CHUNK_COLS), BF16),
            jax.ShapeDtypeStruct((n, 2 * D_MODEL), BF16)],
        scratch_shapes=[pltpu.VMEM((3, tm, D_MODEL), BF16),
                        pltpu.VMEM((D_MODEL // LANES, tm, LANES), F32),
                        pltpu.VMEM((D_MODEL // LANES, STAGE, tm // STAGE, LANES), F32)],
        compiler_params=_params("arbitrary"),
        name="in_proj",
    )(x, g_mix, w_in, gq, gk, bd)
    return outs[0:3], outs[3:6], outs[6:9], outs[9], outs[10]


ATTN_BLOCKS = {1: 8, 4: 8, 16: 1}
ATTN_RESIDUES = {1: 1, 4: 1, 16: 16}


def _attn_kernel(q_ref, k_ref, v_ref, *refs, dilation, nb, nr, merge):
    if merge:
        other = (refs[0], refs[1]), (refs[2], refs[3])
        a_ref, kbuf, vbuf = refs[4:]
        stage_refs = ()
    else:
        o_ref, lse_ref, kbuf, vbuf = refs[:4]
        stage_refs = refs[4:]
    step = pl.program_id(1)
    last = slice((nb - 1) * BLK, nb * BLK)

    qi = lax.broadcasted_iota(jnp.int32, (BLK, 2 * BLK), 0)
    kj = lax.broadcasted_iota(jnp.int32, (BLK, 2 * BLK), 1)
    neg = jnp.full((BLK, 2 * BLK), -jnp.inf, F32)
    zero = jnp.zeros((BLK, 2 * BLK), F32)
    cur_bias = jnp.where(kj - BLK <= qi, zero, neg)
    band = jnp.where(kj < BLK, jnp.where(kj >= qi, zero, neg), cur_bias)
    head = jnp.where(kj < BLK, jnp.where(step > 0, band, neg), cur_bias)
    band = jnp.concatenate([band, band], axis=0)
    head = jnp.concatenate([head, head], axis=0)

    lane_q = lax.broadcasted_iota(jnp.int32, (BLK, LANES), 1) < HEAD_DIM
    lane_kv = lax.broadcasted_iota(jnp.int32, (2 * BLK, LANES), 1) < HEAD_DIM
    head_of_row = lax.broadcasted_iota(jnp.int32, (4 * BLK, LANES), 0) // (2 * BLK)
    head_of_lane = lax.broadcasted_iota(jnp.int32, (4 * BLK, LANES), 1) // HEAD_DIM
    den_cols = jnp.where(head_of_row == head_of_lane, 1.0, 0.0).astype(BF16)

    residues = [rr if nr == dilation else pl.program_id(2) * nr + rr for rr in range(nr)]

    @pl.when(step == 0)
    def _():
        for r in residues:
            kbuf[r, 0:BLK, :] = jnp.zeros((BLK, ATTN_WIDTH), BF16)
            vbuf[r, 0:BLK, :] = jnp.zeros((BLK, ATTN_WIDTH), BF16)

    for rr, r in enumerate(residues):
        kb = kbuf.at[r]
        vb = vbuf.at[r]
        kb[BLK:2 * BLK, :] = k_ref[rr, 0:BLK, :]
        vb[BLK:2 * BLK, :] = v_ref[rr, 0:BLK, :]

        for j in range(nb):
            for p in range(HEADS // 2):
                cols = slice(p * LANES, (p + 1) * LANES)
                if j == 0:
                    keys, vals, bias = kb[:, cols], vb[:, cols], head
                else:
                    window = slice((j - 1) * BLK, (j + 1) * BLK)
                    keys, vals, bias = k_ref[rr, window, cols], v_ref[rr, window, cols], band
                qp = q_ref[rr, j * BLK:(j + 1) * BLK, cols]
                zq = jnp.zeros_like(qp)
                q2 = jnp.concatenate([jnp.where(lane_q, qp, zq), jnp.where(lane_q, zq, qp)],
                                     axis=0)
                s = lax.dot_general(q2, keys, (((1,), (1,)), ((), ())),
                                    preferred_element_type=F32) + bias
                m = jnp.max(s, axis=-1, keepdims=True)
                eb = jnp.exp2(s - m).astype(BF16)
                zv = jnp.zeros_like(vals)
                v2 = jnp.concatenate([jnp.where(lane_kv, vals, zv),
                                      jnp.where(lane_kv, zv, vals)], axis=0)
                acc = jnp.dot(jnp.concatenate([eb[:BLK], eb[BLK:]], axis=1),
                              jnp.concatenate([v2, den_cols], axis=1),
                              preferred_element_type=F32)
                den_lanes = acc[:, LANES:]
                m_lanes = jnp.where(lane_q, m[:BLK], m[BLK:])
                lse = m_lanes + jnp.log2(den_lanes)
                o = acc[:, :LANES] * (1.0 / den_lanes)
                if merge:
                    rows = slice(j * BLK, (j + 1) * BLK)
                    (o_a, l_a), (o_b, l_b) = [(o_r[p, rows, :], l_r[p, rows, :])
                                              for o_r, l_r in other]
                    top = jnp.maximum(jnp.maximum(lse, l_a), l_b)
                    e0, e1, e2 = jnp.exp2(lse - top), jnp.exp2(l_a - top), jnp.exp2(l_b - top)
                    a_ref[rows, cols] = ((e0 * o + e1 * o_a + e2 * o_b)
                                         * (1.0 / (e0 + e1 + e2))).astype(BF16)
                elif stage_refs:
                    rows = pl.ds(r // STAGE, BLK, stride=STAGE)
                    stage_refs[0][p, r % STAGE, rows, :] = o
                    stage_refs[1][p, r % STAGE, rows, :] = lse
                else:
                    rows = pl.ds(j * BLK * dilation + r, BLK, stride=dilation)
                    o_ref[p, rows, :] = o
                    lse_ref[p, rows, :] = lse

        kb[0:BLK, :] = k_ref[rr, last, :]
        vb[0:BLK, :] = v_ref[rr, last, :]

    if not merge and stage_refs:
        for staged, out_ref in zip(stage_refs, (o_ref, lse_ref)):
            for p in range(HEADS // 2):
                for rho in range(STAGE):
                    out_ref[p, pl.ds(rho, STAGE * BLK, stride=STAGE), :] = staged[p, rho]


def _attention_group(q, k, v, batch, seq, group, others=None):
    d = DILATIONS[group]
    nb, nr = ATTN_BLOCKS[d], ATTN_RESIDUES[d]
    steps = seq // d // (BLK * nb)
    merge = others is not None
    two_pass = d == STAGE * STAGE and nr == d and nb == 1
    staging = [pltpu.VMEM((N_SLABS, STAGE, STAGE * BLK, LANES), F32)] * 2 if two_pass else []
    in_spec = pl.BlockSpec((None, nr, BLK * nb, ATTN_WIDTH), lambda b, n, r: (b, r, n, 0))
    slab_spec = pl.BlockSpec((None, N_SLABS, BLK * nb * d, LANES), lambda b, n, r: (b, 0, n, 0))
    slab_shape = jax.ShapeDtypeStruct((batch, N_SLABS, seq, LANES), F32)
    if merge:
        assert d == 1
        extra = [t for pair in others for t in pair]
        out_specs = pl.BlockSpec((BLK * nb, ATTN_WIDTH), lambda b, n, r: (b * steps + n, 0))
        out_shape = jax.ShapeDtypeStruct((batch * seq, ATTN_WIDTH), BF16)
    else:
        extra = []
        out_specs = [slab_spec, slab_spec]
        out_shape = [slab_shape, slab_shape]
    return pl.pallas_call(
        functools.partial(_attn_kernel, dilation=d, nb=nb, nr=nr, merge=merge),
        grid=(batch, steps, d // nr),
        in_specs=[in_spec, in_spec, in_spec] + [slab_spec] * len(extra),
        out_specs=out_specs,
        out_shape=out_shape,
        scratch_shapes=[pltpu.VMEM((d, 2 * BLK, ATTN_WIDTH), BF16),
                        pltpu.VMEM((d, 2 * BLK, ATTN_WIDTH), BF16)] + staging,
        compiler_params=_params("arbitrary", "arbitrary", "arbitrary"),
        name=f"attn_d{d}",
    )(q, k, v, *extra)


def _ssm_tables(lam_re, lam_im, log_dt, b_re, b_im, c_re, c_im):
    hi = lax.Precision.HIGHEST
    depth = lam_re.shape[0]
    lr = lam_re.astype(F32)
    li = lam_im.astype(F32)
    dt = jnp.exp(log_dt.astype(F32))[..., None]
    mag = jnp.exp(lr * dt)
    ang = li * dt
    abar_re = mag * jnp.cos(ang)
    abar_im = mag * jnp.sin(ang)
    nr = abar_re - 1.0
    ni = abar_im
    den = lr * lr + li * li
    cr = ((nr * lr + ni * li) / den)[:, :, None, :]
    ci = ((ni * lr - nr * li) / den)[:, :, None, :]
    brt = b_re.astype(F32).transpose(0, 1, 3, 2)
    bit = b_im.astype(F32).transpose(0, 1, 3, 2)
    bbar_re = cr * brt - ci * bit
    bbar_im = cr * bit + ci * brt

    def powers(tau):
        tau = tau.astype(F32)[None, None, :, None]
        pmag = jnp.exp((lr * dt)[:, :, None, :] * tau)
        pang = ang[:, :, None, :] * tau
        return pmag * jnp.cos(pang), pmag * jnp.sin(pang)

    pw_re, pw_im = powers(jnp.arange(CHUNK + 1))

    wide = (CHUNK + 1) * SSM_GROUP
    lane = jnp.arange(wide)
    rep = (lane[None, :] // SSM_GROUP == jnp.arange(CHUNK + 1)[:, None]).astype(F32)
    til = (lane[None, :] % SSM_GROUP == jnp.arange(SSM_GROUP)[:, None]).astype(F32)
    pr_l = jnp.einsum('dgtp,tl->dgpl', pw_re, rep, precision=hi)
    pi_l = jnp.einsum('dgtp,tl->dgpl', pw_im, rep, precision=hi)
    cr_l = jnp.einsum('dgcp,cl->dgpl', c_re.astype(F32), til, precision=hi)
    ci_l = jnp.einsum('dgcp,cl->dgpl', c_im.astype(F32), til, precision=hi)
    cp_re = cr_l * pr_l - ci_l * pi_l
    cp_im = cr_l * pi_l + ci_l * pr_l

    lag = (jnp.einsum('dgcp,dgpl->dgcl', bbar_re, cp_re[..., :CHUNK_COLS], precision=hi)
           - jnp.einsum('dgcp,dgpl->dgcl', bbar_im, cp_im[..., :CHUNK_COLS], precision=hi))
    col = jnp.arange(CHUNK_COLS)
    shift = (col[None, None, :] == col[None, :, None]
             + PIECE * jnp.arange(CHUNK)[:, None, None]).astype(BF16)
    toep = jnp.einsum('dgcl,slm->dgscm', lag.astype(BF16), shift)
    toep = toep.reshape(depth, SSM_GROUPS, CHUNK_COLS, CHUNK_COLS)

    rev_re, rev_im = powers(CHUNK - 1 - jnp.arange(CHUNK))
    rev_re = rev_re[:, :, :, None, :]
    rev_im = rev_im[:, :, :, None, :]
    win_re = (rev_re * bbar_re[:, :, None] - rev_im * bbar_im[:, :, None])
    win_im = (rev_re * bbar_im[:, :, None] + rev_im * bbar_re[:, :, None])
    win_re = win_re.reshape(depth, SSM_GROUPS, CHUNK_COLS, SSM_STATE)
    win_im = win_im.reshape(depth, SSM_GROUPS, CHUNK_COLS, SSM_STATE)

    wout_re = cp_re[..., SSM_GROUP:]
    wout_im = -cp_im[..., SSM_GROUP:]

    odd = (jnp.arange(SSM_GROUPS) % 2 == 1)[None, :, None, None]
    zc = jnp.zeros_like(win_re)
    w_in = jnp.concatenate([jnp.where(odd, zc, win_re), jnp.where(odd, win_re, zc),
                            jnp.where(odd, zc, win_im), jnp.where(odd, win_im, zc)], axis=3)
    zr = jnp.zeros_like(wout_re)
    w_out = jnp.concatenate([jnp.where(odd, zr, wout_re), jnp.where(odd, wout_re, zr),
                             jnp.where(odd, zr, wout_im), jnp.where(odd, wout_im, zr)], axis=2)
    a_re, a_im = abar_re, abar_im
    for _ in range(CHUNK.bit_length() - 1):
        a_re, a_im = a_re * a_re - a_im * a_im, 2.0 * a_re * a_im
    a_re = a_re.reshape(depth, STATE_ROWS, LANES)
    a_im = a_im.reshape(depth, STATE_ROWS, LANES)
    return toep, w_in.astype(BF16), w_out.astype(BF16), a_re, a_im


def _ssm_kernel(x_ref, win_ref, toep_ref, wout_ref, are_ref, aim_ref, d_ref, y_ref,
                sre_ref, sim_ref, *, batch, per_batch):
    phase = pl.program_id(0)
    pair = pl.program_id(1)
    nchunk = batch * per_batch
    pair_rows = pl.ds(pair, nchunk, stride=STATE_ROWS)

    @pl.when(phase == 0)
    def _():
        v = (jnp.dot(x_ref[0], win_ref[0], preferred_element_type=F32)
             + jnp.dot(x_ref[1], win_ref[1], preferred_element_type=F32))
        sre_ref[pair_rows, :] = v[:, :LANES]
        sim_ref[pair_rows, :] = v[:, LANES:]

    @pl.when((phase == 0) & (pair == pl.num_programs(1) - 1))
    def _():
        ar = are_ref[...]
        ai = aim_ref[...]

        def body(k, carry):
            new = []
            for b in range(batch):
                sre, sim = carry[b]
                rows = pl.ds(pl.multiple_of((b * per_batch + k) * STATE_ROWS, STATE_ROWS),
                             STATE_ROWS)
                vre = sre_ref[rows, :]
                vim = sim_ref[rows, :]
                sre_ref[rows, :] = sre
                sim_ref[rows, :] = sim
                new.append((ar * sre - ai * sim + vre, ar * sim + ai * sre + vim))
            return tuple(new)

        zero = jnp.zeros((STATE_ROWS, LANES), F32)
        lax.fori_loop(0, per_batch, body, tuple((zero, zero) for _ in range(batch)))

    @pl.when(phase == 1)
    def _():
        sp = jnp.concatenate([sre_ref[pair_rows, :], sim_ref[pair_rows, :]],
                             axis=1).astype(BF16)
        for h in range(2):
            x = x_ref[h]
            y = (jnp.dot(x, toep_ref[h], preferred_element_type=F32)
                 + jnp.dot(sp, wout_ref[h], preferred_element_type=F32)
                 + d_ref[h] * x.astype(F32))
            y_ref[h] = jax.nn.gelu(y).astype(BF16)


def _ssm(x, toep, w_in, w_out, a_re, a_im, d_tiled, layer, batch, seq):
    nchunk = x.shape[1]
    per_batch = seq // CHUNK
    pair3 = lambda shape: pl.BlockSpec((2,) + shape, lambda ph, q: (q, 0, 0))
    table = lambda shape: pl.BlockSpec((None, 2) + shape, lambda ph, q: (layer, q, 0, 0))
    return pl.pallas_call(
        functools.partial(_ssm_kernel, batch=batch, per_batch=per_batch),
        grid=(2, SSM_GROUPS // 2),
        in_specs=[pair3((nchunk, CHUNK_COLS)), table((CHUNK_COLS, CHUNK_COLS)),
                  table((CHUNK_COLS, CHUNK_COLS)), table((CHUNK_COLS, CHUNK_COLS)),
                  _layer((STATE_ROWS, LANES), layer), _layer((STATE_ROWS, LANES), layer),
                  table((1, CHUNK_COLS))],
        out_specs=pl.BlockSpec((2, nchunk, CHUNK_COLS), lambda ph, q: (q * ph, 0, 0)),
        out_shape=jax.ShapeDtypeStruct((SSM_GROUPS, nchunk, CHUNK_COLS), BF16),
        scratch_shapes=[pltpu.VMEM((nchunk * STATE_ROWS, LANES), F32),
                        pltpu.VMEM((nchunk * STATE_ROWS, LANES), F32)],
        compiler_params=_params("arbitrary", "arbitrary"),
        name="ssm",
    )(x, w_in, toep, w_out, a_re, a_im, d_tiled)


FF_CHUNK = D_FF


def _mix_ffn_kernel(x_ref, a_ref, ys_ref, gate_ref, wap_ref, wa_ref, wb_ref, wout_ref,
                    g_ref, wg_ref, wu_ref, wd_ref, out_ref, slab_ref):
    tm = x_ref.shape[0]
    a_out = jnp.dot(a_ref[...], wap_ref[...], preferred_element_type=F32)

    for s in range(N_SLABS):
        for half in range(CHUNK // PIECES):
            ws = [ys_ref[s * PIECES + i, :, half * LANES:(half + 1) * LANES].astype(F32)
                  for i in range(PIECES)]
            for j, v in enumerate(_piece_transpose(ws)):
                slab_ref[s, pl.ds(half * PIECES + j, tm // CHUNK, stride=CHUNK), :] = v
    y = jnp.concatenate([slab_ref[c] for c in range(N_SLABS)], axis=1).astype(BF16)
    s_out = (jnp.dot(y, wa_ref[...], preferred_element_type=F32)
             * jax.nn.sigmoid(jnp.dot(y, wb_ref[...], preferred_element_type=F32)))
    mix = (gate_ref[:, :D_MODEL].astype(F32) * a_out
           + gate_ref[:, D_MODEL:].astype(F32) * s_out)
    x = x_ref[...] + jnp.dot(mix.astype(BF16), wout_ref[...], preferred_element_type=F32)

    h = _rms_norm(x, g_ref[...]).astype(BF16)
    for c in range(D_FF // FF_CHUNK):
        cols = slice(c * FF_CHUNK, (c + 1) * FF_CHUNK)
        gate = jnp.dot(h, wg_ref[:, cols], preferred_element_type=F32)
        up = jnp.dot(h, wu_ref[:, cols], preferred_element_type=F32)
        hidden = (jax.nn.silu(gate) * up).astype(BF16)
        x = x + jnp.dot(hidden, wd_ref[cols, :], preferred_element_type=F32)
    out_ref[...] = x


def _mix_ffn(x, a, ys, gates, wap, wa, wb, wout, g_ffn, wg, wu, wd, layer):
    n = x.shape[0]
    tm = TOKEN_TILE
    row = lambda i: (i, 0)
    return pl.pallas_call(
        _mix_ffn_kernel,
        grid=(n // tm,),
        in_specs=[pl.BlockSpec((tm, D_MODEL), row),
                  pl.BlockSpec((tm, ATTN_WIDTH), row),
                  pl.BlockSpec((SSM_GROUPS, tm // CHUNK, CHUNK_COLS), lambda i: (0, i, 0)),
                  pl.BlockSpec((tm, 2 * D_MODEL), row),
                  _layer((ATTN_WIDTH, D_MODEL), layer), _layer((SSM_WIDTH, D_MODEL), layer),
                  _layer((SSM_WIDTH, D_MODEL), layer), _layer((D_MODEL, D_MODEL), layer),
                  _layer((1, D_MODEL), layer), _layer((D_MODEL, D_FF), layer),
                  _layer((D_MODEL, D_FF), layer), _layer((D_FF, D_MODEL), layer)],
        out_specs=pl.BlockSpec((tm, D_MODEL), row),
        out_shape=jax.ShapeDtypeStruct((n, D_MODEL), F32),
        scratch_shapes=[pltpu.VMEM((N_SLABS, tm, LANES), F32)],
        compiler_params=_params("arbitrary"),
        name="mix_ffn",
    )(x, a, ys, gates, wap, wa, wb, wout, g_ffn, wg, wu, wd)


def kernel(x, g_mix, w_in, g_q, g_k, w_attn_proj, lambda_re, lambda_im, log_dt, b_re, b_im,
           c_re, c_im, d_skip, w_glu_a, w_glu_b, w_out, g_ffn, w_ffn_gate, w_ffn_up, w_ffn_down):
    batch, seq, _ = x.shape
    depth = w_in.shape[0]
    n = batch * seq
    xf = x.reshape(n, D_MODEL).astype(F32)

    blk = jnp.arange(COL_TILE // 2) // HEAD_DIM
    bd = (blk[:, None] == blk[None, :]).astype(BF16)
    gq = jnp.tile(g_q.astype(F32) * (LOG2E * HEAD_DIM ** -0.5), (1, HEADS))[:, None]
    gk = jnp.tile(g_k.astype(F32), (1, HEADS))[:, None]
    toep, s_in, s_out, a_re, a_im = _ssm_tables(lambda_re, lambda_im, log_dt, b_re, b_im,
                                                c_re, c_im)
    d_tiled = jnp.tile(d_skip.astype(F32).reshape(depth, SSM_GROUPS, 1, SSM_GROUP),
                       (1, 1, 1, CHUNK))

    row3 = lambda g: g.astype(F32)[:, None, :]
    g_mix, g_ffn = row3(g_mix), row3(g_ffn)
    (w_in, w_attn_proj, w_glu_a, w_glu_b, w_out, w_ffn_gate, w_ffn_up, w_ffn_down) = (
        w.astype(BF16) for w in (w_in, w_attn_proj, w_glu_a, w_glu_b, w_out,
                                 w_ffn_gate, w_ffn_up, w_ffn_down))

    for l in range(depth):
        q, k, v, xs, gates = _in_proj(xf, g_mix, w_in, gq, gk, bd, l, batch, seq)
        dilated = [_attention_group(q[g], k[g], v[g], batch, seq, g)
                   for g in range(1, N_GROUPS)]
        a = _attention_group(q[0], k[0], v[0], batch, seq, 0, others=dilated)
        ys = _ssm(xs, toep, s_in, s_out, a_re, a_im, d_tiled, l, batch, seq)
        xf = _mix_ffn(xf, a, ys, gates, w_attn_proj, w_glu_a, w_glu_b, w_out, g_ffn,
                      w_ffn_gate, w_ffn_up, w_ffn_down, l)
    return xf.reshape(batch, seq, D_MODEL).astype(x.dtype)
```

```python
import functools

import jax
import jax.numpy as jnp
from jax import lax
from jax.experimental import pallas as pl
from jax.experimental.pallas import tpu as pltpu

F32 = jnp.float32
BF16 = jnp.bfloat16

D_MODEL = 1024
HEAD_DIM = 64
HEADS = 8
ATTN_WIDTH = HEADS * HEAD_DIM
DILATIONS = (1, 4, 16)
N_GROUPS = len(DILATIONS)
BLK = 128
SSM_WIDTH = 512
SSM_GROUP = 16
SSM_GROUPS = 32
SSM_STATE = 64
CHUNK = 16
CHUNK_COLS = CHUNK * SSM_GROUP
D_FF = 2816
IN_COLS = 3 * N_GROUPS * ATTN_WIDTH + SSM_WIDTH + 2 * D_MODEL
EPS = 1e-6
LOG2E = 1.4426950408889634

VMEM_LIMIT_BYTES = 56 * 1024 * 1024
LANES = 128
COL_TILE = 512
TOKEN_TILE = 512
PIECE = SSM_GROUP
PIECES = LANES // PIECE
N_SLABS = ATTN_WIDTH // LANES
STATE_ROWS = SSM_GROUPS // 2
STAGE = 4


def _params(*semantics):
    return pltpu.CompilerParams(dimension_semantics=semantics,
                                vmem_limit_bytes=VMEM_LIMIT_BYTES)


def _resident(shape):
    return pl.BlockSpec(shape, lambda *_: (0,) * len(shape),
                        pipeline_mode=pl.Buffered(1))


def _layer(shape, l):
    return pl.BlockSpec((None,) + shape, lambda *_: (l,) + (0,) * len(shape),
                        pipeline_mode=pl.Buffered(1))


def _rms_norm(x, gain):
    ms = jnp.mean(x * x, axis=-1, keepdims=True)
    return x * lax.rsqrt(ms + EPS) * gain


def _piece_transpose(vs):
    vs = list(vs)
    piece = lax.broadcasted_iota(jnp.int32, vs[0].shape, 1) // PIECE
    for k in (4, 2, 1):
        upper = (piece & k) != 0
        for j in range(PIECES):
            if j & k:
                continue
            a, b = vs[j], vs[j + k]
            vs[j] = jnp.where(upper, pltpu.roll(b, PIECE * k, 1), a)
            vs[j + k] = jnp.where(upper, b, pltpu.roll(a, LANES - PIECE * k, 1))
    return vs


def _in_proj_kernel(x_ref, g_ref, w_ref, gq_ref, gk_ref, bd_ref, *refs):
    qkv_refs = refs[:3 * N_GROUPS]
    xs_ref, gate_ref, h_ref, slab_ref, quad_ref = refs[3 * N_GROUPS:]
    tm = x_ref.shape[0]
    n_slabs = D_MODEL // LANES

    h = _rms_norm(x_ref[...], g_ref[...])
    h_ref[0] = h.astype(BF16)
    for c in range(n_slabs):
        slab_ref[c] = h[:, c * LANES:(c + 1) * LANES]
    for c in range(n_slabs):
        cols = slice(c * LANES, (c + 1) * LANES)
        for rho in range(STAGE):
            quarter = slab_ref[c, pl.ds(rho, tm // STAGE, stride=STAGE), :]
            quad_ref[c, rho] = quarter
            h_ref[1, rho * (tm // STAGE):(rho + 1) * (tm // STAGE), cols] = quarter.astype(BF16)
    for c in range(n_slabs):
        cols = slice(c * LANES, (c + 1) * LANES)
        for r in range(CHUNK):
            rows = quad_ref[c, r % STAGE, pl.ds(r // STAGE, tm // CHUNK, stride=STAGE), :]
            h_ref[2, r * (tm // CHUNK):(r + 1) * (tm // CHUNK), cols] = rows.astype(BF16)
    order = {1: 0, STAGE: 1, STAGE * STAGE: 2}

    def proj(j, d=1):
        return jnp.dot(h_ref[order[d]], w_ref[:, j * COL_TILE:(j + 1) * COL_TILE],
                       preferred_element_type=F32)

    def head_norm(z, gain):
        zz = (z * z).astype(BF16)
        half = COL_TILE // 2
        ss = jnp.concatenate(
            [jnp.dot(zz[:, :half], bd_ref[...], preferred_element_type=F32),
             jnp.dot(zz[:, half:], bd_ref[...], preferred_element_type=F32)], axis=1)
        return z * lax.rsqrt(ss * (1.0 / HEAD_DIM) + EPS) * gain

    def emit(z, out_ref, d):
        for r in range(d):
            out_ref[r] = z[r * (tm // d):(r + 1) * (tm // d)].astype(BF16)

    for g, d in enumerate(DILATIONS):
        emit(head_norm(proj(g, d), gq_ref[...]), qkv_refs[g], d)
        emit(head_norm(proj(N_GROUPS + g, d), gk_ref[...]), qkv_refs[N_GROUPS + g], d)
        emit(proj(2 * N_GROUPS + g, d), qkv_refs[2 * N_GROUPS + g], d)

    u = proj(3 * N_GROUPS, CHUNK)
    per = tm // CHUNK
    for a in range(N_SLABS):
        for half in range(CHUNK // PIECES):
            vs = [u[(half * PIECES + j) * per:(half * PIECES + j + 1) * per,
                    a * LANES:(a + 1) * LANES] for j in range(PIECES)]
            for i, w in enumerate(_piece_transpose(vs)):
                xs_ref[a * PIECES + i, :, half * LANES:(half + 1) * LANES] = w.astype(BF16)

    for j in range(2 * D_MODEL // COL_TILE):
        cols = slice(j * COL_TILE, (j + 1) * COL_TILE)
        gate_ref[:, cols] = jax.nn.sigmoid(proj(3 * N_GROUPS + 1 + j)).astype(BF16)


def _in_proj(x, g_mix, w_in, gq, gk, bd, layer, batch, seq):
    n = x.shape[0]
    tm = TOKEN_TILE
    tiles = seq // tm
    row = lambda i: (i, 0)
    dil_specs, dil_shapes = [], []
    for _ in range(3):
        for d in DILATIONS:
            dil_specs.append(pl.BlockSpec((None, d, tm // d, ATTN_WIDTH),
                                          lambda i: (i // tiles, 0, i % tiles, 0)))
            dil_shapes.append(jax.ShapeDtypeStruct((batch, d, seq // d, ATTN_WIDTH), BF16))
    outs = pl.pallas_call(
        _in_proj_kernel,
        grid=(n // tm,),
        in_specs=[pl.BlockSpec((tm, D_MODEL), row),
                  _layer((1, D_MODEL), layer),
                  _layer((D_MODEL, IN_COLS), layer),
                  _layer((1, COL_TILE), layer),
                  _layer((1, COL_TILE), layer),
                  _resident((COL_TILE // 2, COL_TILE // 2))],
        out_specs=dil_specs + [
            pl.BlockSpec((SSM_GROUPS, tm // CHUNK, CHUNK_COLS), lambda i: (0, i, 0)),
            pl.BlockSpec((tm, 2 * D_MODEL), row)],
        out_shape=dil_shapes + [
            jax.ShapeDtypeStruct((SSM_GROUPS, n // CHUNK, CHUNK_COLS), BF16),
            jax.ShapeDtypeStruct((n, 2 * D_MODEL), BF16)],
        scratch_shapes=[pltpu.VMEM((3, tm, D_MODEL), BF16),
                        pltpu.VMEM((D_MODEL // LANES, tm, LANES), F32),
                        pltpu.VMEM((D_MODEL // LANES, STAGE, tm // STAGE, LANES), F32)],
        compiler_params=_params("arbitrary"),
        name="in_proj",
    )(x, g_mix, w_in, gq, gk, bd)
    return outs[0:3], outs[3:6], outs[6:9], outs[9], outs[10]


ATTN_BLOCKS = {1: 8, 4: 8, 16: 1}
ATTN_RESIDUES = {1: 1, 4: 1, 16: 16}


def _attn_kernel(q_ref, k_ref, v_ref, *refs, dilation, nb, nr, merge):
    if merge:
        other = (refs[0], refs[1]), (refs[2], refs[3])
        a_ref, kbuf, vbuf = refs[4:]
        stage_refs = ()
    else:
        o_ref, lse_ref, kbuf, vbuf = refs[:4]
        stage_refs = refs[4:]
    step = pl.program_id(1)
    last = slice((nb - 1) * BLK, nb * BLK)

    qi = lax.broadcasted_iota(jnp.int32, (BLK, 2 * BLK), 0)
    kj = lax.broadcasted_iota(jnp.int32, (BLK, 2 * BLK), 1)
    neg = jnp.full((BLK, 2 * BLK), -jnp.inf, F32)
    zero = jnp.zeros((BLK, 2 * BLK), F32)
    cur_bias = jnp.where(kj - BLK <= qi, zero, neg)
    band = jnp.where(kj < BLK, jnp.where(kj >= qi, zero, neg), cur_bias)
    head = jnp.where(kj < BLK, jnp.where(step > 0, band, neg), cur_bias)
    band = jnp.concatenate([band, band], axis=0)
    head = jnp.concatenate([head, head], axis=0)

    lane_q = lax.broadcasted_iota(jnp.int32, (BLK, LANES), 1) < HEAD_DIM
    lane_kv = lax.broadcasted_iota(jnp.int32, (2 * BLK, LANES), 1) < HEAD_DIM
    head_of_row = lax.broadcasted_iota(jnp.int32, (4 * BLK, LANES), 0) // (2 * BLK)
    head_of_lane = lax.broadcasted_iota(jnp.int32, (4 * BLK, LANES), 1) // HEAD_DIM
    den_cols = jnp.where(head_of_row == head_of_lane, 1.0, 0.0).astype(BF16)

    residues = [rr if nr == dilation else pl.program_id(2) * nr + rr for rr in range(nr)]

    @pl.when(step == 0)
    def _():
        for r in residues:
            kbuf[r, 0:BLK, :] = jnp.zeros((BLK, ATTN_WIDTH), BF16)
            vbuf[r, 0:BLK, :] = jnp.zeros((BLK, ATTN_WIDTH), BF16)

    for rr, r in enumerate(residues):
        kb = kbuf.at[r]
        vb = vbuf.at[r]
        kb[BLK:2 * BLK, :] = k_ref[rr, 0:BLK, :]
        vb[BLK:2 * BLK, :] = v_ref[rr, 0:BLK, :]

        for j in range(nb):
            for p in range(HEADS // 2):
                cols = slice(p * LANES, (p + 1) * LANES)
                if j == 0:
                    keys, vals, bias = kb[:, cols], vb[:, cols], head
                else:
                    window = slice((j - 1) * BLK, (j + 1) * BLK)
                    keys, vals, bias = k_ref[rr, window, cols], v_ref[rr, window, cols], band
                qp = q_ref[rr, j * BLK:(j + 1) * BLK, cols]
                zq = jnp.zeros_like(qp)
                q2 = jnp.concatenate([jnp.where(lane_q, qp, zq), jnp.where(lane_q, zq, qp)],
                                     axis=0)
                s = lax.dot_general(q2, keys, (((1,), (1,)), ((), ())),
                                    preferred_element_type=F32) + bias
                m = jnp.max(s, axis=-1, keepdims=True)
                eb = jnp.exp2(s - m).astype(BF16)
                zv = jnp.zeros_like(vals)
                v2 = jnp.concatenate([jnp.where(lane_kv, vals, zv),
                                      jnp.where(lane_kv, zv, vals)], axis=0)
                acc = jnp.dot(jnp.concatenate([eb[:BLK], eb[BLK:]], axis=1),
                              jnp.concatenate([v2, den_cols], axis=1),
                              preferred_element_type=F32)
                den_lanes = acc[:, LANES:]
                m_lanes = jnp.where(lane_q, m[:BLK], m[BLK:])
                lse = m_lanes + jnp.log2(den_lanes)
                o = acc[:, :LANES] * (1.0 / den_lanes)
                if merge:
                    rows = slice(j * BLK, (j + 1) * BLK)
                    (o_a, l_a), (o_b, l_b) = [(o_r[p, rows, :], l_r[p, rows, :])
                                              for o_r, l_r in other]
                    top = jnp.maximum(jnp.maximum(lse, l_a), l_b)
                    e0, e1, e2 = jnp.exp2(lse - top), jnp.exp2(l_a - top), jnp.exp2(l_b - top)
                    a_ref[rows, cols] = ((e0 * o + e1 * o_a + e2 * o_b)
                                         * (1.0 / (e0 + e1 + e2))).astype(BF16)
                elif stage_refs:
                    rows = pl.ds(r // STAGE, BLK, stride=STAGE)
                    stage_refs[0][p, r % STAGE, rows, :] = o
                    stage_refs[1][p, r % STAGE, rows, :] = lse
                else:
                    rows = pl.ds(j * BLK * dilation + r, BLK, stride=dilation)
                    o_ref[p, rows, :] = o
                    lse_ref[p, rows, :] = lse

        kb[0:BLK, :] = k_ref[rr, last, :]
        vb[0:BLK, :] = v_ref[rr, last, :]

    if not merge and stage_refs:
        for staged, out_ref in zip(stage_refs, (o_ref, lse_ref)):
            for p in range(HEADS // 2):
                for rho in range(STAGE):
                    out_ref[p, pl.ds(rho, STAGE * BLK, stride=STAGE), :] = staged[p, rho]


def _attention_group(q, k, v, batch, seq, group, others=None):
    d = DILATIONS[group]
    nb, nr = ATTN_BLOCKS[d], ATTN_RESIDUES[d]
    steps = seq // d // (BLK * nb)
    merge = others is not None
    two_pass = d == STAGE * STAGE and nr == d and nb == 1
    staging = [pltpu.VMEM((N_SLABS, STAGE, STAGE * BLK, LANES), F32)] * 2 if two_pass else []
    in_spec = pl.BlockSpec((None, nr, BLK * nb, ATTN_WIDTH), lambda b, n, r: (b, r, n, 0))
    slab_spec = pl.BlockSpec((None, N_SLABS, BLK * nb * d, LANES), lambda b, n, r: (b, 0, n, 0))
    slab_shape = jax.ShapeDtypeStruct((batch, N_SLABS, seq, LANES), F32)
    if merge:
        assert d == 1
        extra = [t for pair in others for t in pair]
        out_specs = pl.BlockSpec((BLK * nb, ATTN_WIDTH), lambda b, n, r: (b * steps + n, 0))
        out_shape = jax.ShapeDtypeStruct((batch * seq, ATTN_WIDTH), BF16)
    else:
        extra = []
        out_specs = [slab_spec, slab_spec]
        out_shape = [slab_shape, slab_shape]
    return pl.pallas_call(
        functools.partial(_attn_kernel, dilation=d, nb=nb, nr=nr, merge=merge),
        grid=(batch, steps, d // nr),
        in_specs=[in_spec, in_spec, in_spec] + [slab_spec] * len(extra),
        out_specs=out_specs,
        out_shape=out_shape,
        scratch_shapes=[pltpu.VMEM((d, 2 * BLK, ATTN_WIDTH), BF16),
                        pltpu.VMEM((d, 2 * BLK, ATTN_WIDTH), BF16)] + staging,
        compiler_params=_params("arbitrary", "arbitrary", "arbitrary"),
        name=f"attn_d{d}",
    )(q, k, v, *extra)


def _ssm_tables(lam_re, lam_im, log_dt, b_re, b_im, c_re, c_im):
    hi = lax.Precision.HIGHEST
    depth = lam_re.shape[0]
    lr = lam_re.astype(F32)
    li = lam_im.astype(F32)
    dt = jnp.exp(log_dt.astype(F32))[..., None]
    mag = jnp.exp(lr * dt)
    ang = li * dt
    abar_re = mag * jnp.cos(ang)
    abar_im = mag * jnp.sin(ang)
    nr = abar_re - 1.0
    ni = abar_im
    den = lr * lr + li * li
    cr = ((nr * lr + ni * li) / den)[:, :, None, :]
    ci = ((ni * lr - nr * li) / den)[:, :, None, :]
    brt = b_re.astype(F32).transpose(0, 1, 3, 2)
    bit = b_im.astype(F32).transpose(0, 1, 3, 2)
    bbar_re = cr * brt - ci * bit
    bbar_im = cr * bit + ci * brt

    def powers(tau):
        tau = tau.astype(F32)[None, None, :, None]
        pmag = jnp.exp((lr * dt)[:, :, None, :] * tau)
        pang = ang[:, :, None, :] * tau
        return pmag * jnp.cos(pang), pmag * jnp.sin(pang)

    lane = jnp.arange(CHUNK_COLS)
    idx = jnp.arange(CHUNK)[:, None]
    by_step = (lane[None, :] // SSM_GROUP == idx).astype(F32)
    by_chan = (lane[None, :] % SSM_GROUP == idx).astype(F32)
    spread = lambda m, onehot: jnp.einsum('dgkp,kl->dgpl', m, onehot, precision=hi)

    def table(tau, m_re, m_im):
        p_re, p_im = powers(tau)
        pr, pi = spread(p_re, by_step), spread(p_im, by_step)
        mr, mi = spread(m_re, by_chan), spread(m_im, by_chan)
        return mr * pr - mi * pi, mr * pi + mi * pr

    steps = jnp.arange(CHUNK)
    c_re, c_im = c_re.astype(F32), c_im.astype(F32)
    cp_re, cp_im = table(steps, c_re, c_im)
    out_re, out_im = table(steps + 1, c_re, c_im)
    in_re, in_im = table(CHUNK - 1 - steps, bbar_re, bbar_im)

    lag = (jnp.einsum('dgcp,dgpl->dgcl', bbar_re, cp_re, precision=hi)
           - jnp.einsum('dgcp,dgpl->dgcl', bbar_im, cp_im, precision=hi))
    shift = (lane[None, None, :] == lane[None, :, None]
             + PIECE * jnp.arange(CHUNK)[:, None, None]).astype(BF16)
    toep = jnp.einsum('dgcl,slm->dgscm', lag.astype(BF16), shift)
    toep = toep.reshape(depth, SSM_GROUPS, CHUNK_COLS, CHUNK_COLS)

    parity = (jnp.arange(SSM_GROUPS) % 2)[None, :, None, None, None, None]
    slot = jnp.arange(2)[None, None, None, :, None, None]

    def place(re, im):
        both = jnp.stack([re, im], axis=2)[:, :, :, None]
        placed = jnp.where(slot == parity, both, 0.0)
        return placed.reshape(depth, SSM_GROUPS, CHUNK_COLS, CHUNK_COLS).astype(BF16)

    w_in = place(in_re, in_im)
    w_out = place(out_re, -out_im)
    a_re, a_im = abar_re, abar_im
    for _ in range(CHUNK.bit_length() - 1):
        a_re, a_im = a_re * a_re - a_im * a_im, 2.0 * a_re * a_im
    a_re = a_re.reshape(depth, STATE_ROWS, LANES)
    a_im = a_im.reshape(depth, STATE_ROWS, LANES)
    return toep, w_in, w_out, a_re, a_im


SSM_PAIRS = 2


def _ssm_kernel(x_ref, win_ref, toep_ref, wout_ref, are_ref, aim_ref, d_ref, y_ref,
                sre_ref, sim_ref, *, batch, per_batch):
    phase = pl.program_id(0)
    step = pl.program_id(1)
    nchunk = batch * per_batch
    pair_rows = [pl.ds(step * SSM_PAIRS + i, nchunk, stride=STATE_ROWS)
                 for i in range(SSM_PAIRS)]
    nt = (((1,), (1,)), ((), ()))

    @pl.when(phase == 0)
    def _():
        for i, rows in enumerate(pair_rows):
            v = (lax.dot_general(x_ref[2 * i], win_ref[2 * i], nt, preferred_element_type=F32)
                 + lax.dot_general(x_ref[2 * i + 1], win_ref[2 * i + 1], nt,
                                   preferred_element_type=F32))
            sre_ref[rows, :] = v[:, :LANES]
            sim_ref[rows, :] = v[:, LANES:]

    @pl.when((phase == 0) & (step == pl.num_programs(1) - 1))
    def _():
        ar = are_ref[...]
        ai = aim_ref[...]

        def body(k, carry):
            new = []
            for b in range(batch):
                sre, sim = carry[b]
                rows = pl.ds(pl.multiple_of((b * per_batch + k) * STATE_ROWS, STATE_ROWS),
                             STATE_ROWS)
                vre = sre_ref[rows, :]
                vim = sim_ref[rows, :]
                sre_ref[rows, :] = sre
                sim_ref[rows, :] = sim
                new.append((ar * sre - ai * sim + vre, ar * sim + ai * sre + vim))
            return tuple(new)

        zero = jnp.zeros((STATE_ROWS, LANES), F32)
        lax.fori_loop(0, per_batch, body, tuple((zero, zero) for _ in range(batch)))

    @pl.when(phase == 1)
    def _():
        for i, rows in enumerate(pair_rows):
            sp = jnp.concatenate([sre_ref[rows, :], sim_ref[rows, :]], axis=1).astype(BF16)
            for h in range(2 * i, 2 * i + 2):
                x = x_ref[h]
                y = (jnp.dot(x, toep_ref[h], preferred_element_type=F32)
                     + jnp.dot(sp, wout_ref[h], preferred_element_type=F32)
                     + d_ref[h] * x.astype(F32))
                y_ref[h] = jax.nn.gelu(y).astype(BF16)


def _ssm(x, toep, w_in, w_out, a_re, a_im, d_tiled, layer, batch, seq):
    nchunk = x.shape[1]
    per_batch = seq // CHUNK
    groups = 2 * SSM_PAIRS
    table = lambda shape: pl.BlockSpec((None, groups) + shape, lambda ph, q: (layer, q, 0, 0))
    return pl.pallas_call(
        functools.partial(_ssm_kernel, batch=batch, per_batch=per_batch),
        grid=(2, SSM_GROUPS // groups),
        in_specs=[pl.BlockSpec((groups, nchunk, CHUNK_COLS), lambda ph, q: (q, 0, 0)),
                  table((CHUNK_COLS, CHUNK_COLS)), table((CHUNK_COLS, CHUNK_COLS)),
                  table((CHUNK_COLS, CHUNK_COLS)),
                  _layer((STATE_ROWS, LANES), layer), _layer((STATE_ROWS, LANES), layer),
                  table((1, CHUNK_COLS))],
        out_specs=pl.BlockSpec((groups, nchunk, CHUNK_COLS), lambda ph, q: (q * ph, 0, 0)),
        out_shape=jax.ShapeDtypeStruct((SSM_GROUPS, nchunk, CHUNK_COLS), BF16),
        scratch_shapes=[pltpu.VMEM((nchunk * STATE_ROWS, LANES), F32),
                        pltpu.VMEM((nchunk * STATE_ROWS, LANES), F32)],
        compiler_params=_params("arbitrary", "arbitrary"),
        name="ssm",
    )(x, w_in, toep, w_out, a_re, a_im, d_tiled)


FF_CHUNK = D_FF


def _mix_ffn_kernel(x_ref, a_ref, ys_ref, gate_ref, wap_ref, wa_ref, wb_ref, wout_ref,
                    g_ref, wg_ref, wu_ref, wd_ref, out_ref, slab_ref):
    tm = x_ref.shape[0]
    a_out = jnp.dot(a_ref[...], wap_ref[...], preferred_element_type=F32)

    for s in range(N_SLABS):
        for half in range(CHUNK // PIECES):
            ws = [ys_ref[s * PIECES + i, :, half * LANES:(half + 1) * LANES].astype(F32)
                  for i in range(PIECES)]
            for j, v in enumerate(_piece_transpose(ws)):
                slab_ref[s, pl.ds(half * PIECES + j, tm // CHUNK, stride=CHUNK), :] = v
    y = jnp.concatenate([slab_ref[c] for c in range(N_SLABS)], axis=1).astype(BF16)
    s_out = (jnp.dot(y, wa_ref[...], preferred_element_type=F32)
             * jax.nn.sigmoid(jnp.dot(y, wb_ref[...], preferred_element_type=F32)))
    mix = (gate_ref[:, :D_MODEL].astype(F32) * a_out
           + gate_ref[:, D_MODEL:].astype(F32) * s_out)
    x = x_ref[...] + jnp.dot(mix.astype(BF16), wout_ref[...], preferred_element_type=F32)

    h = _rms_norm(x, g_ref[...]).astype(BF16)
    for c in range(D_FF // FF_CHUNK):
        cols = slice(c * FF_CHUNK, (c + 1) * FF_CHUNK)
        gate = jnp.dot(h, wg_ref[:, cols], preferred_element_type=F32)
        up = jnp.dot(h, wu_ref[:, cols], preferred_element_type=F32)
        hidden = (jax.nn.silu(gate) * up).astype(BF16)
        x = x + jnp.dot(hidden, wd_ref[cols, :], preferred_element_type=F32)
    out_ref[...] = x


def _mix_ffn(x, a, ys, gates, wap, wa, wb, wout, g_ffn, wg, wu, wd, layer):
    n = x.shape[0]
    tm = TOKEN_TILE
    row = lambda i: (i, 0)
    return pl.pallas_call(
        _mix_ffn_kernel,
        grid=(n // tm,),
        in_specs=[pl.BlockSpec((tm, D_MODEL), row),
                  pl.BlockSpec((tm, ATTN_WIDTH), row),
                  pl.BlockSpec((SSM_GROUPS, tm // CHUNK, CHUNK_COLS), lambda i: (0, i, 0)),
                  pl.BlockSpec((tm, 2 * D_MODEL), row),
                  _layer((ATTN_WIDTH, D_MODEL), layer), _layer((SSM_WIDTH, D_MODEL), layer),
                  _layer((SSM_WIDTH, D_MODEL), layer), _layer((D_MODEL, D_MODEL), layer),
                  _layer((1, D_MODEL), layer), _layer((D_MODEL, D_FF), layer),
                  _layer((D_MODEL, D_FF), layer), _layer((D_FF, D_MODEL), layer)],
        out_specs=pl.BlockSpec((tm, D_MODEL), row),
        out_shape=jax.ShapeDtypeStruct((n, D_MODEL), F32),
        scratch_shapes=[pltpu.VMEM((N_SLABS, tm, LANES), F32)],
        compiler_params=_params("arbitrary"),
        name="mix_ffn",
    )(x, a, ys, gates, wap, wa, wb, wout, g_ffn, wg, wu, wd)


def kernel(x, g_mix, w_in, g_q, g_k, w_attn_proj, lambda_re, lambda_im, log_dt, b_re, b_im,
           c_re, c_im, d_skip, w_glu_a, w_glu_b, w_out, g_ffn, w_ffn_gate, w_ffn_up, w_ffn_down):
    batch, seq, _ = x.shape
    depth = w_in.shape[0]
    n = batch * seq
    xf = x.reshape(n, D_MODEL).astype(F32)

    blk = jnp.arange(COL_TILE // 2) // HEAD_DIM
    bd = (blk[:, None] == blk[None, :]).astype(BF16)
    gq = jnp.tile(g_q.astype(F32) * (LOG2E * HEAD_DIM ** -0.5), (1, HEADS))[:, None]
    gk = jnp.tile(g_k.astype(F32), (1, HEADS))[:, None]
    toep, s_in, s_out, a_re, a_im = _ssm_tables(lambda_re, lambda_im, log_dt, b_re, b_im,
                                                c_re, c_im)
    d_tiled = jnp.tile(d_skip.astype(F32).reshape(depth, SSM_GROUPS, 1, SSM_GROUP),
                       (1, 1, 1, CHUNK))

    row3 = lambda g: g.astype(F32)[:, None, :]
    g_mix, g_ffn = row3(g_mix), row3(g_ffn)
    (w_in, w_attn_proj, w_glu_a, w_glu_b, w_out, w_ffn_gate, w_ffn_up, w_ffn_down) = (
        w.astype(BF16) for w in (w_in, w_attn_proj, w_glu_a, w_glu_b, w_out,
                                 w_ffn_gate, w_ffn_up, w_ffn_down))

    for l in range(depth):
        q, k, v, xs, gates = _in_proj(xf, g_mix, w_in, gq, gk, bd, l, batch, seq)
        dilated = [_attention_group(q[g], k[g], v[g], batch, seq, g)
                   for g in range(1, N_GROUPS)]
        a = _attention_group(q[0], k[0], v[0], batch, seq, 0, others=dilated)
        ys = _ssm(xs, toep, s_in, s_out, a_re, a_im, d_tiled, l, batch, seq)
        xf = _mix_ffn(xf, a, ys, gates, w_attn_proj, w_glu_a, w_glu_b, w_out, g_ffn,
                      w_ffn_gate, w_ffn_up, w_ffn_down, l)
    return xf.reshape(batch, seq, D_MODEL).astype(x.dtype)
```

```python
import functools

import jax
import jax.numpy as jnp
from jax import lax
from jax.experimental import pallas as pl
from jax.experimental.pallas import tpu as pltpu

F32 = jnp.float32
BF16 = jnp.bfloat16

D_MODEL = 1024
HEAD_DIM = 64
HEADS = 8
ATTN_WIDTH = HEADS * HEAD_DIM
DILATIONS = (1, 4, 16)
N_GROUPS = len(DILATIONS)
BLK = 128
SSM_WIDTH = 512
SSM_GROUP = 16
SSM_GROUPS = 32
SSM_STATE = 64
CHUNK = 16
CHUNK_COLS = CHUNK * SSM_GROUP
D_FF = 2816
IN_COLS = 3 * N_GROUPS * ATTN_WIDTH + SSM_WIDTH + 2 * D_MODEL
EPS = 1e-6
LOG2E = 1.4426950408889634

VMEM_LIMIT_BYTES = 56 * 1024 * 1024
LANES = 128
COL_TILE = 512
TOKEN_TILE = 512
PIECE = SSM_GROUP
PIECES = LANES // PIECE
N_SLABS = ATTN_WIDTH // LANES
STATE_ROWS = SSM_GROUPS // 2
STAGE = 4


def _params(*semantics):
    return pltpu.CompilerParams(dimension_semantics=semantics,
                                vmem_limit_bytes=VMEM_LIMIT_BYTES)


def _resident(shape):
    return pl.BlockSpec(shape, lambda *_: (0,) * len(shape),
                        pipeline_mode=pl.Buffered(1))


def _layer(shape, l):
    return pl.BlockSpec((None,) + shape, lambda *_: (l,) + (0,) * len(shape),
                        pipeline_mode=pl.Buffered(1))


def _rms_norm(x, gain):
    ms = jnp.mean(x * x, axis=-1, keepdims=True)
    return x * lax.rsqrt(ms + EPS) * gain


def _piece_transpose(vs):
    vs = list(vs)
    piece = lax.broadcasted_iota(jnp.int32, vs[0].shape, 1) // PIECE
    for k in (4, 2, 1):
        upper = (piece & k) != 0
        for j in range(PIECES):
            if j & k:
                continue
            a, b = vs[j], vs[j + k]
            vs[j] = jnp.where(upper, pltpu.roll(b, PIECE * k, 1), a)
            vs[j + k] = jnp.where(upper, b, pltpu.roll(a, LANES - PIECE * k, 1))
    return vs


def _in_proj_kernel(x_ref, g_ref, w_ref, gq_ref, gk_ref, bd_ref, *refs):
    qkv_refs = refs[:3 * N_GROUPS]
    xs_ref, gate_ref, h_ref, slab_ref, quad_ref = refs[3 * N_GROUPS:]
    tm = x_ref.shape[0]
    n_slabs = D_MODEL // LANES

    h = _rms_norm(x_ref[...], g_ref[...])
    h_ref[0] = h.astype(BF16)
    for c in range(n_slabs):
        slab_ref[c] = h[:, c * LANES:(c + 1) * LANES]
    for c in range(n_slabs):
        cols = slice(c * LANES, (c + 1) * LANES)
        for rho in range(STAGE):
            quarter = slab_ref[c, pl.ds(rho, tm // STAGE, stride=STAGE), :]
            quad_ref[c, rho] = quarter
            h_ref[1, rho * (tm // STAGE):(rho + 1) * (tm // STAGE), cols] = quarter.astype(BF16)
    for c in range(n_slabs):
        cols = slice(c * LANES, (c + 1) * LANES)
        for r in range(CHUNK):
            rows = quad_ref[c, r % STAGE, pl.ds(r // STAGE, tm // CHUNK, stride=STAGE), :]
            h_ref[2, r * (tm // CHUNK):(r + 1) * (tm // CHUNK), cols] = rows.astype(BF16)
    order = {1: 0, STAGE: 1, STAGE * STAGE: 2}

    def proj(j, d=1):
        return jnp.dot(h_ref[order[d]], w_ref[:, j * COL_TILE:(j + 1) * COL_TILE],
                       preferred_element_type=F32)

    def head_norm(z, gain):
        zz = (z * z).astype(BF16)
        half = COL_TILE // 2
        ss = jnp.concatenate(
            [jnp.dot(zz[:, :half], bd_ref[...], preferred_element_type=F32),
             jnp.dot(zz[:, half:], bd_ref[...], preferred_element_type=F32)], axis=1)
        return z * lax.rsqrt(ss * (1.0 / HEAD_DIM) + EPS) * gain

    def emit(z, out_ref, d):
        for r in range(d):
            out_ref[r] = z[r * (tm // d):(r + 1) * (tm // d)].astype(BF16)

    for g, d in enumerate(DILATIONS):
        emit(head_norm(proj(g, d), gq_ref[...]), qkv_refs[g], d)
        emit(head_norm(proj(N_GROUPS + g, d), gk_ref[...]), qkv_refs[N_GROUPS + g], d)
        emit(proj(2 * N_GROUPS + g, d), qkv_refs[2 * N_GROUPS + g], d)

    u = proj(3 * N_GROUPS, CHUNK)
    per = tm // CHUNK
    for a in range(N_SLABS):
        for half in range(CHUNK // PIECES):
            vs = [u[(half * PIECES + j) * per:(half * PIECES + j + 1) * per,
                    a * LANES:(a + 1) * LANES] for j in range(PIECES)]
            for i, w in enumerate(_piece_transpose(vs)):
                xs_ref[a * PIECES + i, :, half * LANES:(half + 1) * LANES] = w.astype(BF16)

    for j in range(2 * D_MODEL // COL_TILE):
        cols = slice(j * COL_TILE, (j + 1) * COL_TILE)
        gate_ref[:, cols] = jax.nn.sigmoid(proj(3 * N_GROUPS + 1 + j)).astype(BF16)


def _in_proj(x, g_mix, w_in, gq, gk, bd, layer, batch, seq):
    n = x.shape[0]
    tm = TOKEN_TILE
    tiles = seq // tm
    row = lambda i: (i, 0)
    dil_specs, dil_shapes = [], []
    for _ in range(3):
        for d in DILATIONS:
            dil_specs.append(pl.BlockSpec((None, d, tm // d, ATTN_WIDTH),
                                          lambda i: (i // tiles, 0, i % tiles, 0)))
            dil_shapes.append(jax.ShapeDtypeStruct((batch, d, seq // d, ATTN_WIDTH), BF16))
    outs = pl.pallas_call(
        _in_proj_kernel,
        grid=(n // tm,),
        in_specs=[pl.BlockSpec((tm, D_MODEL), row),
                  _layer((1, D_MODEL), layer),
                  _layer((D_MODEL, IN_COLS), layer),
                  _layer((1, COL_TILE), layer),
                  _layer((1, COL_TILE), layer),
                  _resident((COL_TILE // 2, COL_TILE // 2))],
        out_specs=dil_specs + [
            pl.BlockSpec((SSM_GROUPS, tm // CHUNK, CHUNK_COLS), lambda i: (0, i, 0)),
            pl.BlockSpec((tm, 2 * D_MODEL), row)],
        out_shape=dil_shapes + [
            jax.ShapeDtypeStruct((SSM_GROUPS, n // CHUNK, CHUNK_COLS), BF16),
            jax.ShapeDtypeStruct((n, 2 * D_MODEL), BF16)],
        scratch_shapes=[pltpu.VMEM((3, tm, D_MODEL), BF16),
                        pltpu.VMEM((D_MODEL // LANES, tm, LANES), F32),
                        pltpu.VMEM((D_MODEL // LANES, STAGE, tm // STAGE, LANES), F32)],
        compiler_params=_params("arbitrary"),
        name="in_proj",
    )(x, g_mix, w_in, gq, gk, bd)
    return outs[0:3], outs[3:6], outs[6:9], outs[9], outs[10]


ATTN_BLOCKS = {1: 8, 4: 8, 16: 1}
ATTN_RESIDUES = {1: 1, 4: 1, 16: 16}


def _attn_kernel(q_ref, k_ref, v_ref, *refs, dilation, nb, nr, merge):
    if merge:
        other = (refs[0], refs[1]), (refs[2], refs[3])
        a_ref, kbuf, vbuf = refs[4:]
        stage_refs = ()
    else:
        o_ref, lse_ref, kbuf, vbuf = refs[:4]
        stage_refs = refs[4:]
    step = pl.program_id(1)
    last = slice((nb - 1) * BLK, nb * BLK)

    qi = lax.broadcasted_iota(jnp.int32, (BLK, 2 * BLK), 0)
    kj = lax.broadcasted_iota(jnp.int32, (BLK, 2 * BLK), 1)
    neg = jnp.full((BLK, 2 * BLK), -jnp.inf, F32)
    zero = jnp.zeros((BLK, 2 * BLK), F32)
    cur_bias = jnp.where(kj - BLK <= qi, zero, neg)
    band = jnp.where(kj < BLK, jnp.where(kj >= qi, zero, neg), cur_bias)
    head = jnp.where(kj < BLK, jnp.where(step > 0, band, neg), cur_bias)
    band = jnp.concatenate([band, band], axis=0)
    head = jnp.concatenate([head, head], axis=0)

    lane_q = lax.broadcasted_iota(jnp.int32, (BLK, LANES), 1) < HEAD_DIM
    lane_kv = lax.broadcasted_iota(jnp.int32, (2 * BLK, LANES), 1) < HEAD_DIM
    head_of_row = lax.broadcasted_iota(jnp.int32, (4 * BLK, LANES), 0) // (2 * BLK)
    head_of_lane = lax.broadcasted_iota(jnp.int32, (4 * BLK, LANES), 1) // HEAD_DIM
    den_cols = jnp.where(head_of_row == head_of_lane, 1.0, 0.0).astype(BF16)

    residues = [rr if nr == dilation else pl.program_id(2) * nr + rr for rr in range(nr)]

    @pl.when(step == 0)
    def _():
        for r in residues:
            kbuf[r, 0:BLK, :] = jnp.zeros((BLK, ATTN_WIDTH), BF16)
            vbuf[r, 0:BLK, :] = jnp.zeros((BLK, ATTN_WIDTH), BF16)

    for rr, r in enumerate(residues):
        kb = kbuf.at[r]
        vb = vbuf.at[r]
        kb[BLK:2 * BLK, :] = k_ref[rr, 0:BLK, :]
        vb[BLK:2 * BLK, :] = v_ref[rr, 0:BLK, :]

        for j in range(nb):
            for p in range(HEADS // 2):
                cols = slice(p * LANES, (p + 1) * LANES)
                if j == 0:
                    keys, vals, bias = kb[:, cols], vb[:, cols], head
                else:
                    window = slice((j - 1) * BLK, (j + 1) * BLK)
                    keys, vals, bias = k_ref[rr, window, cols], v_ref[rr, window, cols], band
                qp = q_ref[rr, j * BLK:(j + 1) * BLK, cols]
                zq = jnp.zeros_like(qp)
                q2 = jnp.concatenate([jnp.where(lane_q, qp, zq), jnp.where(lane_q, zq, qp)],
                                     axis=0)
                s = lax.dot_general(q2, keys, (((1,), (1,)), ((), ())),
                                    preferred_element_type=F32) + bias
                m = jnp.max(s, axis=-1, keepdims=True)
                eb = jnp.exp2(s - m).astype(BF16)
                zv = jnp.zeros_like(vals)
                v2 = jnp.concatenate([jnp.where(lane_kv, vals, zv),
                                      jnp.where(lane_kv, zv, vals)], axis=0)
                acc = jnp.dot(jnp.concatenate([eb[:BLK], eb[BLK:]], axis=1),
                              jnp.concatenate([v2, den_cols], axis=1),
                              preferred_element_type=F32)
                den_lanes = acc[:, LANES:]
                m_lanes = jnp.where(lane_q, m[:BLK], m[BLK:])
                lse = m_lanes + jnp.log2(den_lanes)
                o = acc[:, :LANES] * (1.0 / den_lanes)
                if merge:
                    rows = slice(j * BLK, (j + 1) * BLK)
                    (o_a, l_a), (o_b, l_b) = [(o_r[p, rows, :], l_r[p, rows, :])
                                              for o_r, l_r in other]
                    top = jnp.maximum(jnp.maximum(lse, l_a), l_b)
                    e0, e1, e2 = jnp.exp2(lse - top), jnp.exp2(l_a - top), jnp.exp2(l_b - top)
                    a_ref[rows, cols] = ((e0 * o + e1 * o_a + e2 * o_b)
                                         * (1.0 / (e0 + e1 + e2))).astype(BF16)
                elif stage_refs:
                    rows = pl.ds(r // STAGE, BLK, stride=STAGE)
                    stage_refs[0][p, r % STAGE, rows, :] = o
                    stage_refs[1][p, r % STAGE, rows, :] = lse
                else:
                    rows = pl.ds(j * BLK * dilation + r, BLK, stride=dilation)
                    o_ref[p, rows, :] = o
                    lse_ref[p, rows, :] = lse

        kb[0:BLK, :] = k_ref[rr, last, :]
        vb[0:BLK, :] = v_ref[rr, last, :]

    if not merge and stage_refs:
        for staged, out_ref in zip(stage_refs, (o_ref, lse_ref)):
            for p in range(HEADS // 2):
                for rho in range(STAGE):
                    out_ref[p, pl.ds(rho, STAGE * BLK, stride=STAGE), :] = staged[p, rho]


def _attention_group(q, k, v, batch, seq, group, others=None):
    d = DILATIONS[group]
    nb, nr = ATTN_BLOCKS[d], ATTN_RESIDUES[d]
    steps = seq // d // (BLK * nb)
    merge = others is not None
    two_pass = d == STAGE * STAGE and nr == d and nb == 1
    staging = [pltpu.VMEM((N_SLABS, STAGE, STAGE * BLK, LANES), F32)] * 2 if two_pass else []
    in_spec = pl.BlockSpec((None, nr, BLK * nb, ATTN_WIDTH), lambda b, n, r: (b, r, n, 0))
    slab_spec = pl.BlockSpec((None, N_SLABS, BLK * nb * d, LANES), lambda b, n, r: (b, 0, n, 0))
    slab_shape = jax.ShapeDtypeStruct((batch, N_SLABS, seq, LANES), F32)
    if merge:
        assert d == 1
        extra = [t for pair in others for t in pair]
        out_specs = pl.BlockSpec((BLK * nb, ATTN_WIDTH), lambda b, n, r: (b * steps + n, 0))
        out_shape = jax.ShapeDtypeStruct((batch * seq, ATTN_WIDTH), BF16)
    else:
        extra = []
        out_specs = [slab_spec, slab_spec]
        out_shape = [slab_shape, slab_shape]
    return pl.pallas_call(
        functools.partial(_attn_kernel, dilation=d, nb=nb, nr=nr, merge=merge),
        grid=(batch, steps, d // nr),
        in_specs=[in_spec, in_spec, in_spec] + [slab_spec] * len(extra),
        out_specs=out_specs,
        out_shape=out_shape,
        scratch_shapes=[pltpu.VMEM((d, 2 * BLK, ATTN_WIDTH), BF16),
                        pltpu.VMEM((d, 2 * BLK, ATTN_WIDTH), BF16)] + staging,
        compiler_params=_params("arbitrary", "arbitrary", "arbitrary"),
        name=f"attn_d{d}",
    )(q, k, v, *extra)


def _ssm_tables(lam_re, lam_im, log_dt, b_re, b_im, c_re, c_im):
    hi = lax.Precision.HIGHEST
    depth = lam_re.shape[0]
    lr = lam_re.astype(F32)
    li = lam_im.astype(F32)
    dt = jnp.exp(log_dt.astype(F32))[..., None]
    mag = jnp.exp(lr * dt)
    ang = li * dt
    abar_re = mag * jnp.cos(ang)
    abar_im = mag * jnp.sin(ang)
    nr = abar_re - 1.0
    ni = abar_im
    den = lr * lr + li * li
    cr = ((nr * lr + ni * li) / den)[:, :, None, :]
    ci = ((ni * lr - nr * li) / den)[:, :, None, :]
    brt = b_re.astype(F32).transpose(0, 1, 3, 2)
    bit = b_im.astype(F32).transpose(0, 1, 3, 2)
    bbar_re = cr * brt - ci * bit
    bbar_im = cr * bit + ci * brt

    def powers(tau):
        tau = tau.astype(F32)[None, None, :, None]
        pmag = jnp.exp((lr * dt)[:, :, None, :] * tau)
        pang = ang[:, :, None, :] * tau
        return pmag * jnp.cos(pang), pmag * jnp.sin(pang)

    pw_re, pw_im = powers(jnp.arange(CHUNK + 1))

    wide = (CHUNK + 1) * SSM_GROUP
    lane = jnp.arange(wide)
    rep = (lane[None, :] // SSM_GROUP == jnp.arange(CHUNK + 1)[:, None]).astype(F32)
    til = (lane[None, :] % SSM_GROUP == jnp.arange(SSM_GROUP)[:, None]).astype(F32)
    pr_l = jnp.einsum('dgtp,tl->dgpl', pw_re, rep, precision=hi)
    pi_l = jnp.einsum('dgtp,tl->dgpl', pw_im, rep, precision=hi)
    cr_l = jnp.einsum('dgcp,cl->dgpl', c_re.astype(F32), til, precision=hi)
    ci_l = jnp.einsum('dgcp,cl->dgpl', c_im.astype(F32), til, precision=hi)
    cp_re = cr_l * pr_l - ci_l * pi_l
    cp_im = cr_l * pi_l + ci_l * pr_l

    lag = (jnp.einsum('dgcp,dgpl->dgcl', bbar_re, cp_re[..., :CHUNK_COLS], precision=hi)
           - jnp.einsum('dgcp,dgpl->dgcl', bbar_im, cp_im[..., :CHUNK_COLS], precision=hi))
    col = jnp.arange(CHUNK_COLS)
    shift = (col[None, None, :] == col[None, :, None]
             + PIECE * jnp.arange(CHUNK)[:, None, None]).astype(BF16)
    toep = jnp.einsum('dgcl,slm->dgscm', lag.astype(BF16), shift)
    toep = toep.reshape(depth, SSM_GROUPS, CHUNK_COLS, CHUNK_COLS)

    rev_re, rev_im = powers(CHUNK - 1 - jnp.arange(CHUNK))
    rev_re = rev_re[:, :, :, None, :]
    rev_im = rev_im[:, :, :, None, :]
    win_re = (rev_re * bbar_re[:, :, None] - rev_im * bbar_im[:, :, None])
    win_im = (rev_re * bbar_im[:, :, None] + rev_im * bbar_re[:, :, None])
    win_re = win_re.reshape(depth, SSM_GROUPS, CHUNK_COLS, SSM_STATE)
    win_im = win_im.reshape(depth, SSM_GROUPS, CHUNK_COLS, SSM_STATE)

    wout_re = cp_re[..., SSM_GROUP:]
    wout_im = -cp_im[..., SSM_GROUP:]

    odd = (jnp.arange(SSM_GROUPS) % 2 == 1)[None, :, None, None]
    zc = jnp.zeros_like(win_re)
    w_in = jnp.concatenate([jnp.where(odd, zc, win_re), jnp.where(odd, win_re, zc),
                            jnp.where(odd, zc, win_im), jnp.where(odd, win_im, zc)], axis=3)
    zr = jnp.zeros_like(wout_re)
    w_out = jnp.concatenate([jnp.where(odd, zr, wout_re), jnp.where(odd, wout_re, zr),
                             jnp.where(odd, zr, wout_im), jnp.where(odd, wout_im, zr)], axis=2)
    a_re, a_im = abar_re, abar_im
    for _ in range(CHUNK.bit_length() - 1):
        a_re, a_im = a_re * a_re - a_im * a_im, 2.0 * a_re * a_im
    a_re = a_re.reshape(depth, STATE_ROWS, LANES)
    a_im = a_im.reshape(depth, STATE_ROWS, LANES)
    return toep, w_in.astype(BF16), w_out.astype(BF16), a_re, a_im


SSM_PAIRS = 4


def _ssm_kernel(x_ref, win_ref, toep_ref, wout_ref, are_ref, aim_ref, d_ref, y_ref,
                sre_ref, sim_ref, *, batch, per_batch):
    phase = pl.program_id(0)
    step = pl.program_id(1)
    nchunk = batch * per_batch
    pair_rows = [pl.ds(step * SSM_PAIRS + i, nchunk, stride=STATE_ROWS)
                 for i in range(SSM_PAIRS)]

    @pl.when(phase == 0)
    def _():
        for i, rows in enumerate(pair_rows):
            v = (jnp.dot(x_ref[2 * i], win_ref[2 * i], preferred_element_type=F32)
                 + jnp.dot(x_ref[2 * i + 1], win_ref[2 * i + 1], preferred_element_type=F32))
            sre_ref[rows, :] = v[:, :LANES]
            sim_ref[rows, :] = v[:, LANES:]

    @pl.when((phase == 0) & (step == pl.num_programs(1) - 1))
    def _():
        ar = are_ref[...]
        ai = aim_ref[...]

        def body(k, carry):
            new = []
            for b in range(batch):
                sre, sim = carry[b]
                rows = pl.ds(pl.multiple_of((b * per_batch + k) * STATE_ROWS, STATE_ROWS),
                             STATE_ROWS)
                vre = sre_ref[rows, :]
                vim = sim_ref[rows, :]
                sre_ref[rows, :] = sre
                sim_ref[rows, :] = sim
                new.append((ar * sre - ai * sim + vre, ar * sim + ai * sre + vim))
            return tuple(new)

        zero = jnp.zeros((STATE_ROWS, LANES), F32)
        lax.fori_loop(0, per_batch, body, tuple((zero, zero) for _ in range(batch)))

    @pl.when(phase == 1)
    def _():
        for i, rows in enumerate(pair_rows):
            sp = jnp.concatenate([sre_ref[rows, :], sim_ref[rows, :]], axis=1).astype(BF16)
            for h in range(2 * i, 2 * i + 2):
                x = x_ref[h]
                y = (jnp.dot(x, toep_ref[h], preferred_element_type=F32)
                     + jnp.dot(sp, wout_ref[h], preferred_element_type=F32)
                     + d_ref[h] * x.astype(F32))
                y_ref[h] = jax.nn.gelu(y).astype(BF16)


def _ssm(x, toep, w_in, w_out, a_re, a_im, d_tiled, layer, batch, seq):
    nchunk = x.shape[1]
    per_batch = seq // CHUNK
    groups = 2 * SSM_PAIRS
    table = lambda shape: pl.BlockSpec((None, groups) + shape, lambda ph, q: (layer, q, 0, 0))
    return pl.pallas_call(
        functools.partial(_ssm_kernel, batch=batch, per_batch=per_batch),
        grid=(2, SSM_GROUPS // groups),
        in_specs=[pl.BlockSpec((groups, nchunk, CHUNK_COLS), lambda ph, q: (q, 0, 0)),
                  table((CHUNK_COLS, CHUNK_COLS)), table((CHUNK_COLS, CHUNK_COLS)),
                  table((CHUNK_COLS, CHUNK_COLS)),
                  _layer((STATE_ROWS, LANES), layer), _layer((STATE_ROWS, LANES), layer),
                  table((1, CHUNK_COLS))],
        out_specs=pl.BlockSpec((groups, nchunk, CHUNK_COLS), lambda ph, q: (q * ph, 0, 0)),
        out_shape=jax.ShapeDtypeStruct((SSM_GROUPS, nchunk, CHUNK_COLS), BF16),
        scratch_shapes=[pltpu.VMEM((nchunk * STATE_ROWS, LANES), F32),
                        pltpu.VMEM((nchunk * STATE_ROWS, LANES), F32)],
        compiler_params=_params("arbitrary", "arbitrary"),
        name="ssm",
    )(x, w_in, toep, w_out, a_re, a_im, d_tiled)


FF_CHUNK = D_FF


def _mix_ffn_kernel(x_ref, a_ref, ys_ref, gate_ref, wap_ref, wa_ref, wb_ref, wout_ref,
                    g_ref, wg_ref, wu_ref, wd_ref, out_ref, slab_ref):
    tm = x_ref.shape[0]
    a_out = jnp.dot(a_ref[...], wap_ref[...], preferred_element_type=F32)

    for s in range(N_SLABS):
        for half in range(CHUNK // PIECES):
            ws = [ys_ref[s * PIECES + i, :, half * LANES:(half + 1) * LANES].astype(F32)
                  for i in range(PIECES)]
            for j, v in enumerate(_piece_transpose(ws)):
                slab_ref[s, pl.ds(half * PIECES + j, tm // CHUNK, stride=CHUNK), :] = v
    y = jnp.concatenate([slab_ref[c] for c in range(N_SLABS)], axis=1).astype(BF16)
    s_out = (jnp.dot(y, wa_ref[...], preferred_element_type=F32)
             * jax.nn.sigmoid(jnp.dot(y, wb_ref[...], preferred_element_type=F32)))
    mix = (gate_ref[:, :D_MODEL].astype(F32) * a_out
           + gate_ref[:, D_MODEL:].astype(F32) * s_out)
    x = x_ref[...] + jnp.dot(mix.astype(BF16), wout_ref[...], preferred_element_type=F32)

    h = _rms_norm(x, g_ref[...]).astype(BF16)
    for c in range(D_FF // FF_CHUNK):
        cols = slice(c * FF_CHUNK, (c + 1) * FF_CHUNK)
        gate = jnp.dot(h, wg_ref[:, cols], preferred_element_type=F32)
        up = jnp.dot(h, wu_ref[:, cols], preferred_element_type=F32)
        hidden = (jax.nn.silu(gate) * up).astype(BF16)
        x = x + jnp.dot(hidden, wd_ref[cols, :], preferred_element_type=F32)
    out_ref[...] = x


def _mix_ffn(x, a, ys, gates, wap, wa, wb, wout, g_ffn, wg, wu, wd, layer):
    n = x.shape[0]
    tm = TOKEN_TILE
    row = lambda i: (i, 0)
    return pl.pallas_call(
        _mix_ffn_kernel,
        grid=(n // tm,),
        in_specs=[pl.BlockSpec((tm, D_MODEL), row),
                  pl.BlockSpec((tm, ATTN_WIDTH), row),
                  pl.BlockSpec((SSM_GROUPS, tm // CHUNK, CHUNK_COLS), lambda i: (0, i, 0)),
                  pl.BlockSpec((tm, 2 * D_MODEL), row),
                  _layer((ATTN_WIDTH, D_MODEL), layer), _layer((SSM_WIDTH, D_MODEL), layer),
                  _layer((SSM_WIDTH, D_MODEL), layer), _layer((D_MODEL, D_MODEL), layer),
                  _layer((1, D_MODEL), layer), _layer((D_MODEL, D_FF), layer),
                  _layer((D_MODEL, D_FF), layer), _layer((D_FF, D_MODEL), layer)],
        out_specs=pl.BlockSpec((tm, D_MODEL), row),
        out_shape=jax.ShapeDtypeStruct((n, D_MODEL), F32),
        scratch_shapes=[pltpu.VMEM((N_SLABS, tm, LANES), F32)],
        compiler_params=_params("arbitrary"),
        name="mix_ffn",
    )(x, a, ys, gates, wap, wa, wb, wout, g_ffn, wg, wu, wd)


def kernel(x, g_mix, w_in, g_q, g_k, w_attn_proj, lambda_re, lambda_im, log_dt, b_re, b_im,
           c_re, c_im, d_skip, w_glu_a, w_glu_b, w_out, g_ffn, w_ffn_gate, w_ffn_up, w_ffn_down):
    batch, seq, _ = x.shape
    depth = w_in.shape[0]
    n = batch * seq
    xf = x.reshape(n, D_MODEL).astype(F32)

    blk = jnp.arange(COL_TILE // 2) // HEAD_DIM
    bd = (blk[:, None] == blk[None, :]).astype(BF16)
    gq = jnp.tile(g_q.astype(F32) * (LOG2E * HEAD_DIM ** -0.5), (1, HEADS))[:, None]
    gk = jnp.tile(g_k.astype(F32), (1, HEADS))[:, None]
    toep, s_in, s_out, a_re, a_im = _ssm_tables(lambda_re, lambda_im, log_dt, b_re, b_im,
                                                c_re, c_im)
    d_tiled = jnp.tile(d_skip.astype(F32).reshape(depth, SSM_GROUPS, 1, SSM_GROUP),
                       (1, 1, 1, CHUNK))

    row3 = lambda g: g.astype(F32)[:, None, :]
    g_mix, g_ffn = row3(g_mix), row3(g_ffn)
    (w_in, w_attn_proj, w_glu_a, w_glu_b, w_out, w_ffn_gate, w_ffn_up, w_ffn_down) = (
        w.astype(BF16) for w in (w_in, w_attn_proj, w_glu_a, w_glu_b, w_out,
                                 w_ffn_gate, w_ffn_up, w_ffn_down))

    for l in range(depth):
        q, k, v, xs, gates = _in_proj(xf, g_mix, w_in, gq, gk, bd, l, batch, seq)
        dilated = [_attention_group(q[g], k[g], v[g], batch, seq, g)
                   for g in range(1, N_GROUPS)]
        a = _attention_group(q[0], k[0], v[0], batch, seq, 0, others=dilated)
        ys = _ssm(xs, toep, s_in, s_out, a_re, a_im, d_tiled, l, batch, seq)
        xf = _mix_ffn(xf, a, ys, gates, w_attn_proj, w_glu_a, w_glu_b, w_out, g_ffn,
                      w_ffn_gate, w_ffn_up, w_ffn_down, l)
    return xf.reshape(batch, seq, D_MODEL).astype(x.dtype)
```

```python
import functools

import jax
import jax.numpy as jnp
from jax import lax
from jax.experimental import pallas as pl
from jax.experimental.pallas import tpu as pltpu

F32 = jnp.float32
BF16 = jnp.bfloat16

D_MODEL = 1024
HEAD_DIM = 64
HEADS = 8
ATTN_WIDTH = HEADS * HEAD_DIM
DILATIONS = (1, 4, 16)
N_GROUPS = len(DILATIONS)
BLK = 128
SSM_WIDTH = 512
SSM_GROUP = 16
SSM_GROUPS = 32
SSM_STATE = 64
CHUNK = 16
CHUNK_COLS = CHUNK * SSM_GROUP
D_FF = 2816
IN_COLS = 3 * N_GROUPS * ATTN_WIDTH + SSM_WIDTH + 2 * D_MODEL
EPS = 1e-6
LOG2E = 1.4426950408889634

VMEM_LIMIT_BYTES = 56 * 1024 * 1024
LANES = 128
COL_TILE = 512
TOKEN_TILE = 512
PIECE = SSM_GROUP
PIECES = LANES // PIECE
N_SLABS = ATTN_WIDTH // LANES
STATE_ROWS = SSM_GROUPS // 2
STAGE = 4


def _params(*semantics):
    return pltpu.CompilerParams(dimension_semantics=semantics,
                                vmem_limit_bytes=VMEM_LIMIT_BYTES)


def _resident(shape):
    return pl.BlockSpec(shape, lambda *_: (0,) * len(shape),
                        pipeline_mode=pl.Buffered(1))


def _layer(shape, l):
    return pl.BlockSpec((None,) + shape, lambda *_: (l,) + (0,) * len(shape),
                        pipeline_mode=pl.Buffered(1))


def _rms_norm(x, gain):
    ms = jnp.mean(x * x, axis=-1, keepdims=True)
    return x * lax.rsqrt(ms + EPS) * gain


def _piece_transpose(vs):
    vs = list(vs)
    piece = lax.broadcasted_iota(jnp.int32, vs[0].shape, 1) // PIECE
    for k in (4, 2, 1):
        upper = (piece & k) != 0
        for j in range(PIECES):
            if j & k:
                continue
            a, b = vs[j], vs[j + k]
            vs[j] = jnp.where(upper, pltpu.roll(b, PIECE * k, 1), a)
            vs[j + k] = jnp.where(upper, b, pltpu.roll(a, LANES - PIECE * k, 1))
    return vs


def _in_proj_kernel(x_ref, g_ref, w_ref, gq_ref, gk_ref, bd_ref, *refs):
    qkv_refs = refs[:3 * N_GROUPS]
    xs_ref, gate_ref, h_ref, slab_ref, quad_ref = refs[3 * N_GROUPS:]
    tm = x_ref.shape[0]
    n_slabs = D_MODEL // LANES

    h = _rms_norm(x_ref[...], g_ref[...])
    h_ref[0] = h.astype(BF16)
    for c in range(n_slabs):
        slab_ref[c] = h[:, c * LANES:(c + 1) * LANES]
    for c in range(n_slabs):
        cols = slice(c * LANES, (c + 1) * LANES)
        for rho in range(STAGE):
            quarter = slab_ref[c, pl.ds(rho, tm // STAGE, stride=STAGE), :]
            quad_ref[c, rho] = quarter
            h_ref[1, rho * (tm // STAGE):(rho + 1) * (tm // STAGE), cols] = quarter.astype(BF16)
    for c in range(n_slabs):
        cols = slice(c * LANES, (c + 1) * LANES)
        for r in range(CHUNK):
            rows = quad_ref[c, r % STAGE, pl.ds(r // STAGE, tm // CHUNK, stride=STAGE), :]
            h_ref[2, r * (tm // CHUNK):(r + 1) * (tm // CHUNK), cols] = rows.astype(BF16)
    order = {1: 0, STAGE: 1, STAGE * STAGE: 2}

    def proj(j, d=1):
        return jnp.dot(h_ref[order[d]], w_ref[:, j * COL_TILE:(j + 1) * COL_TILE],
                       preferred_element_type=F32)

    def head_norm(z, gain):
        zz = (z * z).astype(BF16)
        half = COL_TILE // 2
        ss = jnp.concatenate(
            [jnp.dot(zz[:, :half], bd_ref[...], preferred_element_type=F32),
             jnp.dot(zz[:, half:], bd_ref[...], preferred_element_type=F32)], axis=1)
        return z * lax.rsqrt(ss * (1.0 / HEAD_DIM) + EPS) * gain

    def emit(z, out_ref, d):
        for r in range(d):
            out_ref[r] = z[r * (tm // d):(r + 1) * (tm // d)].astype(BF16)

    for g, d in enumerate(DILATIONS):
        emit(head_norm(proj(g, d), gq_ref[...]), qkv_refs[g], d)
        emit(head_norm(proj(N_GROUPS + g, d), gk_ref[...]), qkv_refs[N_GROUPS + g], d)
        emit(proj(2 * N_GROUPS + g, d), qkv_refs[2 * N_GROUPS + g], d)

    u = proj(3 * N_GROUPS, CHUNK)
    per = tm // CHUNK
    for a in range(N_SLABS):
        for half in range(CHUNK // PIECES):
            vs = [u[(half * PIECES + j) * per:(half * PIECES + j + 1) * per,
                    a * LANES:(a + 1) * LANES] for j in range(PIECES)]
            for i, w in enumerate(_piece_transpose(vs)):
                xs_ref[a * PIECES + i, :, half * LANES:(half + 1) * LANES] = w.astype(BF16)

    for j in range(2 * D_MODEL // COL_TILE):
        cols = slice(j * COL_TILE, (j + 1) * COL_TILE)
        gate_ref[:, cols] = jax.nn.sigmoid(proj(3 * N_GROUPS + 1 + j)).astype(BF16)


def _in_proj(x, g_mix, w_in, gq, gk, bd, layer, batch, seq):
    n = x.shape[0]
    tm = TOKEN_TILE
    tiles = seq // tm
    row = lambda i: (i, 0)
    dil_specs, dil_shapes = [], []
    for _ in range(3):
        for d in DILATIONS:
            dil_specs.append(pl.BlockSpec((None, d, tm // d, ATTN_WIDTH),
                                          lambda i: (i // tiles, 0, i % tiles, 0)))
            dil_shapes.append(jax.ShapeDtypeStruct((batch, d, seq // d, ATTN_WIDTH), BF16))
    outs = pl.pallas_call(
        _in_proj_kernel,
        grid=(n // tm,),
        in_specs=[pl.BlockSpec((tm, D_MODEL), row),
                  _layer((1, D_MODEL), layer),
                  _layer((D_MODEL, IN_COLS), layer),
                  _layer((1, COL_TILE), layer),
                  _layer((1, COL_TILE), layer),
                  _resident((COL_TILE // 2, COL_TILE // 2))],
        out_specs=dil_specs + [
            pl.BlockSpec((SSM_GROUPS, tm // CHUNK, CHUNK_COLS), lambda i: (0, i, 0)),
            pl.BlockSpec((tm, 2 * D_MODEL), row)],
        out_shape=dil_shapes + [
            jax.ShapeDtypeStruct((SSM_GROUPS, n // CHUNK, CHUNK_COLS), BF16),
            jax.ShapeDtypeStruct((n, 2 * D_MODEL), BF16)],
        scratch_shapes=[pltpu.VMEM((3, tm, D_MODEL), BF16),
                        pltpu.VMEM((D_MODEL // LANES, tm, LANES), F32),
                        pltpu.VMEM((D_MODEL // LANES, STAGE, tm // STAGE, LANES), F32)],
        compiler_params=_params("arbitrary"),
        name="in_proj",
    )(x, g_mix, w_in, gq, gk, bd)
    return outs[0:3], outs[3:6], outs[6:9], outs[9], outs[10]


ATTN_BLOCKS = {1: 16, 4: 2, 16: 1}
ATTN_RESIDUES = {1: 1, 4: 4, 16: 16}


def _attn_kernel(q_ref, k_ref, v_ref, *refs, dilation, nb, nr, merge):
    if merge:
        other = (refs[0], refs[1]), (refs[2], refs[3])
        a_ref, kbuf, vbuf = refs[4:]
        stage_refs = ()
    else:
        o_ref, lse_ref, kbuf, vbuf = refs[:4]
        stage_refs = refs[4:]
    step = pl.program_id(1)
    last = slice((nb - 1) * BLK, nb * BLK)

    qi = lax.broadcasted_iota(jnp.int32, (BLK, 2 * BLK), 0)
    kj = lax.broadcasted_iota(jnp.int32, (BLK, 2 * BLK), 1)
    neg = jnp.full((BLK, 2 * BLK), -jnp.inf, F32)
    zero = jnp.zeros((BLK, 2 * BLK), F32)
    cur_bias = jnp.where(kj - BLK <= qi, zero, neg)
    band = jnp.where(kj < BLK, jnp.where(kj >= qi, zero, neg), cur_bias)
    head = jnp.where(kj < BLK, jnp.where(step > 0, band, neg), cur_bias)
    band = jnp.concatenate([band, band], axis=0)
    head = jnp.concatenate([head, head], axis=0)

    lane_q = lax.broadcasted_iota(jnp.int32, (BLK, LANES), 1) < HEAD_DIM
    lane_kv = lax.broadcasted_iota(jnp.int32, (2 * BLK, LANES), 1) < HEAD_DIM
    head_of_row = lax.broadcasted_iota(jnp.int32, (4 * BLK, LANES), 0) // (2 * BLK)
    head_of_lane = lax.broadcasted_iota(jnp.int32, (4 * BLK, LANES), 1) // HEAD_DIM
    den_cols = jnp.where(head_of_row == head_of_lane, 1.0, 0.0).astype(BF16)

    residues = [rr if nr == dilation else pl.program_id(2) * nr + rr for rr in range(nr)]

    @pl.when(step == 0)
    def _():
        for r in residues:
            kbuf[r, 0:BLK, :] = jnp.zeros((BLK, ATTN_WIDTH), BF16)
            vbuf[r, 0:BLK, :] = jnp.zeros((BLK, ATTN_WIDTH), BF16)

    for rr, r in enumerate(residues):
        kb = kbuf.at[r]
        vb = vbuf.at[r]
        kb[BLK:2 * BLK, :] = k_ref[rr, 0:BLK, :]
        vb[BLK:2 * BLK, :] = v_ref[rr, 0:BLK, :]

        for j in range(nb):
            for p in range(HEADS // 2):
                cols = slice(p * LANES, (p + 1) * LANES)
                if j == 0:
                    keys, vals, bias = kb[:, cols], vb[:, cols], head
                else:
                    window = slice((j - 1) * BLK, (j + 1) * BLK)
                    keys, vals, bias = k_ref[rr, window, cols], v_ref[rr, window, cols], band
                qp = q_ref[rr, j * BLK:(j + 1) * BLK, cols]
                zq = jnp.zeros_like(qp)
                q2 = jnp.concatenate([jnp.where(lane_q, qp, zq), jnp.where(lane_q, zq, qp)],
                                     axis=0)
                s = lax.dot_general(q2, keys, (((1,), (1,)), ((), ())),
                                    preferred_element_type=F32) + bias
                m = jnp.max(s, axis=-1, keepdims=True)
                eb = jnp.exp2(s - m).astype(BF16)
                zv = jnp.zeros_like(vals)
                v2 = jnp.concatenate([jnp.where(lane_kv, vals, zv),
                                      jnp.where(lane_kv, zv, vals)], axis=0)
                acc = jnp.dot(jnp.concatenate([eb[:BLK], eb[BLK:]], axis=1),
                              jnp.concatenate([v2, den_cols], axis=1),
                              preferred_element_type=F32)
                den_lanes = acc[:, LANES:]
                m_lanes = jnp.where(lane_q, m[:BLK], m[BLK:])
                lse = m_lanes + jnp.log2(den_lanes)
                o = acc[:, :LANES] * (1.0 / den_lanes)
                if merge:
                    rows = slice(j * BLK, (j + 1) * BLK)
                    (o_a, l_a), (o_b, l_b) = [(o_r[p, rows, :], l_r[p, rows, :])
                                              for o_r, l_r in other]
                    top = jnp.maximum(jnp.maximum(lse, l_a), l_b)
                    e0, e1, e2 = jnp.exp2(lse - top), jnp.exp2(l_a - top), jnp.exp2(l_b - top)
                    a_ref[rows, cols] = ((e0 * o + e1 * o_a + e2 * o_b)
                                         * (1.0 / (e0 + e1 + e2))).astype(BF16)
                elif stage_refs:
                    rows = pl.ds(r // STAGE, BLK, stride=STAGE)
                    stage_refs[0][p, r % STAGE, rows, :] = o
                    stage_refs[1][p, r % STAGE, rows, :] = lse
                else:
                    rows = pl.ds(j * BLK * dilation + r, BLK, stride=dilation)
                    o_ref[p, rows, :] = o
                    lse_ref[p, rows, :] = lse

        kb[0:BLK, :] = k_ref[rr, last, :]
        vb[0:BLK, :] = v_ref[rr, last, :]

    if not merge and stage_refs:
        for staged, out_ref in zip(stage_refs, (o_ref, lse_ref)):
            for p in range(HEADS // 2):
                for rho in range(STAGE):
                    out_ref[p, pl.ds(rho, STAGE * BLK, stride=STAGE), :] = staged[p, rho]


def _attention_group(q, k, v, batch, seq, group, others=None):
    d = DILATIONS[group]
    nb, nr = ATTN_BLOCKS[d], ATTN_RESIDUES[d]
    steps = seq // d // (BLK * nb)
    merge = others is not None
    two_pass = d == STAGE * STAGE and nr == d and nb == 1
    staging = [pltpu.VMEM((N_SLABS, STAGE, STAGE * BLK, LANES), F32)] * 2 if two_pass else []
    in_spec = pl.BlockSpec((None, nr, BLK * nb, ATTN_WIDTH), lambda b, n, r: (b, r, n, 0))
    slab_spec = pl.BlockSpec((None, N_SLABS, BLK * nb * d, LANES), lambda b, n, r: (b, 0, n, 0))
    slab_shape = jax.ShapeDtypeStruct((batch, N_SLABS, seq, LANES), F32)
    if merge:
        assert d == 1
        extra = [t for pair in others for t in pair]
        out_specs = pl.BlockSpec((BLK * nb, ATTN_WIDTH), lambda b, n, r: (b * steps + n, 0))
        out_shape = jax.ShapeDtypeStruct((batch * seq, ATTN_WIDTH), BF16)
    else:
        extra = []
        out_specs = [slab_spec, slab_spec]
        out_shape = [slab_shape, slab_shape]
    return pl.pallas_call(
        functools.partial(_attn_kernel, dilation=d, nb=nb, nr=nr, merge=merge),
        grid=(batch, steps, d // nr),
        in_specs=[in_spec, in_spec, in_spec] + [slab_spec] * len(extra),
        out_specs=out_specs,
        out_shape=out_shape,
        scratch_shapes=[pltpu.VMEM((d, 2 * BLK, ATTN_WIDTH), BF16),
                        pltpu.VMEM((d, 2 * BLK, ATTN_WIDTH), BF16)] + staging,
        compiler_params=_params("arbitrary", "arbitrary", "arbitrary"),
        name=f"attn_d{d}",
    )(q, k, v, *extra)


def _ssm_tables(lam_re, lam_im, log_dt, b_re, b_im, c_re, c_im):
    hi = lax.Precision.HIGHEST
    depth = lam_re.shape[0]
    lr = lam_re.astype(F32)
    li = lam_im.astype(F32)
    dt = jnp.exp(log_dt.astype(F32))[..., None]
    mag = jnp.exp(lr * dt)
    ang = li * dt
    abar_re = mag * jnp.cos(ang)
    abar_im = mag * jnp.sin(ang)
    nr = abar_re - 1.0
    ni = abar_im
    den = lr * lr + li * li
    cr = ((nr * lr + ni * li) / den)[:, :, None, :]
    ci = ((ni * lr - nr * li) / den)[:, :, None, :]
    brt = b_re.astype(F32).transpose(0, 1, 3, 2)
    bit = b_im.astype(F32).transpose(0, 1, 3, 2)
    bbar_re = cr * brt - ci * bit
    bbar_im = cr * bit + ci * brt

    def powers(tau):
        tau = tau.astype(F32)[None, None, :, None]
        pmag = jnp.exp((lr * dt)[:, :, None, :] * tau)
        pang = ang[:, :, None, :] * tau
        return pmag * jnp.cos(pang), pmag * jnp.sin(pang)

    pw_re, pw_im = powers(jnp.arange(CHUNK + 1))

    wide = (CHUNK + 1) * SSM_GROUP
    lane = jnp.arange(wide)
    rep = (lane[None, :] // SSM_GROUP == jnp.arange(CHUNK + 1)[:, None]).astype(F32)
    til = (lane[None, :] % SSM_GROUP == jnp.arange(SSM_GROUP)[:, None]).astype(F32)
    pr_l = jnp.einsum('dgtp,tl->dgpl', pw_re, rep, precision=hi)
    pi_l = jnp.einsum('dgtp,tl->dgpl', pw_im, rep, precision=hi)
    cr_l = jnp.einsum('dgcp,cl->dgpl', c_re.astype(F32), til, precision=hi)
    ci_l = jnp.einsum('dgcp,cl->dgpl', c_im.astype(F32), til, precision=hi)
    cp_re = cr_l * pr_l - ci_l * pi_l
    cp_im = cr_l * pi_l + ci_l * pr_l

    lag = (jnp.einsum('dgcp,dgpl->dgcl', bbar_re, cp_re[..., :CHUNK_COLS], precision=hi)
           - jnp.einsum('dgcp,dgpl->dgcl', bbar_im, cp_im[..., :CHUNK_COLS], precision=hi))
    col = jnp.arange(CHUNK_COLS)
    shift = (col[None, None, :] == col[None, :, None]
             + PIECE * jnp.arange(CHUNK)[:, None, None]).astype(BF16)
    toep = jnp.einsum('dgcl,slm->dgscm', lag.astype(BF16), shift)
    toep = toep.reshape(depth, SSM_GROUPS, CHUNK_COLS, CHUNK_COLS)

    rev_re, rev_im = powers(CHUNK - 1 - jnp.arange(CHUNK))
    rev_re = rev_re[:, :, :, None, :]
    rev_im = rev_im[:, :, :, None, :]
    win_re = (rev_re * bbar_re[:, :, None] - rev_im * bbar_im[:, :, None])
    win_im = (rev_re * bbar_im[:, :, None] + rev_im * bbar_re[:, :, None])
    win_re = win_re.reshape(depth, SSM_GROUPS, CHUNK_COLS, SSM_STATE)
    win_im = win_im.reshape(depth, SSM_GROUPS, CHUNK_COLS, SSM_STATE)

    wout_re = cp_re[..., SSM_GROUP:]
    wout_im = -cp_im[..., SSM_GROUP:]

    odd = (jnp.arange(SSM_GROUPS) % 2 == 1)[None, :, None, None]
    zc = jnp.zeros_like(win_re)
    w_in = jnp.concatenate([jnp.where(odd, zc, win_re), jnp.where(odd, win_re, zc),
                            jnp.where(odd, zc, win_im), jnp.where(odd, win_im, zc)], axis=3)
    zr = jnp.zeros_like(wout_re)
    w_out = jnp.concatenate([jnp.where(odd, zr, wout_re), jnp.where(odd, wout_re, zr),
                             jnp.where(odd, zr, wout_im), jnp.where(odd, wout_im, zr)], axis=2)
    a_re, a_im = abar_re, abar_im
    for _ in range(CHUNK.bit_length() - 1):
        a_re, a_im = a_re * a_re - a_im * a_im, 2.0 * a_re * a_im
    a_re = a_re.reshape(depth, STATE_ROWS, LANES)
    a_im = a_im.reshape(depth, STATE_ROWS, LANES)
    return toep, w_in.astype(BF16), w_out.astype(BF16), a_re, a_im


SSM_PAIRS = 4


def _ssm_kernel(x_ref, win_ref, toep_ref, wout_ref, are_ref, aim_ref, d_ref, y_ref,
                sre_ref, sim_ref, *, batch, per_batch):
    phase = pl.program_id(0)
    step = pl.program_id(1)
    nchunk = batch * per_batch
    pair_rows = [pl.ds(step * SSM_PAIRS + i, nchunk, stride=STATE_ROWS)
                 for i in range(SSM_PAIRS)]

    @pl.when(phase == 0)
    def _():
        for i, rows in enumerate(pair_rows):
            v = (jnp.dot(x_ref[2 * i], win_ref[2 * i], preferred_element_type=F32)
                 + jnp.dot(x_ref[2 * i + 1], win_ref[2 * i + 1], preferred_element_type=F32))
            sre_ref[rows, :] = v[:, :LANES]
            sim_ref[rows, :] = v[:, LANES:]

    @pl.when((phase == 0) & (step == pl.num_programs(1) - 1))
    def _():
        ar = are_ref[...]
        ai = aim_ref[...]

        def body(k, carry):
            new = []
            for b in range(batch):
                sre, sim = carry[b]
                rows = pl.ds(pl.multiple_of((b * per_batch + k) * STATE_ROWS, STATE_ROWS),
                             STATE_ROWS)
                vre = sre_ref[rows, :]
                vim = sim_ref[rows, :]
                sre_ref[rows, :] = sre
                sim_ref[rows, :] = sim
                new.append((ar * sre - ai * sim + vre, ar * sim + ai * sre + vim))
            return tuple(new)

        zero = jnp.zeros((STATE_ROWS, LANES), F32)
        lax.fori_loop(0, per_batch, body, tuple((zero, zero) for _ in range(batch)))

    @pl.when(phase == 1)
    def _():
        for i, rows in enumerate(pair_rows):
            sp = jnp.concatenate([sre_ref[rows, :], sim_ref[rows, :]], axis=1).astype(BF16)
            for h in range(2 * i, 2 * i + 2):
                x = x_ref[h]
                y = (jnp.dot(x, toep_ref[h], preferred_element_type=F32)
                     + jnp.dot(sp, wout_ref[h], preferred_element_type=F32)
                     + d_ref[h] * x.astype(F32))
                y_ref[h] = jax.nn.gelu(y).astype(BF16)


def _ssm(x, toep, w_in, w_out, a_re, a_im, d_tiled, layer, batch, seq):
    nchunk = x.shape[1]
    per_batch = seq // CHUNK
    groups = 2 * SSM_PAIRS
    table = lambda shape: pl.BlockSpec((None, groups) + shape, lambda ph, q: (layer, q, 0, 0))
    return pl.pallas_call(
        functools.partial(_ssm_kernel, batch=batch, per_batch=per_batch),
        grid=(2, SSM_GROUPS // groups),
        in_specs=[pl.BlockSpec((groups, nchunk, CHUNK_COLS), lambda ph, q: (q, 0, 0)),
                  table((CHUNK_COLS, CHUNK_COLS)), table((CHUNK_COLS, CHUNK_COLS)),
                  table((CHUNK_COLS, CHUNK_COLS)),
                  _layer((STATE_ROWS, LANES), layer), _layer((STATE_ROWS, LANES), layer),
                  table((1, CHUNK_COLS))],
        out_specs=pl.BlockSpec((groups, nchunk, CHUNK_COLS), lambda ph, q: (q * ph, 0, 0)),
        out_shape=jax.ShapeDtypeStruct((SSM_GROUPS, nchunk, CHUNK_COLS), BF16),
        scratch_shapes=[pltpu.VMEM((nchunk * STATE_ROWS, LANES), F32),
                        pltpu.VMEM((nchunk * STATE_ROWS, LANES), F32)],
        compiler_params=_params("arbitrary", "arbitrary"),
        name="ssm",
    )(x, w_in, toep, w_out, a_re, a_im, d_tiled)


FF_CHUNK = D_FF


def _mix_ffn_kernel(x_ref, a_ref, ys_ref, gate_ref, wap_ref, wa_ref, wb_ref, wout_ref,
                    g_ref, wg_ref, wu_ref, wd_ref, out_ref, slab_ref):
    tm = x_ref.shape[0]
    a_out = jnp.dot(a_ref[...], wap_ref[...], preferred_element_type=F32)

    for s in range(N_SLABS):
        for half in range(CHUNK // PIECES):
            ws = [ys_ref[s * PIECES + i, :, half * LANES:(half + 1) * LANES].astype(F32)
                  for i in range(PIECES)]
            for j, v in enumerate(_piece_transpose(ws)):
                slab_ref[s, pl.ds(half * PIECES + j, tm // CHUNK, stride=CHUNK), :] = v
    y = jnp.concatenate([slab_ref[c] for c in range(N_SLABS)], axis=1).astype(BF16)
    s_out = (jnp.dot(y, wa_ref[...], preferred_element_type=F32)
             * jax.nn.sigmoid(jnp.dot(y, wb_ref[...], preferred_element_type=F32)))
    mix = (gate_ref[:, :D_MODEL].astype(F32) * a_out
           + gate_ref[:, D_MODEL:].astype(F32) * s_out)
    x = x_ref[...] + jnp.dot(mix.astype(BF16), wout_ref[...], preferred_element_type=F32)

    h = _rms_norm(x, g_ref[...]).astype(BF16)
    for c in range(D_FF // FF_CHUNK):
        cols = slice(c * FF_CHUNK, (c + 1) * FF_CHUNK)
        gate = jnp.dot(h, wg_ref[:, cols], preferred_element_type=F32)
        up = jnp.dot(h, wu_ref[:, cols], preferred_element_type=F32)
        hidden = (jax.nn.silu(gate) * up).astype(BF16)
        x = x + jnp.dot(hidden, wd_ref[cols, :], preferred_element_type=F32)
    out_ref[...] = x


def _mix_ffn(x, a, ys, gates, wap, wa, wb, wout, g_ffn, wg, wu, wd, layer):
    n = x.shape[0]
    tm = TOKEN_TILE
    row = lambda i: (i, 0)
    return pl.pallas_call(
        _mix_ffn_kernel,
        grid=(n // tm,),
        in_specs=[pl.BlockSpec((tm, D_MODEL), row),
                  pl.BlockSpec((tm, ATTN_WIDTH), row),
                  pl.BlockSpec((SSM_GROUPS, tm // CHUNK, CHUNK_COLS), lambda i: (0, i, 0)),
                  pl.BlockSpec((tm, 2 * D_MODEL), row),
                  _layer((ATTN_WIDTH, D_MODEL), layer), _layer((SSM_WIDTH, D_MODEL), layer),
                  _layer((SSM_WIDTH, D_MODEL), layer), _layer((D_MODEL, D_MODEL), layer),
                  _layer((1, D_MODEL), layer), _layer((D_MODEL, D_FF), layer),
                  _layer((D_MODEL, D_FF), layer), _layer((D_FF, D_MODEL), layer)],
        out_specs=pl.BlockSpec((tm, D_MODEL), row),
        out_shape=jax.ShapeDtypeStruct((n, D_MODEL), F32),
        scratch_shapes=[pltpu.VMEM((N_SLABS, tm, LANES), F32)],
        compiler_params=_params("arbitrary"),
        name="mix_ffn",
    )(x, a, ys, gates, wap, wa, wb, wout, g_ffn, wg, wu, wd)


def kernel(x, g_mix, w_in, g_q, g_k, w_attn_proj, lambda_re, lambda_im, log_dt, b_re, b_im,
           c_re, c_im, d_skip, w_glu_a, w_glu_b, w_out, g_ffn, w_ffn_gate, w_ffn_up, w_ffn_down):
    batch, seq, _ = x.shape
    depth = w_in.shape[0]
    n = batch * seq
    xf = x.reshape(n, D_MODEL).astype(F32)

    blk = jnp.arange(COL_TILE // 2) // HEAD_DIM
    bd = (blk[:, None] == blk[None, :]).astype(BF16)
    gq = jnp.tile(g_q.astype(F32) * (LOG2E * HEAD_DIM ** -0.5), (1, HEADS))[:, None]
    gk = jnp.tile(g_k.astype(F32), (1, HEADS))[:, None]
    toep, s_in, s_out, a_re, a_im = _ssm_tables(lambda_re, lambda_im, log_dt, b_re, b_im,
                                                c_re, c_im)
    d_tiled = jnp.tile(d_skip.astype(F32).reshape(depth, SSM_GROUPS, 1, SSM_GROUP),
                       (1, 1, 1, CHUNK))

    row3 = lambda g: g.astype(F32)[:, None, :]
    g_mix, g_ffn = row3(g_mix), row3(g_ffn)
    (w_in, w_attn_proj, w_glu_a, w_glu_b, w_out, w_ffn_gate, w_ffn_up, w_ffn_down) = (
        w.astype(BF16) for w in (w_in, w_attn_proj, w_glu_a, w_glu_b, w_out,
                                 w_ffn_gate, w_ffn_up, w_ffn_down))

    for l in range(depth):
        q, k, v, xs, gates = _in_proj(xf, g_mix, w_in, gq, gk, bd, l, batch, seq)
        dilated = [_attention_group(q[g], k[g], v[g], batch, seq, g)
                   for g in range(1, N_GROUPS)]
        a = _attention_group(q[0], k[0], v[0], batch, seq, 0, others=dilated)
        ys = _ssm(xs, toep, s_in, s_out, a_re, a_im, d_tiled, l, batch, seq)
        xf = _mix_ffn(xf, a, ys, gates, w_attn_proj, w_glu_a, w_glu_b, w_out, g_ffn,
                      w_ffn_gate, w_ffn_up, w_ffn_down, l)
    return xf.reshape(batch, seq, D_MODEL).astype(x.dtype)
```

```python
import functools

import jax
import jax.numpy as jnp
from jax import lax
from jax.experimental import pallas as pl
from jax.experimental.pallas import tpu as pltpu

F32 = jnp.float32
BF16 = jnp.bfloat16

D_MODEL = 1024
HEAD_DIM = 64
HEADS = 8
ATTN_WIDTH = HEADS * HEAD_DIM
DILATIONS = (1, 4, 16)
N_GROUPS = len(DILATIONS)
BLK = 128
SSM_WIDTH = 512
SSM_GROUP = 16
SSM_GROUPS = 32
SSM_STATE = 64
CHUNK = 16
CHUNK_COLS = CHUNK * SSM_GROUP
D_FF = 2816
IN_COLS = 3 * N_GROUPS * ATTN_WIDTH + SSM_WIDTH + 2 * D_MODEL
EPS = 1e-6
LOG2E = 1.4426950408889634

VMEM_LIMIT_BYTES = 56 * 1024 * 1024
LANES = 128
COL_TILE = 512
TOKEN_TILE = 512
PIECE = SSM_GROUP
PIECES = LANES // PIECE
N_SLABS = ATTN_WIDTH // LANES
STATE_ROWS = SSM_GROUPS // 2
STAGE = 4


def _params(*semantics):
    return pltpu.CompilerParams(dimension_semantics=semantics,
                                vmem_limit_bytes=VMEM_LIMIT_BYTES)


def _resident(shape):
    return pl.BlockSpec(shape, lambda *_: (0,) * len(shape),
                        pipeline_mode=pl.Buffered(1))


def _layer(shape, l):
    return pl.BlockSpec((None,) + shape, lambda *_: (l,) + (0,) * len(shape),
                        pipeline_mode=pl.Buffered(1))


def _rms_norm(x, gain):
    ms = jnp.mean(x * x, axis=-1, keepdims=True)
    return x * lax.rsqrt(ms + EPS) * gain


def _piece_transpose(vs):
    vs = list(vs)
    piece = lax.broadcasted_iota(jnp.int32, vs[0].shape, 1) // PIECE
    for k in (4, 2, 1):
        upper = (piece & k) != 0
        for j in range(PIECES):
            if j & k:
                continue
            a, b = vs[j], vs[j + k]
            vs[j] = jnp.where(upper, pltpu.roll(b, PIECE * k, 1), a)
            vs[j + k] = jnp.where(upper, b, pltpu.roll(a, LANES - PIECE * k, 1))
    return vs


def _in_proj_kernel(x_ref, g_ref, w_ref, gq_ref, gk_ref, bd_ref, *refs):
    qkv_refs = refs[:3 * N_GROUPS]
    xs_ref, gate_ref, h_ref, slab_ref, quad_ref = refs[3 * N_GROUPS:]
    tm = x_ref.shape[0]
    n_slabs = D_MODEL // LANES

    h = _rms_norm(x_ref[...], g_ref[...])
    h_ref[0] = h.astype(BF16)
    for c in range(n_slabs):
        slab_ref[c] = h[:, c * LANES:(c + 1) * LANES]
    for c in range(n_slabs):
        cols = slice(c * LANES, (c + 1) * LANES)
        for rho in range(STAGE):
            quarter = slab_ref[c, pl.ds(rho, tm // STAGE, stride=STAGE), :]
            quad_ref[c, rho] = quarter
            h_ref[1, rho * (tm // STAGE):(rho + 1) * (tm // STAGE), cols] = quarter.astype(BF16)
    for c in range(n_slabs):
        cols = slice(c * LANES, (c + 1) * LANES)
        for r in range(CHUNK):
            rows = quad_ref[c, r % STAGE, pl.ds(r // STAGE, tm // CHUNK, stride=STAGE), :]
            h_ref[2, r * (tm // CHUNK):(r + 1) * (tm // CHUNK), cols] = rows.astype(BF16)
    order = {1: 0, STAGE: 1, STAGE * STAGE: 2}

    def proj(j, d=1):
        return jnp.dot(h_ref[order[d]], w_ref[:, j * COL_TILE:(j + 1) * COL_TILE],
                       preferred_element_type=F32)

    def head_norm(z, gain):
        zz = (z * z).astype(BF16)
        half = COL_TILE // 2
        ss = jnp.concatenate(
            [jnp.dot(zz[:, :half], bd_ref[...], preferred_element_type=F32),
             jnp.dot(zz[:, half:], bd_ref[...], preferred_element_type=F32)], axis=1)
        return z * lax.rsqrt(ss * (1.0 / HEAD_DIM) + EPS) * gain

    def emit(z, out_ref, d):
        for r in range(d):
            out_ref[r] = z[r * (tm // d):(r + 1) * (tm // d)].astype(BF16)

    for g, d in enumerate(DILATIONS):
        emit(head_norm(proj(g, d), gq_ref[...]), qkv_refs[g], d)
        emit(head_norm(proj(N_GROUPS + g, d), gk_ref[...]), qkv_refs[N_GROUPS + g], d)
        emit(proj(2 * N_GROUPS + g, d), qkv_refs[2 * N_GROUPS + g], d)

    u = proj(3 * N_GROUPS, CHUNK)
    per = tm // CHUNK
    for a in range(N_SLABS):
        for half in range(CHUNK // PIECES):
            vs = [u[(half * PIECES + j) * per:(half * PIECES + j + 1) * per,
                    a * LANES:(a + 1) * LANES] for j in range(PIECES)]
            for i, w in enumerate(_piece_transpose(vs)):
                xs_ref[a * PIECES + i, :, half * LANES:(half + 1) * LANES] = w.astype(BF16)

    for j in range(2 * D_MODEL // COL_TILE):
        cols = slice(j * COL_TILE, (j + 1) * COL_TILE)
        gate_ref[:, cols] = jax.nn.sigmoid(proj(3 * N_GROUPS + 1 + j)).astype(BF16)


def _in_proj(x, g_mix, w_in, gq, gk, bd, layer, batch, seq):
    n = x.shape[0]
    tm = TOKEN_TILE
    tiles = seq // tm
    row = lambda i: (i, 0)
    dil_specs, dil_shapes = [], []
    for _ in range(3):
        for d in DILATIONS:
            dil_specs.append(pl.BlockSpec((None, d, tm // d, ATTN_WIDTH),
                                          lambda i: (i // tiles, 0, i % tiles, 0)))
            dil_shapes.append(jax.ShapeDtypeStruct((batch, d, seq // d, ATTN_WIDTH), BF16))
    outs = pl.pallas_call(
        _in_proj_kernel,
        grid=(n // tm,),
        in_specs=[pl.BlockSpec((tm, D_MODEL), row),
                  _layer((1, D_MODEL), layer),
                  _layer((D_MODEL, IN_COLS), layer),
                  _layer((1, COL_TILE), layer),
                  _layer((1, COL_TILE), layer),
                  _resident((COL_TILE // 2, COL_TILE // 2))],
        out_specs=dil_specs + [
            pl.BlockSpec((SSM_GROUPS, tm // CHUNK, CHUNK_COLS), lambda i: (0, i, 0)),
            pl.BlockSpec((tm, 2 * D_MODEL), row)],
        out_shape=dil_shapes + [
            jax.ShapeDtypeStruct((SSM_GROUPS, n // CHUNK, CHUNK_COLS), BF16),
            jax.ShapeDtypeStruct((n, 2 * D_MODEL), BF16)],
        scratch_shapes=[pltpu.VMEM((3, tm, D_MODEL), BF16),
                        pltpu.VMEM((D_MODEL // LANES, tm, LANES), F32),
                        pltpu.VMEM((D_MODEL // LANES, STAGE, tm // STAGE, LANES), F32)],
        compiler_params=_params("arbitrary"),
        name="in_proj",
    )(x, g_mix, w_in, gq, gk, bd)
    return outs[0:3], outs[3:6], outs[6:9], outs[9], outs[10]


ATTN_BLOCKS = {1: 16, 4: 4, 16: 1}
ATTN_RESIDUES = {1: 1, 4: 4, 16: 16}


def _attn_kernel(q_ref, k_ref, v_ref, *refs, dilation, nb, nr, merge):
    if merge:
        other = (refs[0], refs[1]), (refs[2], refs[3])
        a_ref, kbuf, vbuf = refs[4:]
        stage_refs = ()
    else:
        o_ref, lse_ref, kbuf, vbuf = refs[:4]
        stage_refs = refs[4:]
    step = pl.program_id(1)
    last = slice((nb - 1) * BLK, nb * BLK)

    qi = lax.broadcasted_iota(jnp.int32, (BLK, 2 * BLK), 0)
    kj = lax.broadcasted_iota(jnp.int32, (BLK, 2 * BLK), 1)
    neg = jnp.full((BLK, 2 * BLK), -jnp.inf, F32)
    zero = jnp.zeros((BLK, 2 * BLK), F32)
    cur_bias = jnp.where(kj - BLK <= qi, zero, neg)
    band = jnp.where(kj < BLK, jnp.where(kj >= qi, zero, neg), cur_bias)
    head = jnp.where(kj < BLK, jnp.where(step > 0, band, neg), cur_bias)
    band = jnp.concatenate([band, band], axis=0)
    head = jnp.concatenate([head, head], axis=0)

    lane_q = lax.broadcasted_iota(jnp.int32, (BLK, LANES), 1) < HEAD_DIM
    lane_kv = lax.broadcasted_iota(jnp.int32, (2 * BLK, LANES), 1) < HEAD_DIM
    head_of_row = lax.broadcasted_iota(jnp.int32, (4 * BLK, LANES), 0) // (2 * BLK)
    head_of_lane = lax.broadcasted_iota(jnp.int32, (4 * BLK, LANES), 1) // HEAD_DIM
    den_cols = jnp.where(head_of_row == head_of_lane, 1.0, 0.0).astype(BF16)

    residues = [rr if nr == dilation else pl.program_id(2) * nr + rr for rr in range(nr)]

    @pl.when(step == 0)
    def _():
        for r in residues:
            kbuf[r, 0:BLK, :] = jnp.zeros((BLK, ATTN_WIDTH), BF16)
            vbuf[r, 0:BLK, :] = jnp.zeros((BLK, ATTN_WIDTH), BF16)

    for rr, r in enumerate(residues):
        kb = kbuf.at[r]
        vb = vbuf.at[r]
        kb[BLK:2 * BLK, :] = k_ref[rr, 0:BLK, :]
        vb[BLK:2 * BLK, :] = v_ref[rr, 0:BLK, :]

        for j in range(nb):
            for p in range(HEADS // 2):
                cols = slice(p * LANES, (p + 1) * LANES)
                if j == 0:
                    keys, vals, bias = kb[:, cols], vb[:, cols], head
                else:
                    window = slice((j - 1) * BLK, (j + 1) * BLK)
                    keys, vals, bias = k_ref[rr, window, cols], v_ref[rr, window, cols], band
                qp = q_ref[rr, j * BLK:(j + 1) * BLK, cols]
                zq = jnp.zeros_like(qp)
                q2 = jnp.concatenate([jnp.where(lane_q, qp, zq), jnp.where(lane_q, zq, qp)],
                                     axis=0)
                s = lax.dot_general(q2, keys, (((1,), (1,)), ((), ())),
                                    preferred_element_type=F32) + bias
                m = jnp.max(s, axis=-1, keepdims=True)
                eb = jnp.exp2(s - m).astype(BF16)
                zv = jnp.zeros_like(vals)
                v2 = jnp.concatenate([jnp.where(lane_kv, vals, zv),
                                      jnp.where(lane_kv, zv, vals)], axis=0)
                acc = jnp.dot(jnp.concatenate([eb[:BLK], eb[BLK:]], axis=1),
                              jnp.concatenate([v2, den_cols], axis=1),
                              preferred_element_type=F32)
                den_lanes = acc[:, LANES:]
                m_lanes = jnp.where(lane_q, m[:BLK], m[BLK:])
                lse = m_lanes + jnp.log2(den_lanes)
                o = acc[:, :LANES] * (1.0 / den_lanes)
                if merge:
                    rows = slice(j * BLK, (j + 1) * BLK)
                    (o_a, l_a), (o_b, l_b) = [(o_r[p, rows, :], l_r[p, rows, :])
                                              for o_r, l_r in other]
                    top = jnp.maximum(jnp.maximum(lse, l_a), l_b)
                    e0, e1, e2 = jnp.exp2(lse - top), jnp.exp2(l_a - top), jnp.exp2(l_b - top)
                    a_ref[rows, cols] = ((e0 * o + e1 * o_a + e2 * o_b)
                                         * (1.0 / (e0 + e1 + e2))).astype(BF16)
                elif stage_refs:
                    rows = pl.ds(r // STAGE, BLK, stride=STAGE)
                    stage_refs[0][p, r % STAGE, rows, :] = o
                    stage_refs[1][p, r % STAGE, rows, :] = lse
                else:
                    rows = pl.ds(j * BLK * dilation + r, BLK, stride=dilation)
                    o_ref[p, rows, :] = o
                    lse_ref[p, rows, :] = lse

        kb[0:BLK, :] = k_ref[rr, last, :]
        vb[0:BLK, :] = v_ref[rr, last, :]

    if not merge and stage_refs:
        for staged, out_ref in zip(stage_refs, (o_ref, lse_ref)):
            for p in range(HEADS // 2):
                for rho in range(STAGE):
                    out_ref[p, pl.ds(rho, STAGE * BLK, stride=STAGE), :] = staged[p, rho]


def _attention_group(q, k, v, batch, seq, group, others=None):
    d = DILATIONS[group]
    nb, nr = ATTN_BLOCKS[d], ATTN_RESIDUES[d]
    steps = seq // d // (BLK * nb)
    merge = others is not None
    two_pass = d == STAGE * STAGE and nr == d and nb == 1
    staging = [pltpu.VMEM((N_SLABS, STAGE, STAGE * BLK, LANES), F32)] * 2 if two_pass else []
    in_spec = pl.BlockSpec((None, nr, BLK * nb, ATTN_WIDTH), lambda b, n, r: (b, r, n, 0))
    slab_spec = pl.BlockSpec((None, N_SLABS, BLK * nb * d, LANES), lambda b, n, r: (b, 0, n, 0))
    slab_shape = jax.ShapeDtypeStruct((batch, N_SLABS, seq, LANES), F32)
    if merge:
        assert d == 1
        extra = [t for pair in others for t in pair]
        out_specs = pl.BlockSpec((BLK * nb, ATTN_WIDTH), lambda b, n, r: (b * steps + n, 0))
        out_shape = jax.ShapeDtypeStruct((batch * seq, ATTN_WIDTH), BF16)
    else:
        extra = []
        out_specs = [slab_spec, slab_spec]
        out_shape = [slab_shape, slab_shape]
    return pl.pallas_call(
        functools.partial(_attn_kernel, dilation=d, nb=nb, nr=nr, merge=merge),
        grid=(batch, steps, d // nr),
        in_specs=[in_spec, in_spec, in_spec] + [slab_spec] * len(extra),
        out_specs=out_specs,
        out_shape=out_shape,
        scratch_shapes=[pltpu.VMEM((d, 2 * BLK, ATTN_WIDTH), BF16),
                        pltpu.VMEM((d, 2 * BLK, ATTN_WIDTH), BF16)] + staging,
        compiler_params=_params("arbitrary", "arbitrary", "arbitrary"),
        name=f"attn_d{d}",
    )(q, k, v, *extra)


def _ssm_tables(lam_re, lam_im, log_dt, b_re, b_im, c_re, c_im):
    hi = lax.Precision.HIGHEST
    depth = lam_re.shape[0]
    lr = lam_re.astype(F32)
    li = lam_im.astype(F32)
    dt = jnp.exp(log_dt.astype(F32))[..., None]
    mag = jnp.exp(lr * dt)
    ang = li * dt
    abar_re = mag * jnp.cos(ang)
    abar_im = mag * jnp.sin(ang)
    nr = abar_re - 1.0
    ni = abar_im
    den = lr * lr + li * li
    cr = ((nr * lr + ni * li) / den)[:, :, None, :]
    ci = ((ni * lr - nr * li) / den)[:, :, None, :]
    brt = b_re.astype(F32).transpose(0, 1, 3, 2)
    bit = b_im.astype(F32).transpose(0, 1, 3, 2)
    bbar_re = cr * brt - ci * bit
    bbar_im = cr * bit + ci * brt

    def powers(tau):
        tau = tau.astype(F32)[None, None, :, None]
        pmag = jnp.exp((lr * dt)[:, :, None, :] * tau)
        pang = ang[:, :, None, :] * tau
        return pmag * jnp.cos(pang), pmag * jnp.sin(pang)

    pw_re, pw_im = powers(jnp.arange(CHUNK + 1))

    wide = (CHUNK + 1) * SSM_GROUP
    lane = jnp.arange(wide)
    rep = (lane[None, :] // SSM_GROUP == jnp.arange(CHUNK + 1)[:, None]).astype(F32)
    til = (lane[None, :] % SSM_GROUP == jnp.arange(SSM_GROUP)[:, None]).astype(F32)
    pr_l = jnp.einsum('dgtp,tl->dgpl', pw_re, rep, precision=hi)
    pi_l = jnp.einsum('dgtp,tl->dgpl', pw_im, rep, precision=hi)
    cr_l = jnp.einsum('dgcp,cl->dgpl', c_re.astype(F32), til, precision=hi)
    ci_l = jnp.einsum('dgcp,cl->dgpl', c_im.astype(F32), til, precision=hi)
    cp_re = cr_l * pr_l - ci_l * pi_l
    cp_im = cr_l * pi_l + ci_l * pr_l

    lag = (jnp.einsum('dgcp,dgpl->dgcl', bbar_re, cp_re[..., :CHUNK_COLS], precision=hi)
           - jnp.einsum('dgcp,dgpl->dgcl', bbar_im, cp_im[..., :CHUNK_COLS], precision=hi))
    col = jnp.arange(CHUNK_COLS)
    shift = (col[None, None, :] == col[None, :, None]
             + PIECE * jnp.arange(CHUNK)[:, None, None]).astype(BF16)
    toep = jnp.einsum('dgcl,slm->dgscm', lag.astype(BF16), shift)
    toep = toep.reshape(depth, SSM_GROUPS, CHUNK_COLS, CHUNK_COLS)

    rev_re, rev_im = powers(CHUNK - 1 - jnp.arange(CHUNK))
    rev_re = rev_re[:, :, :, None, :]
    rev_im = rev_im[:, :, :, None, :]
    win_re = (rev_re * bbar_re[:, :, None] - rev_im * bbar_im[:, :, None])
    win_im = (rev_re * bbar_im[:, :, None] + rev_im * bbar_re[:, :, None])
    win_re = win_re.reshape(depth, SSM_GROUPS, CHUNK_COLS, SSM_STATE)
    win_im = win_im.reshape(depth, SSM_GROUPS, CHUNK_COLS, SSM_STATE)

    wout_re = cp_re[..., SSM_GROUP:]
    wout_im = -cp_im[..., SSM_GROUP:]

    odd = (jnp.arange(SSM_GROUPS) % 2 == 1)[None, :, None, None]
    zc = jnp.zeros_like(win_re)
    w_in = jnp.concatenate([jnp.where(odd, zc, win_re), jnp.where(odd, win_re, zc),
                            jnp.where(odd, zc, win_im), jnp.where(odd, win_im, zc)], axis=3)
    zr = jnp.zeros_like(wout_re)
    w_out = jnp.concatenate([jnp.where(odd, zr, wout_re), jnp.where(odd, wout_re, zr),
                             jnp.where(odd, zr, wout_im), jnp.where(odd, wout_im, zr)], axis=2)
    a_re, a_im = abar_re, abar_im
    for _ in range(CHUNK.bit_length() - 1):
        a_re, a_im = a_re * a_re - a_im * a_im, 2.0 * a_re * a_im
    a_re = a_re.reshape(depth, STATE_ROWS, LANES)
    a_im = a_im.reshape(depth, STATE_ROWS, LANES)
    return toep, w_in.astype(BF16), w_out.astype(BF16), a_re, a_im


SSM_PAIRS = 4


def _ssm_kernel(x_ref, win_ref, toep_ref, wout_ref, are_ref, aim_ref, d_ref, y_ref,
                sre_ref, sim_ref, *, batch, per_batch):
    phase = pl.program_id(0)
    step = pl.program_id(1)
    nchunk = batch * per_batch
    pair_rows = [pl.ds(step * SSM_PAIRS + i, nchunk, stride=STATE_ROWS)
                 for i in range(SSM_PAIRS)]

    @pl.when(phase == 0)
    def _():
        for i, rows in enumerate(pair_rows):
            v = (jnp.dot(x_ref[2 * i], win_ref[2 * i], preferred_element_type=F32)
                 + jnp.dot(x_ref[2 * i + 1], win_ref[2 * i + 1], preferred_element_type=F32))
            sre_ref[rows, :] = v[:, :LANES]
            sim_ref[rows, :] = v[:, LANES:]

    @pl.when((phase == 0) & (step == pl.num_programs(1) - 1))
    def _():
        ar = are_ref[...]
        ai = aim_ref[...]

        def body(k, carry):
            new = []
            for b in range(batch):
                sre, sim = carry[b]
                rows = pl.ds(pl.multiple_of((b * per_batch + k) * STATE_ROWS, STATE_ROWS),
                             STATE_ROWS)
                vre = sre_ref[rows, :]
                vim = sim_ref[rows, :]
                sre_ref[rows, :] = sre
                sim_ref[rows, :] = sim
                new.append((ar * sre - ai * sim + vre, ar * sim + ai * sre + vim))
            return tuple(new)

        zero = jnp.zeros((STATE_ROWS, LANES), F32)
        lax.fori_loop(0, per_batch, body, tuple((zero, zero) for _ in range(batch)))

    @pl.when(phase == 1)
    def _():
        for i, rows in enumerate(pair_rows):
            sp = jnp.concatenate([sre_ref[rows, :], sim_ref[rows, :]], axis=1).astype(BF16)
            for h in range(2 * i, 2 * i + 2):
                x = x_ref[h]
                y = (jnp.dot(x, toep_ref[h], preferred_element_type=F32)
                     + jnp.dot(sp, wout_ref[h], preferred_element_type=F32)
                     + d_ref[h] * x.astype(F32))
                y_ref[h] = jax.nn.gelu(y).astype(BF16)


def _ssm(x, toep, w_in, w_out, a_re, a_im, d_tiled, layer, batch, seq):
    nchunk = x.shape[1]
    per_batch = seq // CHUNK
    groups = 2 * SSM_PAIRS
    table = lambda shape: pl.BlockSpec((None, groups) + shape, lambda ph, q: (layer, q, 0, 0))
    return pl.pallas_call(
        functools.partial(_ssm_kernel, batch=batch, per_batch=per_batch),
        grid=(2, SSM_GROUPS // groups),
        in_specs=[pl.BlockSpec((groups, nchunk, CHUNK_COLS), lambda ph, q: (q, 0, 0)),
                  table((CHUNK_COLS, CHUNK_COLS)), table((CHUNK_COLS, CHUNK_COLS)),
                  table((CHUNK_COLS, CHUNK_COLS)),
                  _layer((STATE_ROWS, LANES), layer), _layer((STATE_ROWS, LANES), layer),
                  table((1, CHUNK_COLS))],
        out_specs=pl.BlockSpec((groups, nchunk, CHUNK_COLS), lambda ph, q: (q * ph, 0, 0)),
        out_shape=jax.ShapeDtypeStruct((SSM_GROUPS, nchunk, CHUNK_COLS), BF16),
        scratch_shapes=[pltpu.VMEM((nchunk * STATE_ROWS, LANES), F32),
                        pltpu.VMEM((nchunk * STATE_ROWS, LANES), F32)],
        compiler_params=_params("arbitrary", "arbitrary"),
        name="ssm",
    )(x, w_in, toep, w_out, a_re, a_im, d_tiled)


FF_CHUNK = D_FF


def _mix_ffn_kernel(x_ref, a_ref, ys_ref, gate_ref, wap_ref, wa_ref, wb_ref, wout_ref,
                    g_ref, wg_ref, wu_ref, wd_ref, out_ref, slab_ref):
    tm = x_ref.shape[0]
    a_out = jnp.dot(a_ref[...], wap_ref[...], preferred_element_type=F32)

    for s in range(N_SLABS):
        for half in range(CHUNK // PIECES):
            ws = [ys_ref[s * PIECES + i, :, half * LANES:(half + 1) * LANES].astype(F32)
                  for i in range(PIECES)]
            for j, v in enumerate(_piece_transpose(ws)):
                slab_ref[s, pl.ds(half * PIECES + j, tm // CHUNK, stride=CHUNK), :] = v
    y = jnp.concatenate([slab_ref[c] for c in range(N_SLABS)], axis=1).astype(BF16)
    s_out = (jnp.dot(y, wa_ref[...], preferred_element_type=F32)
             * jax.nn.sigmoid(jnp.dot(y, wb_ref[...], preferred_element_type=F32)))
    mix = (gate_ref[:, :D_MODEL].astype(F32) * a_out
           + gate_ref[:, D_MODEL:].astype(F32) * s_out)
    x = x_ref[...] + jnp.dot(mix.astype(BF16), wout_ref[...], preferred_element_type=F32)

    h = _rms_norm(x, g_ref[...]).astype(BF16)
    for c in range(D_FF // FF_CHUNK):
        cols = slice(c * FF_CHUNK, (c + 1) * FF_CHUNK)
        gate = jnp.dot(h, wg_ref[:, cols], preferred_element_type=F32)
        up = jnp.dot(h, wu_ref[:, cols], preferred_element_type=F32)
        hidden = (jax.nn.silu(gate) * up).astype(BF16)
        x = x + jnp.dot(hidden, wd_ref[cols, :], preferred_element_type=F32)
    out_ref[...] = x


def _mix_ffn(x, a, ys, gates, wap, wa, wb, wout, g_ffn, wg, wu, wd, layer):
    n = x.shape[0]
    tm = TOKEN_TILE
    row = lambda i: (i, 0)
    return pl.pallas_call(
        _mix_ffn_kernel,
        grid=(n // tm,),
        in_specs=[pl.BlockSpec((tm, D_MODEL), row),
                  pl.BlockSpec((tm, ATTN_WIDTH), row),
                  pl.BlockSpec((SSM_GROUPS, tm // CHUNK, CHUNK_COLS), lambda i: (0, i, 0)),
                  pl.BlockSpec((tm, 2 * D_MODEL), row),
                  _layer((ATTN_WIDTH, D_MODEL), layer), _layer((SSM_WIDTH, D_MODEL), layer),
                  _layer((SSM_WIDTH, D_MODEL), layer), _layer((D_MODEL, D_MODEL), layer),
                  _layer((1, D_MODEL), layer), _layer((D_MODEL, D_FF), layer),
                  _layer((D_MODEL, D_FF), layer), _layer((D_FF, D_MODEL), layer)],
        out_specs=pl.BlockSpec((tm, D_MODEL), row),
        out_shape=jax.ShapeDtypeStruct((n, D_MODEL), F32),
        scratch_shapes=[pltpu.VMEM((N_SLABS, tm, LANES), F32)],
        compiler_params=_params("arbitrary"),
        name="mix_ffn",
    )(x, a, ys, gates, wap, wa, wb, wout, g_ffn, wg, wu, wd)


def kernel(x, g_mix, w_in, g_q, g_k, w_attn_proj, lambda_re, lambda_im, log_dt, b_re, b_im,
           c_re, c_im, d_skip, w_glu_a, w_glu_b, w_out, g_ffn, w_ffn_gate, w_ffn_up, w_ffn_down):
    batch, seq, _ = x.shape
    depth = w_in.shape[0]
    n = batch * seq
    xf = x.reshape(n, D_MODEL).astype(F32)

    blk = jnp.arange(COL_TILE // 2) // HEAD_DIM
    bd = (blk[:, None] == blk[None, :]).astype(BF16)
    gq = jnp.tile(g_q.astype(F32) * (LOG2E * HEAD_DIM ** -0.5), (1, HEADS))[:, None]
    gk = jnp.tile(g_k.astype(F32), (1, HEADS))[:, None]
    toep, s_in, s_out, a_re, a_im = _ssm_tables(lambda_re, lambda_im, log_dt, b_re, b_im,
                                                c_re, c_im)
    d_tiled = jnp.tile(d_skip.astype(F32).reshape(depth, SSM_GROUPS, 1, SSM_GROUP),
                       (1, 1, 1, CHUNK))

    row3 = lambda g: g.astype(F32)[:, None, :]
    g_mix, g_ffn = row3(g_mix), row3(g_ffn)
    (w_in, w_attn_proj, w_glu_a, w_glu_b, w_out, w_ffn_gate, w_ffn_up, w_ffn_down) = (
        w.astype(BF16) for w in (w_in, w_attn_proj, w_glu_a, w_glu_b, w_out,
                                 w_ffn_gate, w_ffn_up, w_ffn_down))

    for l in range(depth):
        q, k, v, xs, gates = _in_proj(xf, g_mix, w_in, gq, gk, bd, l, batch, seq)
        dilated = [_attention_group(q[g], k[g], v[g], batch, seq, g)
                   for g in range(1, N_GROUPS)]
        a = _attention_group(q[0], k[0], v[0], batch, seq, 0, others=dilated)
        ys = _ssm(xs, toep, s_in, s_out, a_re, a_im, d_tiled, l, batch, seq)
        xf = _mix_ffn(xf, a, ys, gates, w_attn_proj, w_glu_a, w_glu_b, w_out, g_ffn,
                      w_ffn_gate, w_ffn_up, w_ffn_down, l)
    return xf.reshape(batch, seq, D_MODEL).astype(x.dtype)
```

```python
import functools

import jax
import jax.numpy as jnp
from jax import lax
from jax.experimental import pallas as pl
from jax.experimental.pallas import tpu as pltpu

F32 = jnp.float32
BF16 = jnp.bfloat16

D_MODEL = 1024
HEAD_DIM = 64
HEADS = 8
ATTN_WIDTH = HEADS * HEAD_DIM
DILATIONS = (1, 4, 16)
N_GROUPS = len(DILATIONS)
BLK = 128
SSM_WIDTH = 512
SSM_GROUP = 16
SSM_GROUPS = 32
SSM_STATE = 64
CHUNK = 16
CHUNK_COLS = CHUNK * SSM_GROUP
D_FF = 2816
IN_COLS = 3 * N_GROUPS * ATTN_WIDTH + SSM_WIDTH + 2 * D_MODEL
EPS = 1e-6
LOG2E = 1.4426950408889634

VMEM_LIMIT_BYTES = 56 * 1024 * 1024
LANES = 128
COL_TILE = 512
TOKEN_TILE = 512
PIECE = SSM_GROUP
PIECES = LANES // PIECE
N_SLABS = ATTN_WIDTH // LANES
STATE_ROWS = SSM_GROUPS // 2
STAGE = 4


def _params(*semantics):
    return pltpu.CompilerParams(dimension_semantics=semantics,
                                vmem_limit_bytes=VMEM_LIMIT_BYTES)


def _resident(shape):
    return pl.BlockSpec(shape, lambda *_: (0,) * len(shape),
                        pipeline_mode=pl.Buffered(1))


def _layer(shape, l):
    return pl.BlockSpec((None,) + shape, lambda *_: (l,) + (0,) * len(shape),
                        pipeline_mode=pl.Buffered(1))


def _rms_norm(x, gain):
    ms = jnp.mean(x * x, axis=-1, keepdims=True)
    return x * lax.rsqrt(ms + EPS) * gain


def _piece_transpose(vs):
    vs = list(vs)
    piece = lax.broadcasted_iota(jnp.int32, vs[0].shape, 1) // PIECE
    for k in (4, 2, 1):
        upper = (piece & k) != 0
        for j in range(PIECES):
            if j & k:
                continue
            a, b = vs[j], vs[j + k]
            vs[j] = jnp.where(upper, pltpu.roll(b, PIECE * k, 1), a)
            vs[j + k] = jnp.where(upper, b, pltpu.roll(a, LANES - PIECE * k, 1))
    return vs


def _in_proj_kernel(x_ref, g_ref, w_ref, gq_ref, gk_ref, bd_ref, *refs):
    qkv_refs = refs[:3 * N_GROUPS]
    xs_ref, gate_ref, h_ref, slab_ref, quad_ref = refs[3 * N_GROUPS:]
    tm = x_ref.shape[0]
    n_slabs = D_MODEL // LANES

    h = _rms_norm(x_ref[...], g_ref[...])
    h_ref[0] = h.astype(BF16)
    for c in range(n_slabs):
        slab_ref[c] = h[:, c * LANES:(c + 1) * LANES]
    for c in range(n_slabs):
        cols = slice(c * LANES, (c + 1) * LANES)
        for rho in range(STAGE):
            quarter = slab_ref[c, pl.ds(rho, tm // STAGE, stride=STAGE), :]
            quad_ref[c, rho] = quarter
            h_ref[1, rho * (tm // STAGE):(rho + 1) * (tm // STAGE), cols] = quarter.astype(BF16)
    for c in range(n_slabs):
        cols = slice(c * LANES, (c + 1) * LANES)
        for r in range(CHUNK):
            rows = quad_ref[c, r % STAGE, pl.ds(r // STAGE, tm // CHUNK, stride=STAGE), :]
            h_ref[2, r * (tm // CHUNK):(r + 1) * (tm // CHUNK), cols] = rows.astype(BF16)
    order = {1: 0, STAGE: 1, STAGE * STAGE: 2}

    def proj(j, d=1):
        return jnp.dot(h_ref[order[d]], w_ref[:, j * COL_TILE:(j + 1) * COL_TILE],
                       preferred_element_type=F32)

    def head_norm(z, gain):
        zz = (z * z).astype(BF16)
        half = COL_TILE // 2
        ss = jnp.concatenate(
            [jnp.dot(zz[:, :half], bd_ref[...], preferred_element_type=F32),
             jnp.dot(zz[:, half:], bd_ref[...], preferred_element_type=F32)], axis=1)
        return z * lax.rsqrt(ss * (1.0 / HEAD_DIM) + EPS) * gain

    def emit(z, out_ref, d):
        for r in range(d):
            out_ref[r] = z[r * (tm // d):(r + 1) * (tm // d)].astype(BF16)

    def attention_tiles(g, d):
        emit(head_norm(proj(g, d), gq_ref[...]), qkv_refs[g], d)
        emit(head_norm(proj(N_GROUPS + g, d), gk_ref[...]), qkv_refs[N_GROUPS + g], d)
        emit(proj(2 * N_GROUPS + g, d), qkv_refs[2 * N_GROUPS + g], d)

    attention_tiles(0, DILATIONS[0])
    for j in range(2 * D_MODEL // COL_TILE):
        cols = slice(j * COL_TILE, (j + 1) * COL_TILE)
        gate_ref[:, cols] = jax.nn.sigmoid(proj(3 * N_GROUPS + 1 + j)).astype(BF16)
    attention_tiles(1, DILATIONS[1])

    u = proj(3 * N_GROUPS, CHUNK)
    per = tm // CHUNK
    for a in range(N_SLABS):
        for half in range(CHUNK // PIECES):
            vs = [u[(half * PIECES + j) * per:(half * PIECES + j + 1) * per,
                    a * LANES:(a + 1) * LANES] for j in range(PIECES)]
            for i, w in enumerate(_piece_transpose(vs)):
                xs_ref[a * PIECES + i, :, half * LANES:(half + 1) * LANES] = w.astype(BF16)
    attention_tiles(2, DILATIONS[2])


def _in_proj(x, g_mix, w_in, gq, gk, bd, layer, batch, seq):
    n = x.shape[0]
    tm = TOKEN_TILE
    tiles = seq // tm
    row = lambda i: (i, 0)
    dil_specs, dil_shapes = [], []
    for _ in range(3):
        for d in DILATIONS:
            dil_specs.append(pl.BlockSpec((None, d, tm // d, ATTN_WIDTH),
                                          lambda i: (i // tiles, 0, i % tiles, 0)))
            dil_shapes.append(jax.ShapeDtypeStruct((batch, d, seq // d, ATTN_WIDTH), BF16))
    outs = pl.pallas_call(
        _in_proj_kernel,
        grid=(n // tm,),
        in_specs=[pl.BlockSpec((tm, D_MODEL), row),
                  _layer((1, D_MODEL), layer),
                  _layer((D_MODEL, IN_COLS), layer),
                  _layer((1, COL_TILE), layer),
                  _layer((1, COL_TILE), layer),
                  _resident((COL_TILE // 2, COL_TILE // 2))],
        out_specs=dil_specs + [
            pl.BlockSpec((SSM_GROUPS, tm // CHUNK, CHUNK_COLS), lambda i: (0, i, 0)),
            pl.BlockSpec((tm, 2 * D_MODEL), row)],
        out_shape=dil_shapes + [
            jax.ShapeDtypeStruct((SSM_GROUPS, n // CHUNK, CHUNK_COLS), BF16),
            jax.ShapeDtypeStruct((n, 2 * D_MODEL), BF16)],
        scratch_shapes=[pltpu.VMEM((3, tm, D_MODEL), BF16),
                        pltpu.VMEM((D_MODEL // LANES, tm, LANES), F32),
                        pltpu.VMEM((D_MODEL // LANES, STAGE, tm // STAGE, LANES), F32)],
        compiler_params=_params("arbitrary"),
        name="in_proj",
    )(x, g_mix, w_in, gq, gk, bd)
    return outs[0:3], outs[3:6], outs[6:9], outs[9], outs[10]


ATTN_BLOCKS = {1: 16, 4: 4, 16: 1}
ATTN_RESIDUES = {1: 1, 4: 4, 16: 16}


def _attn_kernel(q_ref, k_ref, v_ref, *refs, dilation, nb, nr, merge):
    if merge:
        other = (refs[0], refs[1]), (refs[2], refs[3])
        a_ref, kbuf, vbuf = refs[4:]
        stage_refs = ()
    else:
        o_ref, lse_ref, kbuf, vbuf = refs[:4]
        stage_refs = refs[4:]
    step = pl.program_id(1)
    last = slice((nb - 1) * BLK, nb * BLK)

    qi = lax.broadcasted_iota(jnp.int32, (BLK, 2 * BLK), 0)
    kj = lax.broadcasted_iota(jnp.int32, (BLK, 2 * BLK), 1)
    neg = jnp.full((BLK, 2 * BLK), -jnp.inf, F32)
    zero = jnp.zeros((BLK, 2 * BLK), F32)
    cur_bias = jnp.where(kj - BLK <= qi, zero, neg)
    band = jnp.where(kj < BLK, jnp.where(kj >= qi, zero, neg), cur_bias)
    head = jnp.where(kj < BLK, jnp.where(step > 0, band, neg), cur_bias)
    band = jnp.concatenate([band, band], axis=0)
    head = jnp.concatenate([head, head], axis=0)

    lane_q = lax.broadcasted_iota(jnp.int32, (BLK, LANES), 1) < HEAD_DIM
    lane_kv = lax.broadcasted_iota(jnp.int32, (2 * BLK, LANES), 1) < HEAD_DIM
    head_of_row = lax.broadcasted_iota(jnp.int32, (4 * BLK, LANES), 0) // (2 * BLK)
    head_of_lane = lax.broadcasted_iota(jnp.int32, (4 * BLK, LANES), 1) // HEAD_DIM
    den_cols = jnp.where(head_of_row == head_of_lane, 1.0, 0.0).astype(BF16)

    residues = [rr if nr == dilation else pl.program_id(2) * nr + rr for rr in range(nr)]

    @pl.when(step == 0)
    def _():
        for r in residues:
            kbuf[r, 0:BLK, :] = jnp.zeros((BLK, ATTN_WIDTH), BF16)
            vbuf[r, 0:BLK, :] = jnp.zeros((BLK, ATTN_WIDTH), BF16)

    for rr, r in enumerate(residues):
        kb = kbuf.at[r]
        vb = vbuf.at[r]
        kb[BLK:2 * BLK, :] = k_ref[rr, 0:BLK, :]
        vb[BLK:2 * BLK, :] = v_ref[rr, 0:BLK, :]

        for j in range(nb):
            for p in range(HEADS // 2):
                cols = slice(p * LANES, (p + 1) * LANES)
                if j == 0:
                    keys, vals, bias = kb[:, cols], vb[:, cols], head
                else:
                    window = slice((j - 1) * BLK, (j + 1) * BLK)
                    keys, vals, bias = k_ref[rr, window, cols], v_ref[rr, window, cols], band
                qp = q_ref[rr, j * BLK:(j + 1) * BLK, cols]
                zq = jnp.zeros_like(qp)
                q2 = jnp.concatenate([jnp.where(lane_q, qp, zq), jnp.where(lane_q, zq, qp)],
                                     axis=0)
                s = lax.dot_general(q2, keys, (((1,), (1,)), ((), ())),
                                    preferred_element_type=F32) + bias
                m = jnp.max(s, axis=-1, keepdims=True)
                eb = jnp.exp2(s - m).astype(BF16)
                zv = jnp.zeros_like(vals)
                v2 = jnp.concatenate([jnp.where(lane_kv, vals, zv),
                                      jnp.where(lane_kv, zv, vals)], axis=0)
                acc = jnp.dot(jnp.concatenate([eb[:BLK], eb[BLK:]], axis=1),
                              jnp.concatenate([v2, den_cols], axis=1),
                              preferred_element_type=F32)
                den_lanes = acc[:, LANES:]
                m_lanes = jnp.where(lane_q, m[:BLK], m[BLK:])
                lse = m_lanes + jnp.log2(den_lanes)
                o = acc[:, :LANES] * (1.0 / den_lanes)
                if merge:
                    rows = slice(j * BLK, (j + 1) * BLK)
                    (o_a, l_a), (o_b, l_b) = [(o_r[p, rows, :], l_r[p, rows, :])
                                              for o_r, l_r in other]
                    top = jnp.maximum(jnp.maximum(lse, l_a), l_b)
                    e0, e1, e2 = jnp.exp2(lse - top), jnp.exp2(l_a - top), jnp.exp2(l_b - top)
                    a_ref[rows, cols] = ((e0 * o + e1 * o_a + e2 * o_b)
                                         * (1.0 / (e0 + e1 + e2))).astype(BF16)
                elif stage_refs:
                    rows = pl.ds(r // STAGE, BLK, stride=STAGE)
                    stage_refs[0][p, r % STAGE, rows, :] = o
                    stage_refs[1][p, r % STAGE, rows, :] = lse
                else:
                    rows = pl.ds(j * BLK * dilation + r, BLK, stride=dilation)
                    o_ref[p, rows, :] = o
                    lse_ref[p, rows, :] = lse

        kb[0:BLK, :] = k_ref[rr, last, :]
        vb[0:BLK, :] = v_ref[rr, last, :]

    if not merge and stage_refs:
        for staged, out_ref in zip(stage_refs, (o_ref, lse_ref)):
            for p in range(HEADS // 2):
                for rho in range(STAGE):
                    out_ref[p, pl.ds(rho, STAGE * BLK, stride=STAGE), :] = staged[p, rho]


def _attention_group(q, k, v, batch, seq, group, others=None):
    d = DILATIONS[group]
    nb, nr = ATTN_BLOCKS[d], ATTN_RESIDUES[d]
    steps = seq // d // (BLK * nb)
    merge = others is not None
    two_pass = d == STAGE * STAGE and nr == d and nb == 1
    staging = [pltpu.VMEM((N_SLABS, STAGE, STAGE * BLK, LANES), F32)] * 2 if two_pass else []
    in_spec = pl.BlockSpec((None, nr, BLK * nb, ATTN_WIDTH), lambda b, n, r: (b, r, n, 0))
    slab_spec = pl.BlockSpec((None, N_SLABS, BLK * nb * d, LANES), lambda b, n, r: (b, 0, n, 0))
    slab_shape = jax.ShapeDtypeStruct((batch, N_SLABS, seq, LANES), F32)
    if merge:
        assert d == 1
        extra = [t for pair in others for t in pair]
        out_specs = pl.BlockSpec((BLK * nb, ATTN_WIDTH), lambda b, n, r: (b * steps + n, 0))
        out_shape = jax.ShapeDtypeStruct((batch * seq, ATTN_WIDTH), BF16)
    else:
        extra = []
        out_specs = [slab_spec, slab_spec]
        out_shape = [slab_shape, slab_shape]
    return pl.pallas_call(
        functools.partial(_attn_kernel, dilation=d, nb=nb, nr=nr, merge=merge),
        grid=(batch, steps, d // nr),
        in_specs=[in_spec, in_spec, in_spec] + [slab_spec] * len(extra),
        out_specs=out_specs,
        out_shape=out_shape,
        scratch_shapes=[pltpu.VMEM((d, 2 * BLK, ATTN_WIDTH), BF16),
                        pltpu.VMEM((d, 2 * BLK, ATTN_WIDTH), BF16)] + staging,
        compiler_params=_params("arbitrary", "arbitrary", "arbitrary"),
        name=f"attn_d{d}",
    )(q, k, v, *extra)


def _ssm_tables(lam_re, lam_im, log_dt, b_re, b_im, c_re, c_im):
    hi = lax.Precision.HIGHEST
    depth = lam_re.shape[0]
    lr = lam_re.astype(F32)
    li = lam_im.astype(F32)
    dt = jnp.exp(log_dt.astype(F32))[..., None]
    mag = jnp.exp(lr * dt)
    ang = li * dt
    abar_re = mag * jnp.cos(ang)
    abar_im = mag * jnp.sin(ang)
    nr = abar_re - 1.0
    ni = abar_im
    den = lr * lr + li * li
    cr = ((nr * lr + ni * li) / den)[:, :, None, :]
    ci = ((ni * lr - nr * li) / den)[:, :, None, :]
    brt = b_re.astype(F32).transpose(0, 1, 3, 2)
    bit = b_im.astype(F32).transpose(0, 1, 3, 2)
    bbar_re = cr * brt - ci * bit
    bbar_im = cr * bit + ci * brt

    def powers(tau):
        tau = tau.astype(F32)[None, None, :, None]
        pmag = jnp.exp((lr * dt)[:, :, None, :] * tau)
        pang = ang[:, :, None, :] * tau
        return pmag * jnp.cos(pang), pmag * jnp.sin(pang)

    pw_re, pw_im = powers(jnp.arange(CHUNK + 1))

    wide = (CHUNK + 1) * SSM_GROUP
    lane = jnp.arange(wide)
    rep = (lane[None, :] // SSM_GROUP == jnp.arange(CHUNK + 1)[:, None]).astype(F32)
    til = (lane[None, :] % SSM_GROUP == jnp.arange(SSM_GROUP)[:, None]).astype(F32)
    pr_l = jnp.einsum('dgtp,tl->dgpl', pw_re, rep, precision=hi)
    pi_l = jnp.einsum('dgtp,tl->dgpl', pw_im, rep, precision=hi)
    cr_l = jnp.einsum('dgcp,cl->dgpl', c_re.astype(F32), til, precision=hi)
    ci_l = jnp.einsum('dgcp,cl->dgpl', c_im.astype(F32), til, precision=hi)
    cp_re = cr_l * pr_l - ci_l * pi_l
    cp_im = cr_l * pi_l + ci_l * pr_l

    lag = (jnp.einsum('dgcp,dgpl->dgcl', bbar_re, cp_re[..., :CHUNK_COLS], precision=hi)
           - jnp.einsum('dgcp,dgpl->dgcl', bbar_im, cp_im[..., :CHUNK_COLS], precision=hi))
    col = jnp.arange(CHUNK_COLS)
    shift = (col[None, None, :] == col[None, :, None]
             + PIECE * jnp.arange(CHUNK)[:, None, None]).astype(BF16)
    toep = jnp.einsum('dgcl,slm->dgscm', lag.astype(BF16), shift)
    toep = toep.reshape(depth, SSM_GROUPS, CHUNK_COLS, CHUNK_COLS)

    rev_re, rev_im = powers(CHUNK - 1 - jnp.arange(CHUNK))
    rev_re = rev_re[:, :, :, None, :]
    rev_im = rev_im[:, :, :, None, :]
    win_re = (rev_re * bbar_re[:, :, None] - rev_im * bbar_im[:, :, None])
    win_im = (rev_re * bbar_im[:, :, None] + rev_im * bbar_re[:, :, None])
    win_re = win_re.reshape(depth, SSM_GROUPS, CHUNK_COLS, SSM_STATE)
    win_im = win_im.reshape(depth, SSM_GROUPS, CHUNK_COLS, SSM_STATE)

    wout_re = cp_re[..., SSM_GROUP:]
    wout_im = -cp_im[..., SSM_GROUP:]

    odd = (jnp.arange(SSM_GROUPS) % 2 == 1)[None, :, None, None]
    zc = jnp.zeros_like(win_re)
    w_in = jnp.concatenate([jnp.where(odd, zc, win_re), jnp.where(odd, win_re, zc),
                            jnp.where(odd, zc, win_im), jnp.where(odd, win_im, zc)], axis=3)
    zr = jnp.zeros_like(wout_re)
    w_out = jnp.concatenate([jnp.where(odd, zr, wout_re), jnp.where(odd, wout_re, zr),
                             jnp.where(odd, zr, wout_im), jnp.where(odd, wout_im, zr)], axis=2)
    a_re, a_im = abar_re, abar_im
    for _ in range(CHUNK.bit_length() - 1):
        a_re, a_im = a_re * a_re - a_im * a_im, 2.0 * a_re * a_im
    a_re = a_re.reshape(depth, STATE_ROWS, LANES)
    a_im = a_im.reshape(depth, STATE_ROWS, LANES)
    return toep, w_in.astype(BF16), w_out.astype(BF16), a_re, a_im


SSM_PAIRS = 4


def _ssm_kernel(x_ref, win_ref, toep_ref, wout_ref, are_ref, aim_ref, d_ref, y_ref,
                sre_ref, sim_ref, *, batch, per_batch):
    phase = pl.program_id(0)
    step = pl.program_id(1)
    nchunk = batch * per_batch
    pair_rows = [pl.ds(step * SSM_PAIRS + i, nchunk, stride=STATE_ROWS)
                 for i in range(SSM_PAIRS)]

    @pl.when(phase == 0)
    def _():
        for i, rows in enumerate(pair_rows):
            v = (jnp.dot(x_ref[2 * i], win_ref[2 * i], preferred_element_type=F32)
                 + jnp.dot(x_ref[2 * i + 1], win_ref[2 * i + 1], preferred_element_type=F32))
            sre_ref[rows, :] = v[:, :LANES]
            sim_ref[rows, :] = v[:, LANES:]

    @pl.when((phase == 0) & (step == pl.num_programs(1) - 1))
    def _():
        ar = are_ref[...]
        ai = aim_ref[...]

        def body(k, carry):
            new = []
            for b in range(batch):
                sre, sim = carry[b]
                rows = pl.ds(pl.multiple_of((b * per_batch + k) * STATE_ROWS, STATE_ROWS),
                             STATE_ROWS)
                vre = sre_ref[rows, :]
                vim = sim_ref[rows, :]
                sre_ref[rows, :] = sre
                sim_ref[rows, :] = sim
                new.append((ar * sre - ai * sim + vre, ar * sim + ai * sre + vim))
            return tuple(new)

        zero = jnp.zeros((STATE_ROWS, LANES), F32)
        lax.fori_loop(0, per_batch, body, tuple((zero, zero) for _ in range(batch)))

    @pl.when(phase == 1)
    def _():
        for i, rows in enumerate(pair_rows):
            sp = jnp.concatenate([sre_ref[rows, :], sim_ref[rows, :]], axis=1).astype(BF16)
            for h in range(2 * i, 2 * i + 2):
                x = x_ref[h]
                y = (jnp.dot(x, toep_ref[h], preferred_element_type=F32)
                     + jnp.dot(sp, wout_ref[h], preferred_element_type=F32)
                     + d_ref[h] * x.astype(F32))
                y_ref[h] = jax.nn.gelu(y).astype(BF16)


def _ssm(x, toep, w_in, w_out, a_re, a_im, d_tiled, layer, batch, seq):
    nchunk = x.shape[1]
    per_batch = seq // CHUNK
    groups = 2 * SSM_PAIRS
    table = lambda shape: pl.BlockSpec((None, groups) + shape, lambda ph, q: (layer, q, 0, 0))
    return pl.pallas_call(
        functools.partial(_ssm_kernel, batch=batch, per_batch=per_batch),
        grid=(2, SSM_GROUPS // groups),
        in_specs=[pl.BlockSpec((groups, nchunk, CHUNK_COLS), lambda ph, q: (q, 0, 0)),
                  table((CHUNK_COLS, CHUNK_COLS)), table((CHUNK_COLS, CHUNK_COLS)),
                  table((CHUNK_COLS, CHUNK_COLS)),
                  _layer((STATE_ROWS, LANES), layer), _layer((STATE_ROWS, LANES), layer),
                  table((1, CHUNK_COLS))],
        out_specs=pl.BlockSpec((groups, nchunk, CHUNK_COLS), lambda ph, q: (q * ph, 0, 0)),
        out_shape=jax.ShapeDtypeStruct((SSM_GROUPS, nchunk, CHUNK_COLS), BF16),
        scratch_shapes=[pltpu.VMEM((nchunk * STATE_ROWS, LANES), F32),
                        pltpu.VMEM((nchunk * STATE_ROWS, LANES), F32)],
        compiler_params=_params("arbitrary", "arbitrary"),
        name="ssm",
    )(x, w_in, toep, w_out, a_re, a_im, d_tiled)


FF_CHUNK = D_FF


def _mix_ffn_kernel(x_ref, a_ref, ys_ref, gate_ref, wap_ref, wa_ref, wb_ref, wout_ref,
                    g_ref, wg_ref, wu_ref, wd_ref, out_ref, slab_ref):
    tm = x_ref.shape[0]
    a_out = jnp.dot(a_ref[...], wap_ref[...], preferred_element_type=F32)

    for s in range(N_SLABS):
        for half in range(CHUNK // PIECES):
            ws = [ys_ref[s * PIECES + i, :, half * LANES:(half + 1) * LANES].astype(F32)
                  for i in range(PIECES)]
            for j, v in enumerate(_piece_transpose(ws)):
                slab_ref[s, pl.ds(half * PIECES + j, tm // CHUNK, stride=CHUNK), :] = v
    y = jnp.concatenate([slab_ref[c] for c in range(N_SLABS)], axis=1).astype(BF16)
    s_out = (jnp.dot(y, wa_ref[...], preferred_element_type=F32)
             * jax.nn.sigmoid(jnp.dot(y, wb_ref[...], preferred_element_type=F32)))
    mix = (gate_ref[:, :D_MODEL].astype(F32) * a_out
           + gate_ref[:, D_MODEL:].astype(F32) * s_out)
    x = x_ref[...] + jnp.dot(mix.astype(BF16), wout_ref[...], preferred_element_type=F32)

    h = _rms_norm(x, g_ref[...]).astype(BF16)
    for c in range(D_FF // FF_CHUNK):
        cols = slice(c * FF_CHUNK, (c + 1) * FF_CHUNK)
        gate = jnp.dot(h, wg_ref[:, cols], preferred_element_type=F32)
        up = jnp.dot(h, wu_ref[:, cols], preferred_element_type=F32)
        hidden = (jax.nn.silu(gate) * up).astype(BF16)
        x = x + jnp.dot(hidden, wd_ref[cols, :], preferred_element_type=F32)
    out_ref[...] = x


def _mix_ffn(x, a, ys, gates, wap, wa, wb, wout, g_ffn, wg, wu, wd, layer):
    n = x.shape[0]
    tm = TOKEN_TILE
    row = lambda i: (i, 0)
    return pl.pallas_call(
        _mix_ffn_kernel,
        grid=(n // tm,),
        in_specs=[pl.BlockSpec((tm, D_MODEL), row),
                  pl.BlockSpec((tm, ATTN_WIDTH), row),
                  pl.BlockSpec((SSM_GROUPS, tm // CHUNK, CHUNK_COLS), lambda i: (0, i, 0)),
                  pl.BlockSpec((tm, 2 * D_MODEL), row),
                  _layer((ATTN_WIDTH, D_MODEL), layer), _layer((SSM_WIDTH, D_MODEL), layer),
                  _layer((SSM_WIDTH, D_MODEL), layer), _layer((D_MODEL, D_MODEL), layer),
                  _layer((1, D_MODEL), layer), _layer((D_MODEL, D_FF), layer),
                  _layer((D_MODEL, D_FF), layer), _layer((D_FF, D_MODEL), layer)],
        out_specs=pl.BlockSpec((tm, D_MODEL), row),
        out_shape=jax.ShapeDtypeStruct((n, D_MODEL), F32),
        scratch_shapes=[pltpu.VMEM((N_SLABS, tm, LANES), F32)],
        compiler_params=_params("arbitrary"),
        name="mix_ffn",
    )(x, a, ys, gates, wap, wa, wb, wout, g_ffn, wg, wu, wd)


def kernel(x, g_mix, w_in, g_q, g_k, w_attn_proj, lambda_re, lambda_im, log_dt, b_re, b_im,
           c_re, c_im, d_skip, w_glu_a, w_glu_b, w_out, g_ffn, w_ffn_gate, w_ffn_up, w_ffn_down):
    batch, seq, _ = x.shape
    depth = w_in.shape[0]
    n = batch * seq
    xf = x.reshape(n, D_MODEL).astype(F32)

    blk = jnp.arange(COL_TILE // 2) // HEAD_DIM
    bd = (blk[:, None] == blk[None, :]).astype(BF16)
    gq = jnp.tile(g_q.astype(F32) * (LOG2E * HEAD_DIM ** -0.5), (1, HEADS))[:, None]
    gk = jnp.tile(g_k.astype(F32), (1, HEADS))[:, None]
    toep, s_in, s_out, a_re, a_im = _ssm_tables(lambda_re, lambda_im, log_dt, b_re, b_im,
                                                c_re, c_im)
    d_tiled = jnp.tile(d_skip.astype(F32).reshape(depth, SSM_GROUPS, 1, SSM_GROUP),
                       (1, 1, 1, CHUNK))

    row3 = lambda g: g.astype(F32)[:, None, :]
    g_mix, g_ffn = row3(g_mix), row3(g_ffn)
    (w_in, w_attn_proj, w_glu_a, w_glu_b, w_out, w_ffn_gate, w_ffn_up, w_ffn_down) = (
        w.astype(BF16) for w in (w_in, w_attn_proj, w_glu_a, w_glu_b, w_out,
                                 w_ffn_gate, w_ffn_up, w_ffn_down))

    for l in range(depth):
        q, k, v, xs, gates = _in_proj(xf, g_mix, w_in, gq, gk, bd, l, batch, seq)
        dilated = [_attention_group(q[g], k[g], v[g], batch, seq, g)
                   for g in range(1, N_GROUPS)]
        a = _attention_group(q[0], k[0], v[0], batch, seq, 0, others=dilated)
        ys = _ssm(xs, toep, s_in, s_out, a_re, a_im, d_tiled, l, batch, seq)
        xf = _mix_ffn(xf, a, ys, gates, w_attn_proj, w_glu_a, w_glu_b, w_out, g_ffn,
                      w_ffn_gate, w_ffn_up, w_ffn_down, l)
    return xf.reshape(batch, seq, D_MODEL).astype(x.dtype)
```
